```python
import math
import jax
import jax.numpy as jnp
from jax import lax
import numpy as np

D_MODEL = 2048
BATCH = 4
SEQ = 4096
DEPTH = 4

CTX_LEN = 256
GRID_W = 64

N_BRANCH = 4
BRANCH_W = 512
D_FF = 5632
FFN_RES = 0.5
N_MOD = 9
NORM_EPS = 1e-6

MLA_HEADS = 4
MLA_NOPE = 128
MLA_ROPE = 64
MLA_V = 128
MLA_Q_LORA = 768
MLA_KV_LORA = 512
MLA_SCALE = (MLA_NOPE + MLA_ROPE) ** -0.5
Q_BLOCK = 128
ROPE_BASE = 10000.0

RWKV_HEAD = 64
RWKV_HEADS = BRANCH_W // RWKV_HEAD
RWKV_DECAY_LORA = 64
RWKV_A_LORA = 64
RWKV_G_LORA = 128
RWKV_GN_EPS = 64e-5

S5_GROUP = 16
S5_GROUPS = BRANCH_W // S5_GROUP
S5_STATE = 64

LRU_BLOCKS = 8
LRU_BLOCK = BRANCH_W // LRU_BLOCKS
LRU_CONV = 4
LRU_C = 8.0

MLA_IN = MLA_Q_LORA + MLA_KV_LORA + MLA_ROPE
RWKV_IN = 3 * BRANCH_W + RWKV_DECAY_LORA + RWKV_A_LORA + RWKV_G_LORA
S5_IN = BRANCH_W
LRU_IN = 2 * BRANCH_W
O_RWKV = MLA_IN
O_S5 = O_RWKV + RWKV_IN
O_LRU = O_S5 + S5_IN
MIX_IN = O_LRU + LRU_IN
N_IN = MIX_IN + N_BRANCH * D_MODEL
RWKV_SPLITS = [BRANCH_W, 2 * BRANCH_W, 3 * BRANCH_W, 3 * BRANCH_W + RWKV_DECAY_LORA,
               3 * BRANCH_W + RWKV_DECAY_LORA + RWKV_A_LORA]

kernel_name = 'hybrid_mla_rwkv7_s5_rglru_dit_block'

F32 = jnp.float32


def _rms(x, w):
    xf = x.astype(F32)
    y = xf * lax.rsqrt(jnp.mean(xf * xf, axis=-1, keepdims=True) + NORM_EPS)
    return (y * w.astype(F32)).astype(x.dtype)


def _swiglu(h, w1, w3, w2):
    return (jax.nn.silu(h @ w1) * (h @ w3)) @ w2


def _ffn_half(s, shift, scale, gate, g_pre, g_post, w1, w3, w2):
    h = _rms(s, g_pre) * (1 + scale) + shift
    return s + FFN_RES * gate * _rms(_swiglu(h, w1, w3, w2), g_post)


def _dirstack(t):
    return jnp.stack([t, jnp.flip(t, 1)], 0)


def _dir_flip(t):
    return jnp.stack([t[0], jnp.flip(t[1], 1)], 0)


def _axial_rope(T):
    rows = T // GRID_W
    row = jnp.repeat(jnp.arange(rows, dtype=F32), GRID_W)
    col = jnp.tile(jnp.arange(GRID_W, dtype=F32), rows)
    n_pair = MLA_ROPE // 4
    inv = ROPE_BASE ** (-jnp.arange(n_pair, dtype=F32) / n_pair)
    ang = jnp.concatenate([row[:, None] * inv, col[:, None] * inv], -1)
    return jnp.cos(ang), jnp.sin(ang)


def _rope(x, cos, sin):
    xp = x.astype(F32).reshape(x.shape[:-1] + (-1, 2))
    x1, x2 = xp[..., 0], xp[..., 1]
    out = jnp.stack([x1 * cos - x2 * sin, x1 * sin + x2 * cos], -1)
    return out.reshape(x.shape).astype(x.dtype)


def _mla_project(zm, q_norm, w_uq, kv_norm, w_ukv):
    B, T, _ = zm.shape
    cq = zm[..., :MLA_Q_LORA]
    ckv = zm[..., MLA_Q_LORA:MLA_Q_LORA + MLA_KV_LORA]
    kr = zm[..., MLA_Q_LORA + MLA_KV_LORA:]
    q = (_rms(cq, q_norm) @ w_uq).reshape(B, T, MLA_HEADS, MLA_NOPE + MLA_ROPE)
    kv = (_rms(ckv, kv_norm) @ w_ukv).reshape(B, T, MLA_HEADS, MLA_NOPE + MLA_V)
    return q[..., :MLA_NOPE], q[..., MLA_NOPE:], kv[..., :MLA_NOPE], kr, kv[..., MLA_NOPE:]


def _mla_attend(qn, qr, kn, kr, v):
    s = (jnp.einsum('bqhd,bkhd->bhqk', qn, kn, preferred_element_type=F32)
         + jnp.einsum('bqhr,bkr->bhqk', qr, kr, preferred_element_type=F32)) * MLA_SCALE
    p = jax.nn.softmax(s, axis=-1).astype(v.dtype)
    return jnp.einsum('bhqk,bkhd->bqhd', p, v)


def _mla_mixer(zm, zmc, q_norm, w_uq, kv_norm, w_ukv, cos, sin, with_ctx_out):
    B, T, _ = zm.shape
    qn, qr, kn, kr, v = _mla_project(zm, q_norm, w_uq, kv_norm, w_ukv)
    qr = _rope(qr, cos[:, None], sin[:, None])
    kr = _rope(kr, cos, sin)
    qnc, qrc, knc, krc, vc = _mla_project(zmc, q_norm, w_uq, kv_norm, w_ukv)
    kn_all = jnp.concatenate([kn, knc], 1)
    kr_all = jnp.concatenate([kr, krc], 1)
    v_all = jnp.concatenate([v, vc], 1)
    nb = T // Q_BLOCK

    def blocks(t):
        return jnp.moveaxis(t.reshape((B, nb, Q_BLOCK) + t.shape[2:]), 1, 0)

    o = lax.map(lambda qb: _mla_attend(qb[0], qb[1], kn_all, kr_all, v_all), (blocks(qn), blocks(qr)))
    o = jnp.moveaxis(o, 0, 1).reshape(B, T, MLA_HEADS * MLA_V)
    oc = None
    if with_ctx_out:
        oc = _mla_attend(qnc, qrc, knc, krc, vc).reshape(B, -1, MLA_HEADS * MLA_V)
    return o, oc


def _shift_centred(z, mu_prev, mu_next):
    zp = jnp.pad(z[:, :-1], ((0, 0), (1, 0), (0, 0)))
    zn = jnp.pad(z[:, 1:], ((0, 0), (0, 1), (0, 0)))
    return z + (zp - z) * mu_prev + (zn - z) * mu_next


def _rwkv_inputs(z, mu, w0, w2, a0, a2, g2, k_k, k_a):
    B, T, _ = z.shape
    z = _shift_centred(z, mu[0], mu[1])
    r, k, v, wl, al, gl = jnp.split(z, RWKV_SPLITS, axis=-1)
    r, k, v = r.astype(F32), k.astype(F32), v.astype(F32)
    heads = lambda t: t.reshape(t.shape[:-1] + (RWKV_HEADS, RWKV_HEAD))
    w_pre = (w0[:, None, None, :] + jnp.einsum('btl,dlc->dbtc', jnp.tanh(wl), w2)).astype(F32)
    decay = jnp.exp(-jnp.exp(-jax.nn.softplus(-w_pre) - 0.5))
    a = jax.nn.sigmoid((a0[:, None, None, :] + jnp.einsum('btl,dlc->dbtc', al, a2)).astype(F32))
    kk = heads(k * k_k.astype(F32))
    kk = kk * lax.rsqrt(jnp.sum(kk * kk, -1, keepdims=True) + 1e-12)
    kd = heads(k[None] * (1 + (a - 1) * k_a.astype(F32)))
    g = (jax.nn.sigmoid(gl) @ g2).astype(F32)
    return heads(r), heads(decay), kd, heads(v), kk, heads(a), g


def _rwkv_scan(S0, r, w, k, v, aa, bb):
    def step(S, inp):
        r_t, w_t, k_t, v_t, a_t, b_t = inp
        sa = jnp.einsum('dbhij,dbhj->dbhi', S, a_t)
        S = S * w_t[..., None, :] + sa[..., :, None] * b_t[..., None, :] + v_t[..., :, None] * k_t[..., None, :]
        return S, jnp.einsum('dbhij,dbhj->dbhi', S, r_t)
    xs = tuple(jnp.moveaxis(t, 2, 0) for t in (r, w, k, v, aa, bb))
    S, y = lax.scan(step, S0, xs)
    return S, jnp.moveaxis(y, 0, 2)


def _rwkv_bidir(S0, r, w, kd, v, kk, a):
    S, y = _rwkv_scan(S0, _dirstack(r), _dir_flip(w), _dir_flip(kd), _dirstack(v),
                      -_dirstack(kk), _dir_flip(kk[None] * a))
    return S, y[0] + jnp.flip(y[1], 1)


def _rwkv_out(y, r, kd, v, g, rk, ln_w, ln_b):
    B, T = y.shape[:2]
    mean = jnp.mean(y, -1, keepdims=True)
    var = jnp.mean(jnp.square(y - mean), -1, keepdims=True)
    yn = ((y - mean) * lax.rsqrt(var + RWKV_GN_EPS)).reshape(B, T, BRANCH_W)
    yn = yn * ln_w.astype(F32) + ln_b.astype(F32)
    bonus = jnp.sum(r[None] * kd * rk.astype(F32), -1, keepdims=True) * v[None]
    return (yn + jnp.sum(bonus, 0).reshape(B, T, BRANCH_W)) * g


def _rwkv_mixer(zr, zrc, mu, w0, w2, a0, a2, g2, k_k, k_a, rk, ln_w, ln_b):
    pc = _rwkv_inputs(zrc, mu, w0, w2, a0, a2, g2, k_k, k_a)
    p = _rwkv_inputs(zr, mu, w0, w2, a0, a2, g2, k_k, k_a)
    S0 = jnp.zeros((2, zr.shape[0], RWKV_HEADS, RWKV_HEAD, RWKV_HEAD), F32)
    Sc, yc = _rwkv_bidir(S0, *pc[:6])
    _, y = _rwkv_bidir(Sc, *p[:6])
    out = _rwkv_out(y, p[0], p[2], p[3], p[6], rk, ln_w, ln_b)
    outc = _rwkv_out(yc, pc[0], pc[2], pc[3], pc[6], rk, ln_w, ln_b)
    return out, outc


def _s5_discretise(lam_re, lam_im, log_dt, b_re, b_im):
    lre = jnp.minimum(lam_re.astype(F32), -1e-4)
    lim = lam_im.astype(F32)
    dt = jnp.exp(log_dt.astype(F32))[..., None]
    mag = jnp.exp(lre * dt)
    ar, ai = mag * jnp.cos(lim * dt), mag * jnp.sin(lim * dt)
    den = lre * lre + lim * lim
    nr, ni = ar - 1.0, ai
    cr = (nr * lre + ni * lim) / den
    ci = (ni * lre - nr * lim) / den
    b_re, b_im = b_re.astype(F32), b_im.astype(F32)
    br = cr[..., None] * b_re - ci[..., None] * b_im
    bi = cr[..., None] * b_im + ci[..., None] * b_re
    return ar, ai, br, bi


def _cscan(a_re, a_im, b_re, b_im, h_re, h_im):
    b_re = b_re.at[:, 0].add(a_re * h_re - a_im * h_im)
    b_im = b_im.at[:, 0].add(a_re * h_im + a_im * h_re)
    A_re = jnp.broadcast_to(a_re, b_re.shape)
    A_im = jnp.broadcast_to(a_im, b_im.shape)

    def comb(x, y):
        ar1, ai1, br1, bi1 = x
        ar2, ai2, br2, bi2 = y
        return (ar2 * ar1 - ai2 * ai1, ar2 * ai1 + ai2 * ar1,
                ar2 * br1 - ai2 * bi1 + br2, ar2 * bi1 + ai2 * br1 + bi2)
    _, _, hr, hi = lax.associative_scan(comb, (A_re, A_im, b_re, b_im), axis=1)
    return hr, hi


def _maybe_flip(t, rev):
    return jnp.flip(t, 1) if rev else t


def _s5_mixer(zs, zsc, lam_re, lam_im, log_dt, b_re, b_im, c_re, c_im, d, glu_w, glu_b):
    ar, ai, br, bi = _s5_discretise(lam_re, lam_im, log_dt, b_re, b_im)
    to_groups = lambda t: t.astype(F32).reshape(t.shape[:2] + (S5_GROUPS, S5_GROUP))
    u, uc = to_groups(zs), to_groups(zsc)
    dg = d.astype(F32).reshape(S5_GROUPS, S5_GROUP)
    y, yc = u * dg, uc * dg
    zero = jnp.zeros((zs.shape[0], S5_GROUPS, S5_STATE), F32)
    for di in range(2):
        rev = di == 1
        drive = lambda uu: (jnp.einsum('btgh,gph->btgp', uu, br[di]), jnp.einsum('btgh,gph->btgp', uu, bi[di]))
        read = lambda hr, hi: (jnp.einsum('ghp,btgp->btgh', c_re[di].astype(F32), hr)
                               - jnp.einsum('ghp,btgp->btgh', c_im[di].astype(F32), hi))
        hcr, hci = _cscan(ar[di], ai[di], *drive(_maybe_flip(uc, rev)), zero, zero)
        hr, hi = _cscan(ar[di], ai[di], *drive(_maybe_flip(u, rev)), hcr[:, -1], hci[:, -1])
        yc = yc + _maybe_flip(read(hcr, hci), rev)
        y = y + _maybe_flip(read(hr, hi), rev)

    def glu(t):
        t = jax.nn.gelu(t.reshape(t.shape[:2] + (BRANCH_W,)))
        return t * jax.nn.sigmoid(t @ glu_w.astype(F32) + glu_b.astype(F32))
    return glu(y), glu(yc)


def _conv_centred(x, w, b):
    T = x.shape[1]
    xp = jnp.pad(x, ((0, 0), (2, 1), (0, 0)))
    out = b.astype(F32)
    for j in range(LRU_CONV):
        out = out + xp[:, j:j + T] * w[j].astype(F32)
    return out


def _rscan(a, b, h0):
    b = b.at[:, :, 0].add(a[:, :, 0] * h0)
    _, h = lax.associative_scan(lambda x, y: (x[0] * y[0], y[0] * x[1] + y[1]), (a, b), axis=2)
    return h


def _lru_mixer(zl, zlc, conv_w, conv_b, wa, ba, wx, bx, lam):
    def gates(zz):
        xin, gate = jnp.split(zz.astype(F32), 2, axis=-1)
        xcv = _conv_centred(xin, conv_w, conv_b)
        B, T, _ = xcv.shape
        xb = xcv.reshape(B, T, LRU_BLOCKS, LRU_BLOCK)
        gr = jax.nn.sigmoid(jnp.einsum('btnj,dnjk->dbtnk', xb, wa.astype(F32)).reshape(2, B, T, BRANCH_W)
                            + ba.astype(F32)[:, None, None])
        gi = jax.nn.sigmoid(jnp.einsum('btnj,dnjk->dbtnk', xb, wx.astype(F32)).reshape(2, B, T, BRANCH_W)
                            + bx.astype(F32)[:, None, None])
        log_a = -LRU_C * gr * jax.nn.softplus(-lam.astype(F32))[:, None, None]
        bv = jnp.sqrt(-jnp.expm1(2.0 * log_a)) * gi * xcv[None]
        return _dir_flip(jnp.exp(log_a)), _dir_flip(bv), gate

    ac, bc, gc = gates(zlc)
    a, b, g = gates(zl)
    hc = _rscan(ac, bc, jnp.zeros((2, zl.shape[0], BRANCH_W), F32))
    h = _rscan(a, b, hc[:, :, -1])
    out = (h[0] + jnp.flip(h[1], 1)) * jax.nn.gelu(g)
    outc = (hc[0] + jnp.flip(hc[1], 1)) * jax.nn.gelu(gc)
    return out, outc


def _merge(ys, zg, gate_b, w_branch, w_out):
    out = None
    for k in range(N_BRANCH):
        g = jax.nn.sigmoid(zg[..., k * D_MODEL:(k + 1) * D_MODEL] + gate_b[k])
        term = g * (ys[k].astype(zg.dtype) @ w_branch[k])
        out = term if out is None else out + term
    return out @ w_out


def setup_inputs(seed: int = 0) -> dict:
    key = jax.random.key(seed)
    keys = iter(jax.random.split(key, 64))
    L, D, W, G, P = DEPTH, D_MODEL, BRANCH_W, S5_GROUPS, S5_STATE

    def nrm(shape, scale):
        return scale * jax.random.normal(next(keys), shape, F32)

    def unif(shape, lo, hi):
        return jax.random.uniform(next(keys), shape, F32, lo, hi)

    x = nrm((BATCH, SEQ, D), 1.0)
    c = nrm((BATCH, D), 1.0)
    ctx = nrm((BATCH, CTX_LEN, D), 1.0)
    c_ctx = nrm((D,), 1.0)
    ada_w = nrm((L, D, N_MOD * D), 0.5 * D ** -0.5)
    ada_b = nrm((L, N_MOD * D), 0.01)
    norm_w = 1.0 + nrm((L, 6, D), 0.02)
    ffn_w1 = nrm((L, 2, D, D_FF), D ** -0.5)
    ffn_w3 = nrm((L, 2, D, D_FF), D ** -0.5)
    ffn_w2 = nrm((L, 2, D_FF, D), D_FF ** -0.5)
    w_in = nrm((L, D, N_IN), D ** -0.5)
    gate_b = nrm((L, N_BRANCH, D), 0.01)
    mla_q_norm = 1.0 + nrm((L, MLA_Q_LORA), 0.02)
    mla_w_uq = nrm((L, MLA_Q_LORA, MLA_HEADS * (MLA_NOPE + MLA_ROPE)), MLA_Q_LORA ** -0.5)
    mla_kv_norm = 1.0 + nrm((L, MLA_KV_LORA), 0.02)
    mla_w_ukv = nrm((L, MLA_KV_LORA, MLA_HEADS * (MLA_NOPE + MLA_V)), MLA_KV_LORA ** -0.5)
    rwkv_mu = unif((L, 2, RWKV_IN), 0.0, 0.5)
    rwkv_w0 = jnp.linspace(-6.0, -1.0, W, dtype=F32) + nrm((L, 2, W), 0.1)
    rwkv_w2 = nrm((L, 2, RWKV_DECAY_LORA, W), 0.1 * RWKV_DECAY_LORA ** -0.5)
    rwkv_a0 = nrm((L, 2, W), 0.1)
    rwkv_a2 = nrm((L, 2, RWKV_A_LORA, W), 0.1 * RWKV_A_LORA ** -0.5)
    rwkv_g2 = nrm((L, RWKV_G_LORA, W), RWKV_G_LORA ** -0.5)
    rwkv_kk = 0.85 + nrm((L, W), 0.02)
    rwkv_ka = 1.0 + nrm((L, W), 0.02)
    rwkv_rk = nrm((L, RWKV_HEADS, RWKV_HEAD), 0.1)
    rwkv_ln_w = 1.0 + nrm((L, W), 0.02)
    rwkv_ln_b = nrm((L, W), 0.01)
    s5_lam_re = -0.5 + nrm((L, 2, G, P), 0.01)
    s5_lam_im = math.pi * jnp.arange(P, dtype=F32) + nrm((L, 2, G, P), 0.01)
    s5_log_dt = unif((L, 2, G), math.log(1e-3), math.log(1e-1))
    s5_b_re = nrm((L, 2, G, P, S5_GROUP), (2 * S5_GROUP) ** -0.5)
    s5_b_im = nrm((L, 2, G, P, S5_GROUP), (2 * S5_GROUP) ** -0.5)
    s5_c_re = nrm((L, 2, G, S5_GROUP, P), (2 * P) ** -0.5)
    s5_c_im = nrm((L, 2, G, S5_GROUP, P), (2 * P) ** -0.5)
    s5_d = nrm((L, W), 1.0)
    s5_glu_w = nrm((L, W, W), W ** -0.5)
    s5_glu_b = nrm((L, W), 0.01)
    lru_conv_w = nrm((L, LRU_CONV, W), LRU_CONV ** -0.5)
    lru_conv_b = nrm((L, W), 0.01)
    lru_wa = nrm((L, 2, LRU_BLOCKS, LRU_BLOCK, LRU_BLOCK), LRU_BLOCK ** -0.5)
    lru_ba = nrm((L, 2, W), 0.01)
    lru_wx = nrm((L, 2, LRU_BLOCKS, LRU_BLOCK, LRU_BLOCK), LRU_BLOCK ** -0.5)
    lru_bx = nrm((L, 2, W), 0.01)
    a_target = unif((L, 2, W), 0.9, 0.999) ** (1.0 / LRU_C)
    lru_lam = jnp.log(a_target) - jnp.log1p(-a_target)
    w_branch = nrm((L, N_BRANCH, W, D), W ** -0.5)
    w_out = nrm((L, D, D), D ** -0.5)
    return {'x': x, 'c': c, 'ctx': ctx, 'c_ctx': c_ctx, 'ada_w': ada_w, 'ada_b': ada_b, 'norm_w': norm_w,
            'ffn_w1': ffn_w1, 'ffn_w3': ffn_w3, 'ffn_w2': ffn_w2, 'w_in': w_in, 'gate_b': gate_b,
            'mla_q_norm': mla_q_norm, 'mla_w_uq': mla_w_uq, 'mla_kv_norm': mla_kv_norm, 'mla_w_ukv': mla_w_ukv,
            'rwkv_mu': rwkv_mu, 'rwkv_w0': rwkv_w0, 'rwkv_w2': rwkv_w2, 'rwkv_a0': rwkv_a0, 'rwkv_a2': rwkv_a2,
            'rwkv_g2': rwkv_g2, 'rwkv_kk': rwkv_kk, 'rwkv_ka': rwkv_ka, 'rwkv_rk': rwkv_rk,
            'rwkv_ln_w': rwkv_ln_w, 'rwkv_ln_b': rwkv_ln_b,
            's5_lam_re': s5_lam_re, 's5_lam_im': s5_lam_im, 's5_log_dt': s5_log_dt, 's5_b_re': s5_b_re,
            's5_b_im': s5_b_im, 's5_c_re': s5_c_re, 's5_c_im': s5_c_im, 's5_d': s5_d, 's5_glu_w': s5_glu_w,
            's5_glu_b': s5_glu_b, 'lru_conv_w': lru_conv_w, 'lru_conv_b': lru_conv_b, 'lru_wa': lru_wa,
            'lru_ba': lru_ba, 'lru_wx': lru_wx, 'lru_bx': lru_bx, 'lru_lam': lru_lam,
            'w_branch': w_branch, 'w_out': w_out}


def reference(x, c, ctx, c_ctx, ada_w, ada_b, norm_w, ffn_w1, ffn_w3, ffn_w2, w_in, gate_b,
              mla_q_norm, mla_w_uq, mla_kv_norm, mla_w_ukv,
              rwkv_mu, rwkv_w0, rwkv_w2, rwkv_a0, rwkv_a2, rwkv_g2, rwkv_kk, rwkv_ka, rwkv_rk,
              rwkv_ln_w, rwkv_ln_b,
              s5_lam_re, s5_lam_im, s5_log_dt, s5_b_re, s5_b_im, s5_c_re, s5_c_im, s5_d, s5_glu_w, s5_glu_b,
              lru_conv_w, lru_conv_b, lru_wa, lru_ba, lru_wx, lru_bx, lru_lam,
              w_branch, w_out):
    T = x.shape[1]
    cos, sin = _axial_rope(T)
    xc = ctx
    for l in range(DEPTH):
        ctx_out = l < DEPTH - 1
        mod = (jax.nn.silu(c) @ ada_w[l] + ada_b[l]).reshape(-1, N_MOD, 1, D_MODEL)
        modc = (jax.nn.silu(c_ctx) @ ada_w[l] + ada_b[l]).reshape(1, N_MOD, 1, D_MODEL)
        nw = norm_w[l]

        x = _ffn_half(x, mod[:, 0], mod[:, 1], mod[:, 2], nw[0], nw[1], ffn_w1[l, 0], ffn_w3[l, 0], ffn_w2[l, 0])
        xc = _ffn_half(xc, modc[:, 0], modc[:, 1], modc[:, 2], nw[0], nw[1], ffn_w1[l, 0], ffn_w3[l, 0], ffn_w2[l, 0])

        h = _rms(x, nw[2]) * (1 + mod[:, 4]) + mod[:, 3]
        hc = _rms(xc, nw[2]) * (1 + modc[:, 4]) + modc[:, 3]
        z = h @ w_in[l]
        zc = hc @ (w_in[l] if ctx_out else w_in[l][:, :MIX_IN])

        ya, yac = _mla_mixer(z[..., :O_RWKV], zc[..., :O_RWKV], mla_q_norm[l], mla_w_uq[l],
                             mla_kv_norm[l], mla_w_ukv[l], cos, sin, ctx_out)
        yb, ybc = _rwkv_mixer(z[..., O_RWKV:O_S5], zc[..., O_RWKV:O_S5], rwkv_mu[l], rwkv_w0[l], rwkv_w2[l],
                              rwkv_a0[l], rwkv_a2[l], rwkv_g2[l], rwkv_kk[l], rwkv_ka[l], rwkv_rk[l],
                              rwkv_ln_w[l], rwkv_ln_b[l])
        ys, ysc = _s5_mixer(z[..., O_S5:O_LRU], zc[..., O_S5:O_LRU], s5_lam_re[l], s5_lam_im[l], s5_log_dt[l],
                            s5_b_re[l], s5_b_im[l], s5_c_re[l], s5_c_im[l], s5_d[l], s5_glu_w[l], s5_glu_b[l])
        yd, ydc = _lru_mixer(z[..., O_LRU:MIX_IN], zc[..., O_LRU:MIX_IN], lru_conv_w[l], lru_conv_b[l],
                             lru_wa[l], lru_ba[l], lru_wx[l], lru_bx[l], lru_lam[l])

        m = _merge([ya, yb, ys, yd], z[..., MIX_IN:], gate_b[l], w_branch[l], w_out[l])
        x = x + mod[:, 5] * _rms(m, nw[3])
        x = _ffn_half(x, mod[:, 6], mod[:, 7], mod[:, 8], nw[4], nw[5], ffn_w1[l, 1], ffn_w3[l, 1], ffn_w2[l, 1])

        if ctx_out:
            mc = _merge([yac, ybc, ysc, ydc], zc[..., MIX_IN:], gate_b[l], w_branch[l], w_out[l])
            xc = xc + modc[:, 5] * _rms(mc, nw[3])
            xc = _ffn_half(xc, modc[:, 6], modc[:, 7], modc[:, 8], nw[4], nw[5],
                           ffn_w1[l, 1], ffn_w3[l, 1], ffn_w2[l, 1])
    return x
```

```python
import functools
import math

import jax
import jax.numpy as jnp
from jax import lax
from jax.experimental import pallas as pl
from jax.experimental.pallas import tpu as pltpu

F32 = jnp.float32
BF16 = jnp.bfloat16

D_MODEL = 2048
N_BRANCH = 4
BRANCH_W = 512
D_FF = 5632
FFN_RES = 0.5
N_MOD = 9
NORM_EPS = 1e-6
GRID_W = 64

MLA_HEADS = 4
MLA_NOPE = 128
MLA_ROPE = 64
MLA_V = 128
MLA_Q_LORA = 768
MLA_KV_LORA = 512
MLA_SCALE = (MLA_NOPE + MLA_ROPE) ** -0.5
ROPE_BASE = 10000.0
MLA_HEAD_PAD = 256
MLA_IN_PAD = 1408

RWKV_HEAD = 64
RWKV_HEADS = BRANCH_W // RWKV_HEAD
RWKV_DECAY_LORA = 64
RWKV_A_LORA = 64
RWKV_G_LORA = 128
RWKV_GN_EPS = 64e-5
RWKV_IN = 3 * BRANCH_W + RWKV_DECAY_LORA + RWKV_A_LORA + RWKV_G_LORA

S5_GROUP = 16
S5_GROUPS = BRANCH_W // S5_GROUP
S5_STATE = 64
S5_W = S5_GROUPS * S5_STATE

LRU_BLOCKS = 8
LRU_BLOCK = BRANCH_W // LRU_BLOCKS
LRU_CONV = 4
LRU_C = 8.0

MLA_IN = MLA_Q_LORA + MLA_KV_LORA + MLA_ROPE
O_RWKV = MLA_IN
O_S5 = O_RWKV + RWKV_IN
O_LRU = O_S5 + BRANCH_W
MIX_IN = O_LRU + 2 * BRANCH_W

ROWS = 8
LANE = 128
VMEM_LIMIT = 56 * 1024 * 1024


def _cparams(*sem):
    return pltpu.CompilerParams(dimension_semantics=sem, vmem_limit_bytes=VMEM_LIMIT)


def _rms(x, w):
    return x * lax.rsqrt(jnp.mean(x * x, axis=-1, keepdims=True) + NORM_EPS) * w


def _dot(a, b):
    return jnp.dot(a, b, preferred_element_type=F32)


def _dot_exact(a, b):
    return jnp.dot(a, b, preferred_element_type=F32, precision=lax.Precision.HIGHEST)


def _gelu(x):
    return 0.5 * x * (1.0 + jnp.tanh(math.sqrt(2.0 / math.pi) * (x + 0.044715 * (x * x * x))))


def _mod_kernel(cc_ref, w_ref, b_ref, o_ref):
    cc = cc_ref[...]
    s = (cc * jax.nn.sigmoid(cc)).astype(BF16)
    o_ref[...] = _dot(s, w_ref[...].astype(BF16)) + b_ref[...]


def _modulation(cc, ada_w, ada_b):
    L, D, N = ada_w.shape
    tn = 1024
    return pl.pallas_call(
        _mod_kernel,
        grid=(L, N // tn),
        in_specs=[pl.BlockSpec((ROWS, D), lambda l, j: (0, 0)),
                  pl.BlockSpec((None, D, tn), lambda l, j: (l, 0, j)),
                  pl.BlockSpec((None, 1, tn), lambda l, j: (l, 0, j))],
        out_specs=pl.BlockSpec((None, ROWS, tn), lambda l, j: (l, 0, j)),
        out_shape=jax.ShapeDtypeStruct((L, ROWS, N), F32),
        compiler_params=_cparams("parallel", "parallel"),
        name="modulation",
    )(cc, ada_w, ada_b.reshape(L, 1, N))


class _Rows:
    def __init__(self, n_lat, n_ctx, seq, batch, tm):
        assert seq % tm == 0 and n_ctx % tm == 0
        self.tm = tm
        self.n_tiles = (n_lat + n_ctx) // tm
        self.lat_tiles = n_lat // tm
        self.per_seq = seq // tm
        self.batch = batch

    def mod_row(self, i):
        return jnp.where(i < self.lat_tiles, i // self.per_seq, self.batch)

    def pos_tile(self, i):
        return jnp.where(i < self.lat_tiles, i % self.per_seq, self.per_seq)


def _mod_spec(rows, l, group, extra_grid=0):
    if extra_grid:
        return pl.BlockSpec((None, None, 3, 1, D_MODEL), lambda i, j: (l, rows.mod_row(i), group, 0, 0))
    return pl.BlockSpec((None, None, 3, 1, D_MODEL), lambda i: (l, rows.mod_row(i), group, 0, 0))


def _nw_spec(l, idx, extra_grid=0):
    if extra_grid:
        return pl.BlockSpec((None, None, 1, D_MODEL), lambda i, j: (l, idx, 0, 0))
    return pl.BlockSpec((None, None, 1, D_MODEL), lambda i: (l, idx, 0, 0))


def _ffn_kernel(x_ref, mod_ref, gpre_ref, gpost_ref, w1_ref, w3_ref, w2_ref, o_ref, h_ref, acc_ref):
    j = pl.program_id(1)

    @pl.when(j == 0)
    def _():
        h = _rms(x_ref[...], gpre_ref[...]) * (1.0 + mod_ref[1]) + mod_ref[0]
        h_ref[...] = h.astype(BF16)
        acc_ref[...] = jnp.zeros_like(acc_ref)

    h = h_ref[...]
    a = _dot(h, w1_ref[...])
    b = _dot(h, w3_ref[...])
    u = (a * jax.nn.sigmoid(a) * b).astype(BF16)
    acc_ref[...] += _dot(u, w2_ref[...])

    @pl.when(j == pl.num_programs(1) - 1)
    def _():
        o_ref[...] = x_ref[...] + FFN_RES * mod_ref[2] * _rms(acc_ref[...], gpost_ref[...])


def _ffn_half(x, mods, nw, w1, w3, w2, rows, l, hf, group, i_pre, i_post):
    n, d = x.shape
    tm, tf = rows.tm, 512
    return pl.pallas_call(
        _ffn_kernel,
        grid=(rows.n_tiles, D_FF // tf),
        in_specs=[pl.BlockSpec((tm, d), lambda i, j: (i, 0)),
                  _mod_spec(rows, l, group, 1),
                  _nw_spec(l, i_pre, 1), _nw_spec(l, i_post, 1),
                  pl.BlockSpec((None, None, d, tf), lambda i, j: (l, hf, 0, j)),
                  pl.BlockSpec((None, None, d, tf), lambda i, j: (l, hf, 0, j)),
                  pl.BlockSpec((None, None, tf, d), lambda i, j: (l, hf, j, 0))],
        out_specs=pl.BlockSpec((tm, d), lambda i, j: (i, 0)),
        out_shape=jax.ShapeDtypeStruct((n, d), F32),
        scratch_shapes=[pltpu.VMEM((tm, d), BF16), pltpu.VMEM((tm, d), F32)],
        compiler_params=_cparams("parallel", "arbitrary"),
        name="ffn_half",
    )(x, mods, nw, nw, w1, w3, w2)


def _inproj_kernel(x_ref, mod_ref, g_ref, wa_ref, wb_ref, ws_ref, wd_ref, oa_ref, ob_ref, os_ref, od_ref):
    h = (_rms(x_ref[...], g_ref[...]) * (1.0 + mod_ref[1]) + mod_ref[0]).astype(BF16)
    oa_ref[...] = _dot(h, wa_ref[...])
    ob_ref[...] = _dot(h, wb_ref[...])
    os_ref[...] = _dot(h, ws_ref[...])
    od_ref[...] = _dot(h, wd_ref[...])


def _in_projection(x, mods, nw, ws, rows, l):
    n, d = x.shape
    tm = rows.tm
    widths = [w.shape[-1] for w in ws]
    return pl.pallas_call(
        _inproj_kernel,
        grid=(rows.n_tiles,),
        in_specs=[pl.BlockSpec((tm, d), lambda i: (i, 0)), _mod_spec(rows, l, 1), _nw_spec(l, 2)]
        + [pl.BlockSpec((None, d, wd), lambda i: (l, 0, 0)) for wd in widths],
        out_specs=[pl.BlockSpec((tm, wd), lambda i: (i, 0)) for wd in widths],
        out_shape=[jax.ShapeDtypeStruct((n, wd), F32) for wd in widths],
        compiler_params=_cparams("parallel"),
        name="in_projection",
    )(x, mods, nw, *ws)


def _gate_kernel(x_ref, mod_ref, g_ref, y_ref, wg_ref, gb_ref, wb_ref, o_ref, h_ref, acc_ref):
    k = pl.program_id(1)

    @pl.when(k == 0)
    def _():
        h = _rms(x_ref[...], g_ref[...]) * (1.0 + mod_ref[1]) + mod_ref[0]
        h_ref[...] = h.astype(BF16)
        acc_ref[...] = jnp.zeros_like(acc_ref)

    zg = _dot(h_ref[...], wg_ref[...]) + gb_ref[...]
    acc_ref[...] += jax.nn.sigmoid(zg) * _dot(y_ref[...], wb_ref[...])

    @pl.when(k == pl.num_programs(1) - 1)
    def _():
        o_ref[...] = acc_ref[...].astype(BF16)


def _gated_sum(x, mods, nw, ys, w_gate, gate_b, w_branch, rows, l):
    n, d = x.shape
    tm = rows.tm
    return pl.pallas_call(
        _gate_kernel,
        grid=(rows.n_tiles, N_BRANCH),
        in_specs=[pl.BlockSpec((tm, d), lambda i, k: (i, 0)),
                  _mod_spec(rows, l, 1, 1), _nw_spec(l, 2, 1),
                  pl.BlockSpec((None, tm, BRANCH_W), lambda i, k: (k, i, 0)),
                  pl.BlockSpec((None, d, d), lambda i, k: (l, 0, k)),
                  pl.BlockSpec((None, None, 1, d), lambda i, k: (l, k, 0, 0)),
                  pl.BlockSpec((None, None, BRANCH_W, d), lambda i, k: (l, k, 0, 0))],
        out_specs=pl.BlockSpec((tm, d), lambda i, k: (i, 0)),
        out_shape=jax.ShapeDtypeStruct((n, d), BF16),
        scratch_shapes=[pltpu.VMEM((tm, d), BF16), pltpu.VMEM((tm, d), F32)],
        compiler_params=_cparams("parallel", "arbitrary"),
        name="gated_sum",
    )(x, mods, nw, ys, w_gate, gate_b, w_branch)


def _outproj_kernel(m_ref, w_ref, x_ref, mod_ref, g_ref, o_ref):
    m = _dot(m_ref[...], w_ref[...])
    o_ref[...] = x_ref[...] + mod_ref[2] * _rms(m, g_ref[...])


def _out_projection(m, w_out, x, mods, nw, rows, l):
    n, d = x.shape
    tm = rows.tm
    return pl.pallas_call(
        _outproj_kernel,
        grid=(rows.n_tiles,),
        in_specs=[pl.BlockSpec((tm, d), lambda i: (i, 0)),
                  pl.BlockSpec((None, d, d), lambda i: (l, 0, 0)),
                  pl.BlockSpec((tm, d), lambda i: (i, 0)),
                  _mod_spec(rows, l, 1), _nw_spec(l, 3)],
        out_specs=pl.BlockSpec((tm, d), lambda i: (i, 0)),
        out_shape=jax.ShapeDtypeStruct((n, d), F32),
        compiler_params=_cparams("parallel"),
        name="out_projection",
    )(m, w_out, x, mods, nw)


def _rope_lanes(x, c, s_lo, s_hi, width):
    return x * c + pltpu.roll(x, width - MLA_ROPE // 2, 1) * s_lo + pltpu.roll(x, MLA_ROPE // 2, 1) * s_hi


def _mla_proj_kernel(z_ref, qn_ref, kvn_ref, wq_ref, wkv_ref, qc_ref, qs1_ref, qs2_ref,
                     kc_ref, ks1_ref, ks2_ref, q_ref, k_ref, v_ref):
    z = z_ref[...]
    cq = z[:, :MLA_Q_LORA]
    ckv = z[:, MLA_Q_LORA:MLA_Q_LORA + MLA_KV_LORA]
    kr = z[:, MLA_Q_LORA + MLA_KV_LORA:]
    q = _dot(_rms(cq, qn_ref[...]).astype(BF16), wq_ref[...])
    kv = _dot(_rms(ckv, kvn_ref[...]).astype(BF16), wkv_ref[...])
    qc, qs1, qs2 = qc_ref[...], qs1_ref[...], qs2_ref[...]
    krr = _rope_lanes(kr, kc_ref[...], ks1_ref[...], ks2_ref[...], LANE).astype(BF16)
    for h in range(MLA_HEADS):
        qh = q[:, h * MLA_HEAD_PAD:(h + 1) * MLA_HEAD_PAD]
        q_ref[:, h * MLA_HEAD_PAD:(h + 1) * MLA_HEAD_PAD] = _rope_lanes(qh, qc, qs1, qs2, MLA_HEAD_PAD).astype(BF16)
        k_ref[:, h * MLA_HEAD_PAD:h * MLA_HEAD_PAD + MLA_NOPE] = kv[:, h * 256:h * 256 + MLA_NOPE].astype(BF16)
        k_ref[:, h * MLA_HEAD_PAD + MLA_NOPE:(h + 1) * MLA_HEAD_PAD] = krr
        v_ref[:, h * MLA_V:(h + 1) * MLA_V] = kv[:, h * 256 + MLA_NOPE:(h + 1) * 256].astype(BF16)


def _mla_project(z, qn, kvn, wq, wkv, tabs, rows, l):
    n = z.shape[0]
    tm = rows.tm
    hq = MLA_HEADS * MLA_HEAD_PAD
    tab_q = pl.BlockSpec((tm, MLA_HEAD_PAD), lambda i: (rows.pos_tile(i), 0))
    tab_k = pl.BlockSpec((tm, LANE), lambda i: (rows.pos_tile(i), 0))
    return pl.pallas_call(
        _mla_proj_kernel,
        grid=(rows.n_tiles,),
        in_specs=[pl.BlockSpec((tm, MLA_IN_PAD), lambda i: (i, 0)),
                  pl.BlockSpec((None, 1, MLA_Q_LORA), lambda i: (l, 0, 0)),
                  pl.BlockSpec((None, 1, MLA_KV_LORA), lambda i: (l, 0, 0)),
                  pl.BlockSpec((None, MLA_Q_LORA, hq), lambda i: (l, 0, 0)),
                  pl.BlockSpec((None, MLA_KV_LORA, hq), lambda i: (l, 0, 0)),
                  tab_q, tab_q, tab_q, tab_k, tab_k, tab_k],
        out_specs=[pl.BlockSpec((tm, hq), lambda i: (i, 0)),
                   pl.BlockSpec((tm, hq), lambda i: (i, 0)),
                   pl.BlockSpec((tm, MLA_HEADS * MLA_V), lambda i: (i, 0))],
        out_shape=[jax.ShapeDtypeStruct((n, hq), BF16), jax.ShapeDtypeStruct((n, hq), BF16),
                   jax.ShapeDtypeStruct((n, MLA_HEADS * MLA_V), BF16)],
        compiler_params=_cparams("parallel"),
        name="mla_project",
    )(z, qn, kvn, wq, wkv, *tabs)


def _scores(q, k):
    return lax.dot_general(q, k, (((1,), (1,)), ((), ())), preferred_element_type=F32)


def _attn_kernel(q_ref, k1_ref, v1_ref, k2_ref, v2_ref, o_ref):
    q = q_ref[...]
    s1 = _scores(q, k1_ref[...])
    s2 = _scores(q, k2_ref[...])
    m = jnp.maximum(jnp.max(s1, axis=-1, keepdims=True), jnp.max(s2, axis=-1, keepdims=True))
    p1 = jnp.exp(s1 - m)
    p2 = jnp.exp(s2 - m)
    den = jnp.sum(p1, axis=-1, keepdims=True) + jnp.sum(p2, axis=-1, keepdims=True)
    o = _dot(p1.astype(BF16), v1_ref[...]) + _dot(p2.astype(BF16), v2_ref[...])
    o_ref[...] = (o / den).astype(BF16)


def _attn_ctx_kernel(q_ref, k_ref, v_ref, o_ref):
    s = _scores(q_ref[...], k_ref[...])
    p = jnp.exp(s - jnp.max(s, axis=-1, keepdims=True))
    o = _dot(p.astype(BF16), v_ref[...])
    o_ref[...] = (o / jnp.sum(p, axis=-1, keepdims=True)).astype(BF16)


def _mla_attention(q, k, v, batch, seq, ctx_len, tq):
    n_lat = batch * seq
    cb = n_lat // ctx_len
    qt = seq // tq
    lat = pl.pallas_call(
        _attn_kernel,
        grid=(batch, MLA_HEADS, qt),
        in_specs=[pl.BlockSpec((tq, MLA_HEAD_PAD), lambda b, h, i: (b * qt + i, h)),
                  pl.BlockSpec((seq, MLA_HEAD_PAD), lambda b, h, i: (b, h)),
                  pl.BlockSpec((seq, MLA_V), lambda b, h, i: (b, h)),
                  pl.BlockSpec((ctx_len, MLA_HEAD_PAD), lambda b, h, i: (cb + b, h)),
                  pl.BlockSpec((ctx_len, MLA_V), lambda b, h, i: (cb + b, h))],
        out_specs=pl.BlockSpec((tq, MLA_V), lambda b, h, i: (b * qt + i, h)),
        out_shape=jax.ShapeDtypeStruct((n_lat, MLA_HEADS * MLA_V), BF16),
        compiler_params=_cparams("parallel", "parallel", "parallel"),
        name="mla_attention",
    )(q, k, v, k, v)
    ctx = pl.pallas_call(
        _attn_ctx_kernel,
        grid=(batch, MLA_HEADS),
        in_specs=[pl.BlockSpec((ctx_len, MLA_HEAD_PAD), lambda b, h: (cb + b, h)),
                  pl.BlockSpec((ctx_len, MLA_HEAD_PAD), lambda b, h: (cb + b, h)),
                  pl.BlockSpec((ctx_len, MLA_V), lambda b, h: (cb + b, h))],
        out_specs=pl.BlockSpec((ctx_len, MLA_V), lambda b, h: (b, h)),
        out_shape=jax.ShapeDtypeStruct((batch * ctx_len, MLA_HEADS * MLA_V), BF16),
        compiler_params=_cparams("parallel", "parallel"),
        name="mla_attention_ctx",
    )(q, k, v)
    return jnp.concatenate([lat, ctx], axis=0)


def _backward_rows(shape):
    return lax.broadcasted_iota(jnp.int32, shape, len(shape) - 2) % ROWS >= ROWS // 2


def _by_direction(x, bwd):
    zero = jnp.zeros_like(x)
    return jnp.concatenate([jnp.where(bwd, zero, x), jnp.where(bwd, x, zero)], axis=-1)


def _halo_valid(c, n_chunks, seg_chunk):
    prev_ok = jnp.logical_and(c != 0, c != seg_chunk)
    next_ok = jnp.logical_and(c != seg_chunk - 1, c != n_chunks - 1)
    return prev_ok.astype(F32), next_ok.astype(F32)


def _head_sum(x, blk):
    return _dot_exact(x, blk)


def _rwkv_prep_kernel(seg_chunk, zc_ref, zp_ref, zn_ref, mua_ref, mub_ref, w0_ref, a0_ref, kk_ref, ka_ref,
                      rk_ref, wl_ref, g2_ref, blk_ref,
                      w_out, kk_out, kb_out, kd_out, r_out, v_out, g_out, bon_out):
    c = pl.program_id(0)
    prev_ok, next_ok = _halo_valid(c, pl.num_programs(0), seg_chunk)
    z = zc_ref[...]
    tc = z.shape[0]
    zprev = jnp.concatenate([zp_ref[...] * prev_ok, z[:-1]], axis=0)
    znext = jnp.concatenate([z[1:], zn_ref[...] * next_ok], axis=0)
    zs = z + (zprev - z) * mua_ref[...] + (znext - z) * mub_ref[...]
    w = BRANCH_W
    r, k, v = zs[..., :w], zs[..., w:2 * w], zs[..., 2 * w:3 * w]
    wa = zs[..., 3 * w:3 * w + LANE]
    gl = zs[..., 3 * w + LANE:]
    lane = lax.broadcasted_iota(jnp.int32, wa.shape, 2)
    t = jnp.where(lane < RWKV_DECAY_LORA, jnp.tanh(wa), wa)
    bwd = _backward_rows(t.shape)
    lhs = _by_direction(t, bwd).reshape(tc * ROWS, 2 * LANE).astype(BF16)
    lo = _dot(lhs, wl_ref[...]).reshape(tc, ROWS, 2 * w)
    w_pre = w0_ref[...] + lo[..., :w]
    decay = jnp.exp(-math.exp(-0.5) * jax.nn.sigmoid(w_pre))
    asig = jax.nn.sigmoid(a0_ref[...] + lo[..., w:])
    blk = blk_ref[...]
    kkv = k * kk_ref[...]
    ssq = _head_sum((kkv * kkv).reshape(tc * ROWS, w), blk).reshape(tc, ROWS, w)
    kkn = kkv * lax.rsqrt(ssq + 1e-12)
    kd = k * (1.0 + (asig - 1.0) * ka_ref[...])
    g = _dot(jax.nn.sigmoid(gl).reshape(tc * ROWS, RWKV_G_LORA).astype(BF16), g2_ref[...]).reshape(tc, ROWS, w)
    rkd = _head_sum((r * kd * rk_ref[...]).reshape(tc * ROWS, w), blk).reshape(tc, ROWS, w)
    w_out[...] = decay
    kk_out[...] = kkn
    kb_out[...] = kkn * asig
    kd_out[...] = kd
    r_out[...] = r
    v_out[...] = v
    g_out[...] = g
    bon_out[...] = rkd * v


def _rwkv_prep(z_tm, p, seg, tc):
    s = z_tm.shape[0]
    w = BRANCH_W
    n_chunks = s // tc
    row_spec = lambda cols: pl.BlockSpec((ROWS, cols), lambda c: (0, 0))
    out_spec = pl.BlockSpec((tc, ROWS, w), lambda c: (c, 0, 0))
    return pl.pallas_call(
        functools.partial(_rwkv_prep_kernel, seg // tc),
        grid=(n_chunks,),
        in_specs=[pl.BlockSpec((tc, ROWS, RWKV_IN), lambda c: (c, 0, 0)),
                  pl.BlockSpec((1, ROWS, RWKV_IN), lambda c: (jnp.maximum(c * tc - 1, 0), 0, 0)),
                  pl.BlockSpec((1, ROWS, RWKV_IN), lambda c: (jnp.minimum((c + 1) * tc, s - 1), 0, 0)),
                  row_spec(RWKV_IN), row_spec(RWKV_IN), row_spec(w), row_spec(w),
                  pl.BlockSpec((1, w), lambda c: (0, 0)), pl.BlockSpec((1, w), lambda c: (0, 0)),
                  pl.BlockSpec((1, w), lambda c: (0, 0)),
                  pl.BlockSpec((2 * LANE, 2 * w), lambda c: (0, 0)),
                  pl.BlockSpec((RWKV_G_LORA, w), lambda c: (0, 0)),
                  pl.BlockSpec((w, w), lambda c: (0, 0))],
        out_specs=[out_spec] * 8,
        out_shape=[jax.ShapeDtypeStruct((s, ROWS, w), F32)] * 8,
        compiler_params=_cparams("parallel"),
        name="rwkv_prep",
    )(z_tm, z_tm, z_tm, p["mua"], p["mub"], p["w0"], p["a0"], p["kk"], p["ka"], p["rk"], p["wl"], p["g2"],
      p["blk"])


RWKV_ACC = 4
RWKV_VROWS = RWKV_HEAD // 2


def _rwkv_scan_kernel(w_ref, kk_ref, kb_ref, kd_ref, r_ref, v_ref, y_ref, p_ref):
    @pl.when(pl.program_id(0) == 0)
    def _():
        p_ref[...] = jnp.zeros_like(p_ref)

    def step(s, carry):
        v = v_ref[s]
        acc = [None] * RWKV_ACC
        for j in range(RWKV_HEAD):
            term = p_ref[j] * kk_ref[s, j:j + 1, :]
            acc[j % RWKV_ACC] = term if acc[j % RWKV_ACC] is None else acc[j % RWKV_ACC] + term
        sa = -((acc[0] + acc[1]) + (acc[2] + acc[3]))
        yacc = [None] * RWKV_ACC
        for j in range(RWKV_HEAD):
            pn = p_ref[j] * w_ref[s, j:j + 1, :] + sa * kb_ref[s, j:j + 1, :] + v * kd_ref[s, j:j + 1, :]
            p_ref[j] = pn
            term = pn * r_ref[s, j:j + 1, :]
            yacc[j % RWKV_ACC] = term if yacc[j % RWKV_ACC] is None else yacc[j % RWKV_ACC] + term
        y_ref[s] = (yacc[0] + yacc[1]) + (yacc[2] + yacc[3])
        return carry

    lax.fori_loop(0, v_ref.shape[0], step, 0)


def _rwkv_scan(ops_j, v_i, tc):
    s = v_i.shape[0]
    jspec = pl.BlockSpec((tc, RWKV_HEAD, LANE), lambda c: (c, 0, 0))
    ispec = pl.BlockSpec((tc, RWKV_VROWS, LANE), lambda c: (c, 0, 0))
    return pl.pallas_call(
        _rwkv_scan_kernel,
        grid=(s // tc,),
        in_specs=[jspec] * 5 + [ispec],
        out_specs=ispec,
        out_shape=jax.ShapeDtypeStruct((s, RWKV_VROWS, LANE), F32),
        scratch_shapes=[pltpu.VMEM((RWKV_HEAD, RWKV_VROWS, LANE), F32)],
        compiler_params=_cparams("arbitrary"),
        name="rwkv_scan",
    )(*ops_j, v_i)


def _to_scan_key_layout(x):
    s = x.shape[0]
    y = x.reshape(s, ROWS, RWKV_HEADS, RWKV_HEAD).transpose(0, 3, 1, 2).reshape(s, RWKV_HEAD, ROWS * RWKV_HEADS)
    return jnp.concatenate([y, y], axis=-1)


def _to_scan_value_layout(x):
    s = x.shape[0]
    y = x.reshape(s, ROWS, RWKV_HEADS, 2, RWKV_VROWS).transpose(0, 4, 3, 1, 2)
    return y.reshape(s, RWKV_VROWS, LANE)


def _from_scan_value_layout(y):
    s = y.shape[0]
    x = y.reshape(s, RWKV_VROWS, 2, ROWS, RWKV_HEADS).transpose(0, 3, 4, 2, 1)
    return x.reshape(s, ROWS, BRANCH_W)


S5_SLICES = BRANCH_W // LANE
S5_SLICE_W = S5_W // S5_SLICES
S5_COLS = 512


def _s5_kernel(u_ref, ar_ref, ai_ref, wdr_ref, wdi_ref, wc_ref, y_ref, dr_ref, di_ref, hr_ref, hi_ref):
    @pl.when(pl.program_id(0) == 0)
    def _():
        hr_ref[...] = jnp.zeros_like(hr_ref)
        hi_ref[...] = jnp.zeros_like(hi_ref)

    u = u_ref[...]
    tc = u.shape[0]
    m = tc * ROWS
    u2 = u.reshape(m, BRANCH_W)
    bwd = _backward_rows((m, LANE))
    for q in range(S5_SLICES):
        lhs = _by_direction(u2[:, q * LANE:(q + 1) * LANE], bwd).astype(BF16)
        dr_ref[:, q * S5_SLICE_W:(q + 1) * S5_SLICE_W] = _dot(lhs, wdr_ref[q])
        di_ref[:, q * S5_SLICE_W:(q + 1) * S5_SLICE_W] = _dot(lhs, wdi_ref[q])

    for cg in range(S5_W // S5_COLS):
        cols = slice(cg * S5_COLS, (cg + 1) * S5_COLS)
        ar = ar_ref[:, cols]
        ai = ai_ref[:, cols]

        def step(s, carry, cols=cols, ar=ar, ai=ai):
            hr, hi = carry
            rows = pl.ds(pl.multiple_of(s * ROWS, ROWS), ROWS)
            nr = ar * hr - ai * hi + dr_ref[rows, cols]
            ni = ar * hi + ai * hr + di_ref[rows, cols]
            dr_ref[rows, cols] = nr
            di_ref[rows, cols] = ni
            return nr, ni

        hr, hi = lax.fori_loop(0, tc, step, (hr_ref[:, cols], hi_ref[:, cols]))
        hr_ref[:, cols] = hr
        hi_ref[:, cols] = hi

    bwd_w = _backward_rows((m, S5_SLICE_W))
    for q in range(S5_SLICES):
        hre = dr_ref[:, q * S5_SLICE_W:(q + 1) * S5_SLICE_W]
        him = di_ref[:, q * S5_SLICE_W:(q + 1) * S5_SLICE_W]
        lhs = jnp.concatenate([_by_direction(hre, bwd_w), _by_direction(him, bwd_w)], axis=-1).astype(BF16)
        y_ref[:, :, q * LANE:(q + 1) * LANE] = _dot(lhs, wc_ref[q]).reshape(tc, ROWS, LANE)


def _s5_scan(u_tm, p, tc):
    s = u_tm.shape[0]
    m = tc * ROWS
    return pl.pallas_call(
        _s5_kernel,
        grid=(s // tc,),
        in_specs=[pl.BlockSpec((tc, ROWS, BRANCH_W), lambda c: (c, 0, 0)),
                  pl.BlockSpec((ROWS, S5_W), lambda c: (0, 0)),
                  pl.BlockSpec((ROWS, S5_W), lambda c: (0, 0)),
                  pl.BlockSpec((S5_SLICES, 2 * LANE, S5_SLICE_W), lambda c: (0, 0, 0)),
                  pl.BlockSpec((S5_SLICES, 2 * LANE, S5_SLICE_W), lambda c: (0, 0, 0)),
                  pl.BlockSpec((S5_SLICES, 4 * S5_SLICE_W, LANE), lambda c: (0, 0, 0))],
        out_specs=pl.BlockSpec((tc, ROWS, BRANCH_W), lambda c: (c, 0, 0)),
        out_shape=jax.ShapeDtypeStruct((s, ROWS, BRANCH_W), F32),
        scratch_shapes=[pltpu.VMEM((m, S5_W), F32), pltpu.VMEM((m, S5_W), F32),
                        pltpu.VMEM((ROWS, S5_W), F32), pltpu.VMEM((ROWS, S5_W), F32)],
        compiler_params=_cparams("arbitrary"),
        name="s5_scan",
    )(u_tm, p["ar"], p["ai"], p["wdr"], p["wdi"], p["wc"])


LRU_HALO = 2


def _lru_kernel(seg_chunk, xc_ref, xp_ref, xn_ref, cwf_ref, cwb_ref, cb_ref, wa_ref, wx_ref, ba_ref, bx_ref,
                sp_ref, h_out, a_ref, b_ref, h_ref):
    c = pl.program_id(0)

    @pl.when(c == 0)
    def _():
        h_ref[...] = jnp.zeros_like(h_ref)

    prev_ok, next_ok = _halo_valid(c, pl.num_programs(0), seg_chunk)
    x = xc_ref[...]
    tc = x.shape[0]
    ext = jnp.concatenate([xp_ref[...] * prev_ok, x, xn_ref[...] * next_ok], axis=0)
    xcv = cb_ref[...]
    for j in range(LRU_CONV):
        xcv = xcv + ext[j:j + tc] * cwf_ref[j] + ext[LRU_HALO * 2 - j:LRU_HALO * 2 - j + tc] * cwb_ref[j]
    m = tc * ROWS
    x2 = xcv.reshape(m, BRANCH_W)
    bwd = _backward_rows((m, LANE))
    ga, gx = [], []
    for q in range(BRANCH_W // LANE):
        lhs = _by_direction(x2[:, q * LANE:(q + 1) * LANE], bwd).astype(BF16)
        ga.append(_dot(lhs, wa_ref[q]))
        gx.append(_dot(lhs, wx_ref[q]))
    gr = jax.nn.sigmoid(jnp.concatenate(ga, axis=-1).reshape(tc, ROWS, BRANCH_W) + ba_ref[...])
    gi = jax.nn.sigmoid(jnp.concatenate(gx, axis=-1).reshape(tc, ROWS, BRANCH_W) + bx_ref[...])
    log_a = -LRU_C * gr * sp_ref[...]
    a_ref[...] = jnp.exp(log_a)
    b_ref[...] = jnp.sqrt(-jnp.tanh(log_a) * (jnp.exp(2.0 * log_a) + 1.0)) * gi * xcv

    def step(s, h):
        h = a_ref[s] * h + b_ref[s]
        h_out[s] = h
        return h

    h_ref[...] = lax.fori_loop(0, tc, step, h_ref[...])


def _lru_scan(x_tm, p, seg, tc):
    s = x_tm.shape[0]
    w = BRANCH_W
    hb = tc // LRU_HALO
    row_spec = pl.BlockSpec((ROWS, w), lambda c: (0, 0))
    return pl.pallas_call(
        functools.partial(_lru_kernel, seg // tc),
        grid=(s // tc,),
        in_specs=[pl.BlockSpec((tc, ROWS, w), lambda c: (c, 0, 0)),
                  pl.BlockSpec((LRU_HALO, ROWS, w), lambda c: (jnp.maximum(c * hb - 1, 0), 0, 0)),
                  pl.BlockSpec((LRU_HALO, ROWS, w), lambda c: (jnp.minimum((c + 1) * hb, s // LRU_HALO - 1), 0, 0)),
                  pl.BlockSpec((LRU_CONV, ROWS, w), lambda c: (0, 0, 0)),
                  pl.BlockSpec((LRU_CONV, ROWS, w), lambda c: (0, 0, 0)),
                  pl.BlockSpec((1, w), lambda c: (0, 0)),
                  pl.BlockSpec((w // LANE, 2 * LANE, LANE), lambda c: (0, 0, 0)),
                  pl.BlockSpec((w // LANE, 2 * LANE, LANE), lambda c: (0, 0, 0)),
                  row_spec, row_spec, row_spec],
        out_specs=pl.BlockSpec((tc, ROWS, w), lambda c: (c, 0, 0)),
        out_shape=jax.ShapeDtypeStruct((s, ROWS, w), F32),
        scratch_shapes=[pltpu.VMEM((tc, ROWS, w), F32), pltpu.VMEM((tc, ROWS, w), F32),
                        pltpu.VMEM((ROWS, w), F32)],
        compiler_params=_cparams("arbitrary"),
        name="lru_scan",
    )(x_tm, x_tm, x_tm, p["cwf"], p["cwb"], p["cb"], p["wa"], p["wx"], p["ba"], p["bx"], p["sp"])


def _branch_out_kernel(yf_ref, yb_ref, b0_ref, b1_ref, g_ref, lnw_ref, lnb_ref, blk_ref,
                       u_ref, sf_ref, sb_ref, d_ref, gw_ref, gb_ref,
                       hf_ref, hb_ref, gate_ref, ob_ref, os_ref, od_ref):
    blk = blk_ref[...]
    y = yf_ref[...] + yb_ref[...]
    mean = _head_sum(y, blk) * (1.0 / RWKV_HEAD)
    yc = y - mean
    var = _head_sum(yc * yc, blk) * (1.0 / RWKV_HEAD)
    yn = yc * lax.rsqrt(var + RWKV_GN_EPS) * lnw_ref[...] + lnb_ref[...]
    ob_ref[...] = ((yn + (b0_ref[...] + b1_ref[...])) * g_ref[...]).astype(BF16)

    t = _gelu(u_ref[...] * d_ref[...] + sf_ref[...] + sb_ref[...])
    os_ref[...] = (t * jax.nn.sigmoid(_dot(t.astype(BF16), gw_ref[...]) + gb_ref[...])).astype(BF16)

    od_ref[...] = ((hf_ref[...] + hb_ref[...]) * _gelu(gate_ref[...])).astype(BF16)


def _branch_out(rw, s5, lru, p, tm):
    n = rw[0].shape[0]
    w = BRANCH_W
    tok = pl.BlockSpec((tm, w), lambda i: (i, 0))
    vec = pl.BlockSpec((1, w), lambda i: (0, 0))
    mat = pl.BlockSpec((w, w), lambda i: (0, 0))
    return pl.pallas_call(
        _branch_out_kernel,
        grid=(n // tm,),
        in_specs=[tok] * 5 + [vec, vec, mat] + [tok] * 3 + [vec, mat, vec]
        + [tok, tok, pl.BlockSpec((tm, w), lambda i: (i, 1))],
        out_specs=[tok] * 3,
        out_shape=[jax.ShapeDtypeStruct((n, w), BF16)] * 3,
        compiler_params=_cparams("parallel"),
        name="branch_out",
    )(*rw, p["ln_w"], p["ln_b"], p["blk"], *s5, p["s5_d"], p["glu_w"], p["glu_b"], *lru)


def _to_time_major(z, batch, seq, ctx_len):
    c = z.shape[-1]
    lat = z[:batch * seq].reshape(batch, seq, c)
    ctx = z[batch * seq:].reshape(batch, ctx_len, c)
    fwd = jnp.concatenate([ctx, lat], axis=1)
    bwd = jnp.concatenate([ctx[:, ::-1], lat[:, ::-1]], axis=1)
    return jnp.concatenate([fwd, bwd], axis=0).transpose(1, 0, 2)


def _from_time_major(y, batch, seq, ctx_len):
    c = y.shape[-1]
    y = y.transpose(1, 0, 2)
    fwd, bwd = y[:batch], y[batch:]
    f_tok = jnp.concatenate([fwd[:, ctx_len:].reshape(batch * seq, c),
                             fwd[:, :ctx_len].reshape(batch * ctx_len, c)], axis=0)
    b_tok = jnp.concatenate([bwd[:, ctx_len:][:, ::-1].reshape(batch * seq, c),
                             bwd[:, :ctx_len][:, ::-1].reshape(batch * ctx_len, c)], axis=0)
    return f_tok, b_tok


def _rows_by_direction(p, batch):
    return jnp.repeat(p.astype(F32), batch, axis=0)


def _block_diag(blocks):
    n, a, b = blocks.shape
    eye = jnp.eye(n, dtype=blocks.dtype)
    return jnp.einsum('nab,nm->namb', blocks, eye).reshape(n * a, n * b)


def _rope_tables(seq, tm):
    rows = seq // GRID_W
    row = jnp.repeat(jnp.arange(rows, dtype=F32), GRID_W)
    col = jnp.tile(jnp.arange(GRID_W, dtype=F32), rows)
    n_pair = MLA_ROPE // 4
    inv = ROPE_BASE ** (-jnp.arange(n_pair, dtype=F32) / n_pair)
    ang = jnp.concatenate([row[:, None] * inv, col[:, None] * inv], -1)
    cos = jnp.concatenate([jnp.cos(ang), jnp.ones((tm, MLA_ROPE // 2), F32)], axis=0)
    sin = jnp.concatenate([jnp.sin(ang), jnp.zeros((tm, MLA_ROPE // 2), F32)], axis=0)
    n = cos.shape[0]
    z32 = jnp.zeros((n, MLA_ROPE // 2), F32)
    z64 = jnp.zeros((n, MLA_ROPE), F32)
    one = jnp.ones((n, MLA_NOPE), F32)
    z128 = jnp.zeros((n, MLA_NOPE), F32)
    qc = MLA_SCALE * jnp.concatenate([one, cos, cos, z64], axis=1)
    qs1 = MLA_SCALE * jnp.concatenate([z128, -sin, z32, z64], axis=1)
    qs2 = MLA_SCALE * jnp.concatenate([z128, z32, sin, z64], axis=1)
    kc = jnp.concatenate([cos, cos, z64], axis=1)
    ks1 = jnp.concatenate([-sin, z32, z64], axis=1)
    ks2 = jnp.concatenate([z32, sin, z64], axis=1)
    return qc, qs1, qs2, kc, ks1, ks2


def _s5_discretise(lam_re, lam_im, log_dt, b_re, b_im):
    lre = jnp.minimum(lam_re.astype(F32), -1e-4)
    lim = lam_im.astype(F32)
    dt = jnp.exp(log_dt.astype(F32))[..., None]
    mag = jnp.exp(lre * dt)
    ar, ai = mag * jnp.cos(lim * dt), mag * jnp.sin(lim * dt)
    den = lre * lre + lim * lim
    nr, ni = ar - 1.0, ai
    cr = (nr * lre + ni * lim) / den
    ci = (ni * lre - nr * lim) / den
    b_re, b_im = b_re.astype(F32), b_im.astype(F32)
    br = cr[..., None] * b_re - ci[..., None] * b_im
    bi = cr[..., None] * b_im + ci[..., None] * b_re
    return ar, ai, br, bi


def kernel(x, c, ctx, c_ctx, ada_w, ada_b, norm_w, ffn_w1, ffn_w3, ffn_w2, w_in, gate_b, mla_q_norm, mla_w_uq, mla_kv_norm, mla_w_ukv, rwkv_mu, rwkv_w0, rwkv_w2, rwkv_a0, rwkv_a2, rwkv_g2, rwkv_kk, rwkv_ka, rwkv_rk, rwkv_ln_w, rwkv_ln_b, s5_lam_re, s5_lam_im, s5_log_dt, s5_b_re, s5_b_im, s5_c_re, s5_c_im, s5_d, s5_glu_w, s5_glu_b, lru_conv_w, lru_conv_b, lru_wa, lru_ba, lru_wx, lru_bx, lru_lam, w_branch, w_out):
    batch, seq, d = x.shape
    ctx_len = ctx.shape[1]
    depth = ada_w.shape[0]
    assert 2 * batch == ROWS and d == D_MODEL
    n_lat, n_ctx = batch * seq, batch * ctx_len
    w = BRANCH_W

    tm_big = 512 if n_ctx % 512 == 0 else 256
    rows_big = _Rows(n_lat, n_ctx, seq, batch, tm_big)
    rows = _Rows(n_lat, n_ctx, seq, batch, 256)
    tc = min(64, ctx_len)

    w1 = ffn_w1.astype(BF16)
    w3 = ffn_w3.astype(BF16)
    w2 = ffn_w2.astype(BF16)
    perm = jnp.concatenate([jnp.arange(0, MLA_ROPE, 2), jnp.arange(1, MLA_ROPE, 2)])
    kr0 = MLA_Q_LORA + MLA_KV_LORA
    w_mla = jnp.concatenate([w_in[:, :, :kr0], w_in[:, :, kr0 + perm],
                             jnp.zeros((depth, d, MLA_IN_PAD - MLA_IN), F32)], axis=-1).astype(BF16)
    w_rwkv = w_in[:, :, O_RWKV:O_S5].astype(BF16)
    w_s5 = w_in[:, :, O_S5:O_LRU].astype(BF16)
    w_lru = w_in[:, :, O_LRU:MIX_IN].astype(BF16)
    w_gate = w_in[:, :, MIX_IN:].astype(BF16)
    w_br = w_branch.astype(BF16)
    w_o = w_out.astype(BF16)
    nw = norm_w.reshape(depth, 6, 1, d)
    gb = gate_b.reshape(depth, N_BRANCH, 1, d)

    hd = MLA_NOPE + MLA_ROPE
    qcols = jnp.concatenate([jnp.arange(MLA_NOPE), MLA_NOPE + perm])
    wq = mla_w_uq.reshape(depth, MLA_Q_LORA, MLA_HEADS, hd)[..., qcols]
    wq = jnp.concatenate([wq, jnp.zeros((depth, MLA_Q_LORA, MLA_HEADS, MLA_HEAD_PAD - hd), F32)], axis=-1)
    wq = wq.reshape(depth, MLA_Q_LORA, MLA_HEADS * MLA_HEAD_PAD).astype(BF16)
    wkv = mla_w_ukv.astype(BF16)
    qn = mla_q_norm.reshape(depth, 1, MLA_Q_LORA)
    kvn = mla_kv_norm.reshape(depth, 1, MLA_KV_LORA)
    tabs = _rope_tables(seq, rows.tm)

    blk = _block_diag(jnp.ones((RWKV_HEADS, RWKV_HEAD, RWKV_HEAD), F32))

    cc = jnp.concatenate([c, c_ctx[None], jnp.zeros((ROWS - batch - 1, d), F32)], axis=0)
    mods = _modulation(cc, ada_w, ada_b).reshape(depth, ROWS, N_MOD, 1, d)

    xs = jnp.concatenate([x.reshape(n_lat, d), ctx.reshape(n_ctx, d)], axis=0)

    for l in range(depth):
        xs = _ffn_half(xs, mods, nw, w1, w3, w2, rows_big, l, 0, 0, 0, 1)
        z_mla, z_rwkv, z_s5, z_lru = _in_projection(xs, mods, nw, [w_mla, w_rwkv, w_s5, w_lru], rows, l)

        q, k, v = _mla_project(z_mla, qn, kvn, wq, wkv, tabs, rows, l)
        ya = _mla_attention(q, k, v, batch, seq, ctx_len, 256)

        mu = rwkv_mu[l]
        wl = jnp.concatenate([
            jnp.concatenate([jnp.concatenate([rwkv_w2[l, dd], jnp.zeros((RWKV_DECAY_LORA, w), F32)], axis=1),
                             jnp.concatenate([jnp.zeros((RWKV_A_LORA, w), F32), rwkv_a2[l, dd]], axis=1)], axis=0)
            for dd in range(2)], axis=0).astype(BF16)
        rp = {"mua": _rows_by_direction(mu, batch),
              "mub": _rows_by_direction(mu[::-1], batch),
              "w0": _rows_by_direction(rwkv_w0[l], batch),
              "a0": _rows_by_direction(rwkv_a0[l], batch),
              "kk": rwkv_kk[l].reshape(1, w), "ka": rwkv_ka[l].reshape(1, w),
              "rk": rwkv_rk[l].reshape(1, w), "wl": wl, "g2": rwkv_g2[l].astype(BF16), "blk": blk}
        prep = _rwkv_prep(_to_time_major(z_rwkv, batch, seq, ctx_len), rp, ctx_len, tc)
        dec, kkn, kkb, kd, r, vv, g_tm, bon_tm = prep
        y_scan = _rwkv_scan([_to_scan_key_layout(t) for t in (dec, kkn, kkb, kd, r)],
                            _to_scan_value_layout(vv), tc)
        yf, yb = _from_time_major(_from_scan_value_layout(y_scan), batch, seq, ctx_len)
        bon0, bon1 = _from_time_major(bon_tm, batch, seq, ctx_len)
        g_tok, _ = _from_time_major(g_tm, batch, seq, ctx_len)

        ar, ai, br, bi = _s5_discretise(s5_lam_re[l], s5_lam_im[l], s5_log_dt[l], s5_b_re[l], s5_b_im[l])
        gps = S5_GROUPS // S5_SLICES

        def drive(bm):
            bd = jnp.stack([jnp.stack([_block_diag(bm[dd, q * gps:(q + 1) * gps].transpose(0, 2, 1))
                                       for q in range(S5_SLICES)]) for dd in range(2)])
            return jnp.concatenate([bd[0], bd[1]], axis=1).astype(BF16)

        def read(cm):
            return jnp.stack([jnp.stack([_block_diag(cm[dd, q * gps:(q + 1) * gps].astype(F32).transpose(0, 2, 1))
                                         for q in range(S5_SLICES)]) for dd in range(2)])

        cre, cim = read(s5_c_re[l]), read(s5_c_im[l])
        sp = {"ar": _rows_by_direction(ar.reshape(2, S5_W), batch),
              "ai": _rows_by_direction(ai.reshape(2, S5_W), batch),
              "wdr": drive(br), "wdi": drive(bi),
              "wc": jnp.concatenate([cre[0], cre[1], -cim[0], -cim[1]], axis=1).astype(BF16)}
        s_tm = _s5_scan(_to_time_major(z_s5, batch, seq, ctx_len), sp, tc)
        sf, sb = _from_time_major(s_tm, batch, seq, ctx_len)

        def lru_gate(wm):
            per = LANE // LRU_BLOCK
            bd = jnp.stack([jnp.stack([_block_diag(wm[dd, q * per:(q + 1) * per].astype(F32))
                                       for q in range(w // LANE)]) for dd in range(2)])
            return jnp.concatenate([bd[0], bd[1]], axis=1).astype(BF16)

        fwd_rows = (jnp.arange(ROWS) < batch).astype(F32)[None, :, None]
        cw = jnp.broadcast_to(lru_conv_w[l].astype(F32)[:, None, :], (LRU_CONV, ROWS, w))
        lam = lru_lam[l].astype(F32)
        lp = {"cwf": cw * fwd_rows, "cwb": cw * (1.0 - fwd_rows), "cb": lru_conv_b[l].reshape(1, w),
              "wa": lru_gate(lru_wa[l]), "wx": lru_gate(lru_wx[l]),
              "ba": _rows_by_direction(lru_ba[l], batch), "bx": _rows_by_direction(lru_bx[l], batch),
              "sp": _rows_by_direction(jax.nn.softplus(-lam), batch)}
        h_tm = _lru_scan(_to_time_major(z_lru[:, :w], batch, seq, ctx_len), lp, ctx_len, tc)
        hf, hb = _from_time_major(h_tm, batch, seq, ctx_len)

        bp = {"ln_w": rwkv_ln_w[l].reshape(1, w), "ln_b": rwkv_ln_b[l].reshape(1, w), "blk": blk,
              "s5_d": s5_d[l].reshape(1, w), "glu_w": s5_glu_w[l].astype(BF16), "glu_b": s5_glu_b[l].reshape(1, w)}
        yb_o, ys_o, yd_o = _branch_out((yf, yb, bon0, bon1, g_tok), (z_s5, sf, sb), (hf, hb, z_lru), bp, rows.tm)

        ys = jnp.stack([ya, yb_o, ys_o, yd_o], axis=0)
        m = _gated_sum(xs, mods, nw, ys, w_gate, gb, w_br, rows_big, l)
        xs = _out_projection(m, w_o, xs, mods, nw, rows_big, l)
        xs = _ffn_half(xs, mods, nw, w1, w3, w2, rows_big, l, 1, 2, 4, 5)

    return xs[:n_lat].reshape(batch, seq, d)
```

```python
import functools
import math

import jax
import jax.numpy as jnp
from jax import lax
from jax.experimental import pallas as pl
from jax.experimental.pallas import tpu as pltpu

F32 = jnp.float32
BF16 = jnp.bfloat16

D_MODEL = 2048
N_BRANCH = 4
BRANCH_W = 512
D_FF = 5632
FFN_RES = 0.5
N_MOD = 9
NORM_EPS = 1e-6
GRID_W = 64

MLA_HEADS = 4
MLA_NOPE = 128
MLA_ROPE = 64
MLA_V = 128
MLA_Q_LORA = 768
MLA_KV_LORA = 512
MLA_SCALE = (MLA_NOPE + MLA_ROPE) ** -0.5
ROPE_BASE = 10000.0
MLA_HEAD_PAD = 256
MLA_IN_PAD = 1408

RWKV_HEAD = 64
RWKV_HEADS = BRANCH_W // RWKV_HEAD
RWKV_DECAY_LORA = 64
RWKV_A_LORA = 64
RWKV_G_LORA = 128
RWKV_GN_EPS = 64e-5
RWKV_IN = 3 * BRANCH_W + RWKV_DECAY_LORA + RWKV_A_LORA + RWKV_G_LORA

S5_GROUP = 16
S5_GROUPS = BRANCH_W // S5_GROUP
S5_STATE = 64
S5_W = S5_GROUPS * S5_STATE

LRU_BLOCKS = 8
LRU_BLOCK = BRANCH_W // LRU_BLOCKS
LRU_CONV = 4
LRU_C = 8.0

MLA_IN = MLA_Q_LORA + MLA_KV_LORA + MLA_ROPE
O_RWKV = MLA_IN
O_S5 = O_RWKV + RWKV_IN
O_LRU = O_S5 + BRANCH_W
MIX_IN = O_LRU + 2 * BRANCH_W

ROWS = 8
HALF = ROWS // 2
LANE = 128
VMEM_LIMIT = 56 * 1024 * 1024


def _cparams(*sem):
    return pltpu.CompilerParams(dimension_semantics=sem, vmem_limit_bytes=VMEM_LIMIT)


def _rms(x, w):
    return x * lax.rsqrt(jnp.mean(x * x, axis=-1, keepdims=True) + NORM_EPS) * w


def _dot(a, b):
    return jnp.dot(a, b, preferred_element_type=F32)


def _dot_exact(a, b):
    return jnp.dot(a, b, preferred_element_type=F32, precision=lax.Precision.HIGHEST)


def _gelu(x):
    return 0.5 * x * (1.0 + jnp.tanh(math.sqrt(2.0 / math.pi) * (x + 0.044715 * (x * x * x))))


def _rows8(x):
    return x.reshape(x.shape[0] // ROWS, ROWS, x.shape[1])


def _mod_in(x, g, mod_ref):
    return (_rows8(_rms(x, g)) * (1.0 + mod_ref[1]) + mod_ref[0]).reshape(x.shape)


def _mod_gate(y, g, mod_ref):
    return (_rows8(_rms(y, g)) * mod_ref[2]).reshape(y.shape)


def _mod_kernel(cc_ref, w_ref, b_ref, o_ref):
    cc = cc_ref[...]
    s = (cc * jax.nn.sigmoid(cc)).astype(BF16)
    o_ref[...] = _dot(s, w_ref[...].astype(BF16)) + b_ref[...]


def _modulation(cc, ada_w, ada_b):
    L, D, N = ada_w.shape
    tn = 1024
    return pl.pallas_call(
        _mod_kernel,
        grid=(L, N // tn),
        in_specs=[pl.BlockSpec((ROWS, D), lambda l, j: (0, 0)),
                  pl.BlockSpec((None, D, tn), lambda l, j: (l, 0, j)),
                  pl.BlockSpec((None, 1, tn), lambda l, j: (l, 0, j))],
        out_specs=pl.BlockSpec((None, ROWS, tn), lambda l, j: (l, 0, j)),
        out_shape=jax.ShapeDtypeStruct((L, ROWS, N), F32),
        compiler_params=_cparams("parallel", "parallel"),
        name="modulation",
    )(cc, ada_w, ada_b.reshape(L, 1, N))


class _Rows:
    def __init__(self, n_lat, n_ctx, tm):
        assert n_lat % tm == 0 and n_ctx % tm == 0
        self.tm = tm
        self.n_tiles = (n_lat + n_ctx) // tm
        self.lat_tiles = n_lat // tm

    def segment(self, i):
        return (i >= self.lat_tiles).astype(jnp.int32)

    def pos_tile(self, i):
        return jnp.minimum(i, self.lat_tiles)


def _mod_spec(rows, l, group, grid_rank):
    if grid_rank == 2:
        return pl.BlockSpec((None, None, 3, ROWS, D_MODEL), lambda i, j: (l, rows.segment(i), group, 0, 0))
    return pl.BlockSpec((None, None, 3, ROWS, D_MODEL), lambda i: (l, rows.segment(i), group, 0, 0))


def _nw_spec(l, idx, grid_rank):
    if grid_rank == 2:
        return pl.BlockSpec((None, None, 1, D_MODEL), lambda i, j: (l, idx, 0, 0))
    return pl.BlockSpec((None, None, 1, D_MODEL), lambda i: (l, idx, 0, 0))


def _ffn_kernel(x_ref, mod_ref, gpre_ref, gpost_ref, w1_ref, w3_ref, w2_ref, o_ref, h_ref, acc_ref):
    j = pl.program_id(1)

    @pl.when(j == 0)
    def _():
        h_ref[...] = _mod_in(x_ref[...], gpre_ref[...], mod_ref).astype(BF16)
        acc_ref[...] = jnp.zeros_like(acc_ref)

    h = h_ref[...]
    a = _dot(h, w1_ref[...])
    b = _dot(h, w3_ref[...])
    u = (a * jax.nn.sigmoid(a) * b).astype(BF16)
    acc_ref[...] += _dot(u, w2_ref[...])

    @pl.when(j == pl.num_programs(1) - 1)
    def _():
        o_ref[...] = x_ref[...] + FFN_RES * _mod_gate(acc_ref[...], gpost_ref[...], mod_ref)


def _ffn_half(x, mods, nw, w1, w3, w2, rows, l, hf, group, i_pre, i_post):
    n, d = x.shape
    tm, tf = rows.tm, 512
    return pl.pallas_call(
        _ffn_kernel,
        grid=(rows.n_tiles, D_FF // tf),
        in_specs=[pl.BlockSpec((tm, d), lambda i, j: (i, 0)),
                  _mod_spec(rows, l, group, 2),
                  _nw_spec(l, i_pre, 2), _nw_spec(l, i_post, 2),
                  pl.BlockSpec((None, None, d, tf), lambda i, j: (l, hf, 0, j)),
                  pl.BlockSpec((None, None, d, tf), lambda i, j: (l, hf, 0, j)),
                  pl.BlockSpec((None, None, tf, d), lambda i, j: (l, hf, j, 0))],
        out_specs=pl.BlockSpec((tm, d), lambda i, j: (i, 0)),
        out_shape=jax.ShapeDtypeStruct((n, d), F32),
        scratch_shapes=[pltpu.VMEM((tm, d), BF16), pltpu.VMEM((tm, d), F32)],
        compiler_params=_cparams("parallel", "arbitrary"),
        name="ffn_half",
    )(x, mods, nw, nw, w1, w3, w2)


def _inproj_kernel(x_ref, mod_ref, g_ref, wa_ref, wb_ref, ws_ref, wd_ref, oa_ref, ob_ref, os_ref, od_ref):
    h = _mod_in(x_ref[...], g_ref[...], mod_ref).astype(BF16)
    oa_ref[...] = _dot(h, wa_ref[...])
    ob_ref[...] = _dot(h, wb_ref[...])
    os_ref[...] = _dot(h, ws_ref[...])
    od_ref[...] = _dot(h, wd_ref[...])


def _in_projection(x, mods, nw, ws, rows, l):
    n, d = x.shape
    tm = rows.tm
    widths = [w.shape[-1] for w in ws]
    return pl.pallas_call(
        _inproj_kernel,
        grid=(rows.n_tiles,),
        in_specs=[pl.BlockSpec((tm, d), lambda i: (i, 0)), _mod_spec(rows, l, 1, 1), _nw_spec(l, 2, 1)]
        + [pl.BlockSpec((None, d, wd), lambda i: (l, 0, 0)) for wd in widths],
        out_specs=[pl.BlockSpec((tm, wd), lambda i: (i, 0)) for wd in widths],
        out_shape=[jax.ShapeDtypeStruct((n, wd), F32) for wd in widths],
        compiler_params=_cparams("parallel"),
        name="in_projection",
    )(x, mods, nw, *ws)


def _gate_kernel(x_ref, mod_ref, g_ref, ya_ref, yb_ref, ys_ref, yd_ref, pt_ref, wg_ref, gb_ref, wb_ref,
                 o_ref, h_ref, acc_ref, y_ref):
    k = pl.program_id(1)
    tm = x_ref.shape[0]

    @pl.when(k == 0)
    def _():
        h_ref[...] = _mod_in(x_ref[...], g_ref[...], mod_ref).astype(BF16)
        acc_ref[...] = jnp.zeros_like(acc_ref)
        y_ref[...] = _dot(pt_ref[...], ya_ref[...].reshape(tm, BRANCH_W)).astype(BF16)

    for kk, ref in ((1, yb_ref), (2, ys_ref), (3, yd_ref)):
        @pl.when(k == kk)
        def _(ref=ref):
            y_ref[...] = ref[...]

    zg = _dot(h_ref[...], wg_ref[...]) + gb_ref[...]
    acc_ref[...] += jax.nn.sigmoid(zg) * _dot(y_ref[...], wb_ref[...])

    @pl.when(k == pl.num_programs(1) - 1)
    def _():
        o_ref[...] = acc_ref[...].astype(BF16)


def _gated_sum(x, mods, nw, ya, yb, ys, yd, perm_t, w_gate, gate_b, w_branch, rows, l):
    n, d = x.shape
    tm = rows.tm
    tok = pl.BlockSpec((tm, BRANCH_W), lambda i, k: (i, 0))
    return pl.pallas_call(
        _gate_kernel,
        grid=(rows.n_tiles, N_BRANCH),
        in_specs=[pl.BlockSpec((tm, d), lambda i, k: (i, 0)),
                  _mod_spec(rows, l, 1, 2), _nw_spec(l, 2, 2),
                  pl.BlockSpec((HALF, tm // HALF, BRANCH_W), lambda i, k: (0, i, 0)),
                  tok, tok, tok,
                  pl.BlockSpec((tm, tm), lambda i, k: (0, 0)),
                  pl.BlockSpec((None, d, d), lambda i, k: (l, 0, k)),
                  pl.BlockSpec((None, None, 1, d), lambda i, k: (l, k, 0, 0)),
                  pl.BlockSpec((None, None, BRANCH_W, d), lambda i, k: (l, k, 0, 0))],
        out_specs=pl.BlockSpec((tm, d), lambda i, k: (i, 0)),
        out_shape=jax.ShapeDtypeStruct((n, d), BF16),
        scratch_shapes=[pltpu.VMEM((tm, d), BF16), pltpu.VMEM((tm, d), F32), pltpu.VMEM((tm, BRANCH_W), BF16)],
        compiler_params=_cparams("parallel", "arbitrary"),
        name="gated_sum",
    )(x, mods, nw, ya, yb, ys, yd, perm_t, w_gate, gate_b, w_branch)


def _outproj_kernel(m_ref, w_ref, x_ref, mod_ref, g_ref, o_ref):
    m = _dot(m_ref[...], w_ref[...])
    o_ref[...] = x_ref[...] + _mod_gate(m, g_ref[...], mod_ref)


def _out_projection(m, w_out, x, mods, nw, rows, l):
    n, d = x.shape
    tm = rows.tm
    return pl.pallas_call(
        _outproj_kernel,
        grid=(rows.n_tiles,),
        in_specs=[pl.BlockSpec((tm, d), lambda i: (i, 0)),
                  pl.BlockSpec((None, d, d), lambda i: (l, 0, 0)),
                  pl.BlockSpec((tm, d), lambda i: (i, 0)),
                  _mod_spec(rows, l, 1, 1), _nw_spec(l, 3, 1)],
        out_specs=pl.BlockSpec((tm, d), lambda i: (i, 0)),
        out_shape=jax.ShapeDtypeStruct((n, d), F32),
        compiler_params=_cparams("parallel"),
        name="out_projection",
    )(m, w_out, x, mods, nw)


def _rope_lanes(x, c, s_lo, s_hi, width):
    return x * c + pltpu.roll(x, width - MLA_ROPE // 2, 1) * s_lo + pltpu.roll(x, MLA_ROPE // 2, 1) * s_hi


def _mla_proj_kernel(z_ref, qn_ref, kvn_ref, wq_ref, wkv_ref, perm_ref, qc_ref, qs1_ref, qs2_ref,
                     kc_ref, ks1_ref, ks2_ref, q_ref, k_ref, v_ref):
    z = z_ref[...]
    tm = z.shape[0]
    nt = tm // HALF
    pm = perm_ref[...]
    cq = _dot(pm, _rms(z[:, :MLA_Q_LORA], qn_ref[...]).astype(BF16)).astype(BF16)
    ckv = _dot(pm, _rms(z[:, MLA_Q_LORA:MLA_Q_LORA + MLA_KV_LORA], kvn_ref[...]).astype(BF16)).astype(BF16)
    kr = z[:, MLA_Q_LORA + MLA_KV_LORA:]
    krr = _rope_lanes(kr, kc_ref[...], ks1_ref[...], ks2_ref[...], LANE).astype(BF16)
    krr = _dot(pm, krr).astype(BF16)
    q = _dot(cq, wq_ref[...])
    kv = _dot(ckv, wkv_ref[...])
    tile4 = lambda t: jnp.concatenate([t] * HALF, axis=0)
    qc, qs1, qs2 = tile4(qc_ref[...]), tile4(qs1_ref[...]), tile4(qs2_ref[...])
    for h in range(MLA_HEADS):
        qh = _rope_lanes(q[:, h * MLA_HEAD_PAD:(h + 1) * MLA_HEAD_PAD], qc, qs1, qs2, MLA_HEAD_PAD).astype(BF16)
        kn = kv[:, h * 256:h * 256 + MLA_NOPE].astype(BF16)
        vh = kv[:, h * 256 + MLA_NOPE:(h + 1) * 256].astype(BF16)
        for b in range(HALF):
            rs = slice(b * nt, (b + 1) * nt)
            q_ref[b, :, h * MLA_HEAD_PAD:(h + 1) * MLA_HEAD_PAD] = qh[rs]
            k_ref[b, :, h * MLA_HEAD_PAD:h * MLA_HEAD_PAD + MLA_NOPE] = kn[rs]
            k_ref[b, :, h * MLA_HEAD_PAD + MLA_NOPE:(h + 1) * MLA_HEAD_PAD] = krr[rs]
            v_ref[b, :, h * MLA_V:(h + 1) * MLA_V] = vh[rs]


def _mla_project(z, qn, kvn, wq, wkv, perm, tabs, rows, l):
    n = z.shape[0]
    tm = rows.tm
    nt = tm // HALF
    hq = MLA_HEADS * MLA_HEAD_PAD
    tab_q = pl.BlockSpec((nt, MLA_HEAD_PAD), lambda i: (rows.pos_tile(i), 0))
    tab_k = pl.BlockSpec((tm, LANE), lambda i: (rows.pos_tile(i), 0))
    out = lambda wd: pl.BlockSpec((HALF, nt, wd), lambda i: (0, i, 0))
    return pl.pallas_call(
        _mla_proj_kernel,
        grid=(rows.n_tiles,),
        in_specs=[pl.BlockSpec((tm, MLA_IN_PAD), lambda i: (i, 0)),
                  pl.BlockSpec((None, 1, MLA_Q_LORA), lambda i: (l, 0, 0)),
                  pl.BlockSpec((None, 1, MLA_KV_LORA), lambda i: (l, 0, 0)),
                  pl.BlockSpec((None, MLA_Q_LORA, hq), lambda i: (l, 0, 0)),
                  pl.BlockSpec((None, MLA_KV_LORA, hq), lambda i: (l, 0, 0)),
                  pl.BlockSpec((tm, tm), lambda i: (0, 0)),
                  tab_q, tab_q, tab_q, tab_k, tab_k, tab_k],
        out_specs=[out(hq), out(hq), out(MLA_HEADS * MLA_V)],
        out_shape=[jax.ShapeDtypeStruct((HALF, n // HALF, hq), BF16),
                   jax.ShapeDtypeStruct((HALF, n // HALF, hq), BF16),
                   jax.ShapeDtypeStruct((HALF, n // HALF, MLA_HEADS * MLA_V), BF16)],
        compiler_params=_cparams("parallel"),
        name="mla_project",
    )(z, qn, kvn, wq, wkv, perm, *tabs)


def _scores(q, k):
    return lax.dot_general(q, k, (((1,), (1,)), ((), ())), preferred_element_type=F32)


def _attn_kernel(q_ref, k1_ref, v1_ref, k2_ref, v2_ref, o_ref):
    q = q_ref[...]
    s1 = _scores(q, k1_ref[...])
    s2 = _scores(q, k2_ref[...])
    m = jnp.maximum(jnp.max(s1, axis=-1, keepdims=True), jnp.max(s2, axis=-1, keepdims=True))
    p1 = jnp.exp(s1 - m)
    p2 = jnp.exp(s2 - m)
    den = jnp.sum(p1, axis=-1, keepdims=True) + jnp.sum(p2, axis=-1, keepdims=True)
    o = _dot(p1.astype(BF16), v1_ref[...]) + _dot(p2.astype(BF16), v2_ref[...])
    o_ref[...] = (o / den).astype(BF16)


def _attn_ctx_kernel(q_ref, k_ref, v_ref, prev_ref, o_ref):
    del prev_ref
    s = _scores(q_ref[...], k_ref[...])
    p = jnp.exp(s - jnp.max(s, axis=-1, keepdims=True))
    o = _dot(p.astype(BF16), v_ref[...])
    o_ref[...] = (o / jnp.sum(p, axis=-1, keepdims=True)).astype(BF16)


def _mla_attention(q, k, v, seq, ctx_len, tq):
    batch, s_all, _ = q.shape
    cb = seq // ctx_len
    qt = seq // tq
    lat = pl.pallas_call(
        _attn_kernel,
        grid=(batch, MLA_HEADS, qt),
        in_specs=[pl.BlockSpec((None, tq, MLA_HEAD_PAD), lambda b, h, i: (b, i, h)),
                  pl.BlockSpec((None, seq, MLA_HEAD_PAD), lambda b, h, i: (b, 0, h)),
                  pl.BlockSpec((None, seq, MLA_V), lambda b, h, i: (b, 0, h)),
                  pl.BlockSpec((None, ctx_len, MLA_HEAD_PAD), lambda b, h, i: (b, cb, h)),
                  pl.BlockSpec((None, ctx_len, MLA_V), lambda b, h, i: (b, cb, h))],
        out_specs=pl.BlockSpec((None, tq, MLA_V), lambda b, h, i: (b, i, h)),
        out_shape=jax.ShapeDtypeStruct((batch, s_all, MLA_HEADS * MLA_V), BF16),
        compiler_params=_cparams("parallel", "parallel", "parallel"),
        name="mla_attention",
    )(q, k, v, k, v)
    return pl.pallas_call(
        _attn_ctx_kernel,
        grid=(batch, MLA_HEADS),
        in_specs=[pl.BlockSpec((None, ctx_len, MLA_HEAD_PAD), lambda b, h: (b, cb, h)),
                  pl.BlockSpec((None, ctx_len, MLA_HEAD_PAD), lambda b, h: (b, cb, h)),
                  pl.BlockSpec((None, ctx_len, MLA_V), lambda b, h: (b, cb, h)),
                  pl.BlockSpec(memory_space=pl.ANY)],
        out_specs=pl.BlockSpec((None, ctx_len, MLA_V), lambda b, h: (b, cb, h)),
        out_shape=jax.ShapeDtypeStruct((batch, s_all, MLA_HEADS * MLA_V), BF16),
        input_output_aliases={3: 0},
        compiler_params=_cparams("parallel", "parallel"),
        name="mla_attention_ctx",
    )(q, k, v, lat)


class _Scan:
    def __init__(self, seq, ctx_len, tc):
        assert seq % tc == 0 and ctx_len % tc == 0 and tc % 2 == 0
        self.tc = tc
        self.rows = tc * HALF
        self.cs = ctx_len // tc
        self.lat = seq // tc
        self.n = self.cs + self.lat
        self.tiles = tc // 2
        self.n_tiles = self.n * self.tiles

    def fwd(self, g):
        return jnp.where(g < self.cs, self.lat + g, g - self.cs)

    def bwd(self, g):
        return jnp.where(g < self.cs, self.lat + self.cs - 1 - g, self.n - 1 - g)

    def chunk_specs(self, cols, col_block=0):
        return [pl.BlockSpec((self.rows, cols), lambda g: (self.fwd(g), col_block)),
                pl.BlockSpec((self.rows, cols), lambda g: (self.bwd(g), col_block))]

    def halo_specs(self, cols, col_block=0):
        t, last = self.tiles, self.n_tiles - 1
        return [pl.BlockSpec((ROWS, cols), lambda g: (jnp.maximum(self.fwd(g) * t - 1, 0), col_block)),
                pl.BlockSpec((ROWS, cols), lambda g: (jnp.minimum((self.fwd(g) + 1) * t, last), col_block)),
                pl.BlockSpec((ROWS, cols), lambda g: (jnp.minimum((self.bwd(g) + 1) * t, last), col_block)),
                pl.BlockSpec((ROWS, cols), lambda g: (jnp.maximum(self.bwd(g) * t - 1, 0), col_block))]


def _flip0(x):
    n = x.shape[0]
    if n == 1:
        return x
    return jnp.concatenate([x[n - 1 - k:n - k] for k in range(n)], axis=0)


def _low_rows(shape):
    return lax.broadcasted_iota(jnp.int32, shape, len(shape) - 2) % ROWS < HALF


def _merge_dirs(df, db):
    c = df.shape[-1]
    n2 = df.shape[0] // ROWS
    t = df.reshape(n2, ROWS, c)
    u = _flip0(db.reshape(n2, ROWS, c))
    lo = _low_rows(t.shape)
    even = jnp.where(lo, t, u)
    odd = pltpu.roll(jnp.where(lo, u, t), HALF, 1)
    return jnp.stack([even, odd], axis=1).reshape(2 * n2, ROWS, c)


def _split_dirs(y):
    n, _, c = y.shape
    y2 = y.reshape(n // 2, 2 * ROWS, c)
    a, b = y2[:, :ROWS], y2[:, ROWS:]
    rb = pltpu.roll(b, HALF, 1)
    lo = _low_rows(a.shape)
    df = jnp.where(lo, a, rb).reshape(n * HALF, c)
    db = _flip0(jnp.where(lo, rb, a)).reshape(n * HALF, c)
    return df, db


def _backward_rows(shape):
    return lax.broadcasted_iota(jnp.int32, shape, len(shape) - 2) % ROWS >= HALF


def _by_direction(x, bwd):
    zero = jnp.zeros_like(x)
    return jnp.concatenate([jnp.where(bwd, zero, x), jnp.where(bwd, x, zero)], axis=-1)


def _halo_valid(g, n_chunks, seg_chunk):
    prev_ok = jnp.logical_and(g != 0, g != seg_chunk)
    next_ok = jnp.logical_and(g != seg_chunk - 1, g != n_chunks - 1)
    return prev_ok.astype(F32), next_ok.astype(F32)


def _head_sum(x, blk):
    return _dot_exact(x, blk)


RWKV_ACC = 4
RWKV_VQ = 4
RWKV_VROWS = RWKV_HEAD // RWKV_VQ
RWKV_PAIRS = RWKV_HEADS // 2
RWKV_SEQ = RWKV_PAIRS * ROWS


def _to_scan_layout(x):
    tc = x.shape[0]
    a = jnp.concatenate([x[:, :, q * LANE:(q + 1) * LANE] for q in range(RWKV_PAIRS)], axis=1)
    a4 = jnp.concatenate([a] * RWKV_VQ, axis=1).reshape(tc * LANE, LANE)
    return a4.T


def _from_scan_layout(t):
    tc = t.shape[1] // LANE
    r = t.T.reshape(tc, RWKV_VQ, RWKV_SEQ, LANE)
    quarter = (lax.broadcasted_iota(jnp.int32, (tc, RWKV_SEQ, LANE), 2) // RWKV_VROWS) % RWKV_VQ
    ya = r[:, RWKV_VQ - 1]
    for iq in range(RWKV_VQ - 2, -1, -1):
        ya = jnp.where(quarter == iq, r[:, iq], ya)
    return jnp.concatenate([ya[:, q * ROWS:(q + 1) * ROWS] for q in range(RWKV_PAIRS)], axis=2)


def _rwkv_kernel(seg_chunk, zf_ref, zb_ref, zfp_ref, zfn_ref, zbn_ref, zbp_ref, mua_ref, mub_ref, w0_ref, a0_ref,
                 kk_ref, ka_ref, rk_ref, wl_ref, g2_ref, blk_ref,
                 yf_out, yb_out, bf_out, bb_out, g_out,
                 tw_ref, tkk_ref, tkb_ref, tkd_ref, tr_ref, tv_ref, ty_ref, p_ref):
    g = pl.program_id(0)

    @pl.when(g == 0)
    def _():
        p_ref[...] = jnp.zeros_like(p_ref)

    prev_ok, next_ok = _halo_valid(g, pl.num_programs(0), seg_chunk)
    z = _merge_dirs(zf_ref[...], zb_ref[...])
    tc = z.shape[0]
    zprev = jnp.concatenate([_merge_dirs(zfp_ref[...], zbn_ref[...])[1:2] * prev_ok, z[:-1]], axis=0)
    znext = jnp.concatenate([z[1:], _merge_dirs(zfn_ref[...], zbp_ref[...])[0:1] * next_ok], axis=0)
    zs = z + (zprev - z) * mua_ref[...] + (znext - z) * mub_ref[...]
    w = BRANCH_W
    r, k, v = zs[..., :w], zs[..., w:2 * w], zs[..., 2 * w:3 * w]
    wa = zs[..., 3 * w:3 * w + LANE]
    gl = zs[..., 3 * w + LANE:]
    lane = lax.broadcasted_iota(jnp.int32, wa.shape, 2)
    t = jnp.where(lane < RWKV_DECAY_LORA, jnp.tanh(wa), wa)
    bwd = _backward_rows(t.shape)
    lhs = _by_direction(t, bwd).reshape(tc * ROWS, 2 * LANE).astype(BF16)
    lo = _dot(lhs, wl_ref[...]).reshape(tc, ROWS, 2 * w)
    w_pre = w0_ref[...] + lo[..., :w]
    decay = jnp.exp(-math.exp(-0.5) * jax.nn.sigmoid(w_pre))
    asig = jax.nn.sigmoid(a0_ref[...] + lo[..., w:])
    blk = blk_ref[...]
    kkv = k * kk_ref[...]
    ssq = _head_sum((kkv * kkv).reshape(tc * ROWS, w), blk).reshape(tc, ROWS, w)
    kkn = kkv * lax.rsqrt(ssq + 1e-12)
    kd = k * (1.0 + (asig - 1.0) * ka_ref[...])
    gg = _dot(jax.nn.sigmoid(gl).reshape(tc * ROWS, RWKV_G_LORA).astype(BF16), g2_ref[...]).reshape(tc, ROWS, w)
    rkd = _head_sum((r * kd * rk_ref[...]).reshape(tc * ROWS, w), blk).reshape(tc, ROWS, w)
    g_out[...] = _split_dirs(gg)[0]
    bf_out[...], bb_out[...] = _split_dirs(rkd * v)

    tw_ref[...] = _to_scan_layout(decay)
    tkk_ref[...] = _to_scan_layout(kkn)
    tkb_ref[...] = _to_scan_layout(kkn * asig)
    tkd_ref[...] = _to_scan_layout(kd)
    tr_ref[...] = _to_scan_layout(r)
    tv_ref[...] = _to_scan_layout(v)

    def step(s, carry):
        lanes = pl.ds(pl.multiple_of(s * LANE, LANE), LANE)
        copy = lax.broadcasted_iota(jnp.int32, (RWKV_VROWS, LANE), 1) // RWKV_SEQ
        for par in range(2):
            base = par * RWKV_HEAD
            vt = tv_ref[base + (RWKV_VQ - 1) * RWKV_VROWS:base + RWKV_VQ * RWKV_VROWS, lanes]
            for iq in range(RWKV_VQ - 2, -1, -1):
                vt = jnp.where(copy == iq, tv_ref[base + iq * RWKV_VROWS:base + (iq + 1) * RWKV_VROWS, lanes], vt)
            acc = [None] * RWKV_ACC
            for j in range(RWKV_HEAD):
                term = p_ref[base + j] * tkk_ref[base + j:base + j + 1, lanes]
                acc[j % RWKV_ACC] = term if acc[j % RWKV_ACC] is None else acc[j % RWKV_ACC] + term
            sa = -((acc[0] + acc[1]) + (acc[2] + acc[3]))
            yacc = [None] * RWKV_ACC
            for j in range(RWKV_HEAD):
                row = slice(base + j, base + j + 1)
                pn = p_ref[base + j] * tw_ref[row, lanes] + sa * tkb_ref[row, lanes] + vt * tkd_ref[row, lanes]
                p_ref[base + j] = pn
                term = pn * tr_ref[row, lanes]
                yacc[j % RWKV_ACC] = term if yacc[j % RWKV_ACC] is None else yacc[j % RWKV_ACC] + term
            y = (yacc[0] + yacc[1]) + (yacc[2] + yacc[3])
            for iq in range(RWKV_VQ):
                ty_ref[base + iq * RWKV_VROWS:base + (iq + 1) * RWKV_VROWS, lanes] = y
        return carry

    lax.fori_loop(0, tc, step, 0)
    yf_out[...], yb_out[...] = _split_dirs(_from_scan_layout(ty_ref[...]))


def _rwkv_mixer(z, p, scan):
    n = z.shape[0]
    w = BRANCH_W
    row_spec = lambda cols: pl.BlockSpec((ROWS, cols), lambda g: (0, 0))
    vec = pl.BlockSpec((1, w), lambda g: (0, 0))
    fout = pl.BlockSpec((scan.rows, w), lambda g: (scan.fwd(g), 0))
    bout = pl.BlockSpec((scan.rows, w), lambda g: (scan.bwd(g), 0))
    tshape = pltpu.VMEM((LANE, scan.tc * LANE), F32)
    return pl.pallas_call(
        functools.partial(_rwkv_kernel, scan.cs),
        grid=(scan.n,),
        in_specs=scan.chunk_specs(RWKV_IN) + scan.halo_specs(RWKV_IN)
        + [row_spec(RWKV_IN), row_spec(RWKV_IN), row_spec(w), row_spec(w), vec, vec, vec,
           pl.BlockSpec((2 * LANE, 2 * w), lambda g: (0, 0)),
           pl.BlockSpec((RWKV_G_LORA, w), lambda g: (0, 0)),
           pl.BlockSpec((w, w), lambda g: (0, 0))],
        out_specs=[fout, bout, fout, bout, fout],
        out_shape=[jax.ShapeDtypeStruct((n, w), F32)] * 5,
        scratch_shapes=[tshape] * 7 + [pltpu.VMEM((2 * RWKV_HEAD, RWKV_VROWS, LANE), F32)],
        compiler_params=_cparams("arbitrary"),
        name="rwkv_mixer",
    )(z, z, z, z, z, z, p["mua"], p["mub"], p["w0"], p["a0"], p["kk"], p["ka"], p["rk"], p["wl"], p["g2"], p["blk"])


S5_SLICES = BRANCH_W // LANE
S5_SLICE_W = S5_W // S5_SLICES
S5_COLS = 512


def _s5_kernel(uf_ref, ub_ref, ar_ref, ai_ref, wdr_ref, wdi_ref, wc_ref, yf_out, yb_out,
               dr_ref, di_ref, hr_ref, hi_ref):
    @pl.when(pl.program_id(0) == 0)
    def _():
        hr_ref[...] = jnp.zeros_like(hr_ref)
        hi_ref[...] = jnp.zeros_like(hi_ref)

    u = _merge_dirs(uf_ref[...], ub_ref[...])
    tc = u.shape[0]
    m = tc * ROWS
    u2 = u.reshape(m, BRANCH_W)
    bwd = _backward_rows((m, LANE))
    for q in range(S5_SLICES):
        lhs = _by_direction(u2[:, q * LANE:(q + 1) * LANE], bwd).astype(BF16)
        dr_ref[:, q * S5_SLICE_W:(q + 1) * S5_SLICE_W] = _dot(lhs, wdr_ref[q])
        di_ref[:, q * S5_SLICE_W:(q + 1) * S5_SLICE_W] = _dot(lhs, wdi_ref[q])

    for cg in range(S5_W // S5_COLS):
        cols = slice(cg * S5_COLS, (cg + 1) * S5_COLS)
        ar = ar_ref[:, cols]
        ai = ai_ref[:, cols]

        def step(s, carry, cols=cols, ar=ar, ai=ai):
            hr, hi = carry
            rows = pl.ds(pl.multiple_of(s * ROWS, ROWS), ROWS)
            nr = ar * hr - ai * hi + dr_ref[rows, cols]
            ni = ar * hi + ai * hr + di_ref[rows, cols]
            dr_ref[rows, cols] = nr
            di_ref[rows, cols] = ni
            return nr, ni

        hr, hi = lax.fori_loop(0, tc, step, (hr_ref[:, cols], hi_ref[:, cols]))
        hr_ref[:, cols] = hr
        hi_ref[:, cols] = hi

    bwd_w = _backward_rows((m, S5_SLICE_W))
    ys = []
    for q in range(S5_SLICES):
        hre = dr_ref[:, q * S5_SLICE_W:(q + 1) * S5_SLICE_W]
        him = di_ref[:, q * S5_SLICE_W:(q + 1) * S5_SLICE_W]
        lhs = jnp.concatenate([_by_direction(hre, bwd_w), _by_direction(him, bwd_w)], axis=-1).astype(BF16)
        ys.append(_dot(lhs, wc_ref[q]))
    y = jnp.concatenate(ys, axis=-1).reshape(tc, ROWS, BRANCH_W)
    yf_out[...], yb_out[...] = _split_dirs(y)


def _s5_mixer(z, p, scan):
    n = z.shape[0]
    m = scan.tc * ROWS
    w = BRANCH_W
    return pl.pallas_call(
        _s5_kernel,
        grid=(scan.n,),
        in_specs=scan.chunk_specs(w)
        + [pl.BlockSpec((ROWS, S5_W), lambda g: (0, 0)),
           pl.BlockSpec((ROWS, S5_W), lambda g: (0, 0)),
           pl.BlockSpec((S5_SLICES, 2 * LANE, S5_SLICE_W), lambda g: (0, 0, 0)),
           pl.BlockSpec((S5_SLICES, 2 * LANE, S5_SLICE_W), lambda g: (0, 0, 0)),
           pl.BlockSpec((S5_SLICES, 4 * S5_SLICE_W, LANE), lambda g: (0, 0, 0))],
        out_specs=[pl.BlockSpec((scan.rows, w), lambda g: (scan.fwd(g), 0)),
                   pl.BlockSpec((scan.rows, w), lambda g: (scan.bwd(g), 0))],
        out_shape=[jax.ShapeDtypeStruct((n, w), F32)] * 2,
        scratch_shapes=[pltpu.VMEM((m, S5_W), F32), pltpu.VMEM((m, S5_W), F32),
                        pltpu.VMEM((ROWS, S5_W), F32), pltpu.VMEM((ROWS, S5_W), F32)],
        compiler_params=_cparams("arbitrary"),
        name="s5_mixer",
    )(z, z, p["ar"], p["ai"], p["wdr"], p["wdi"], p["wc"])


LRU_HALO = 2


def _lru_kernel(seg_chunk, xf_ref, xb_ref, xfp_ref, xfn_ref, xbn_ref, xbp_ref, cwf_ref, cwb_ref, cb_ref,
                wa_ref, wx_ref, ba_ref, bx_ref, sp_ref, hf_out, hb_out, a_ref, b_ref, hs_ref, h_ref):
    g = pl.program_id(0)

    @pl.when(g == 0)
    def _():
        h_ref[...] = jnp.zeros_like(h_ref)

    prev_ok, next_ok = _halo_valid(g, pl.num_programs(0), seg_chunk)
    x = _merge_dirs(xf_ref[...], xb_ref[...])
    tc = x.shape[0]
    ext = jnp.concatenate([_merge_dirs(xfp_ref[...], xbn_ref[...]) * prev_ok, x,
                           _merge_dirs(xfn_ref[...], xbp_ref[...]) * next_ok], axis=0)
    xcv = cb_ref[...]
    for j in range(LRU_CONV):
        xcv = xcv + ext[j:j + tc] * cwf_ref[j] + ext[LRU_HALO * 2 - j:LRU_HALO * 2 - j + tc] * cwb_ref[j]
    m = tc * ROWS
    x2 = xcv.reshape(m, BRANCH_W)
    bwd = _backward_rows((m, LANE))
    ga, gx = [], []
    for q in range(BRANCH_W // LANE):
        lhs = _by_direction(x2[:, q * LANE:(q + 1) * LANE], bwd).astype(BF16)
        ga.append(_dot(lhs, wa_ref[q]))
        gx.append(_dot(lhs, wx_ref[q]))
    gr = jax.nn.sigmoid(jnp.concatenate(ga, axis=-1).reshape(tc, ROWS, BRANCH_W) + ba_ref[...])
    gi = jax.nn.sigmoid(jnp.concatenate(gx, axis=-1).reshape(tc, ROWS, BRANCH_W) + bx_ref[...])
    log_a = -LRU_C * gr * sp_ref[...]
    a_ref[...] = jnp.exp(log_a)
    b_ref[...] = jnp.sqrt(-jnp.tanh(log_a) * (jnp.exp(2.0 * log_a) + 1.0)) * gi * xcv

    def step(s, h):
        h = a_ref[s] * h + b_ref[s]
        hs_ref[s] = h
        return h

    h_ref[...] = lax.fori_loop(0, tc, step, h_ref[...])
    hf_out[...], hb_out[...] = _split_dirs(hs_ref[...])


def _lru_mixer(z, p, scan):
    n = z.shape[0]
    w = BRANCH_W
    row_spec = pl.BlockSpec((ROWS, w), lambda g: (0, 0))
    chunk = pltpu.VMEM((scan.tc, ROWS, w), F32)
    return pl.pallas_call(
        functools.partial(_lru_kernel, scan.cs),
        grid=(scan.n,),
        in_specs=scan.chunk_specs(w) + scan.halo_specs(w)
        + [pl.BlockSpec((LRU_CONV, ROWS, w), lambda g: (0, 0, 0)),
           pl.BlockSpec((LRU_CONV, ROWS, w), lambda g: (0, 0, 0)),
           pl.BlockSpec((1, w), lambda g: (0, 0)),
           pl.BlockSpec((w // LANE, 2 * LANE, LANE), lambda g: (0, 0, 0)),
           pl.BlockSpec((w // LANE, 2 * LANE, LANE), lambda g: (0, 0, 0)),
           row_spec, row_spec, row_spec],
        out_specs=[pl.BlockSpec((scan.rows, w), lambda g: (scan.fwd(g), 0)),
                   pl.BlockSpec((scan.rows, w), lambda g: (scan.bwd(g), 0))],
        out_shape=[jax.ShapeDtypeStruct((n, w), F32)] * 2,
        scratch_shapes=[chunk, chunk, chunk, pltpu.VMEM((ROWS, w), F32)],
        compiler_params=_cparams("arbitrary"),
        name="lru_mixer",
    )(z, z, z, z, z, z, p["cwf"], p["cwb"], p["cb"], p["wa"], p["wx"], p["ba"], p["bx"], p["sp"])


def _branch_out_kernel(yf_ref, yb_ref, b0_ref, b1_ref, g_ref, lnw_ref, lnb_ref, blk_ref,
                       u_ref, sf_ref, sb_ref, d_ref, gw_ref, gb_ref,
                       hf_ref, hb_ref, gate_ref, ob_ref, os_ref, od_ref):
    blk = blk_ref[...]
    y = yf_ref[...] + yb_ref[...]
    mean = _head_sum(y, blk) * (1.0 / RWKV_HEAD)
    yc = y - mean
    var = _head_sum(yc * yc, blk) * (1.0 / RWKV_HEAD)
    yn = yc * lax.rsqrt(var + RWKV_GN_EPS) * lnw_ref[...] + lnb_ref[...]
    ob_ref[...] = ((yn + (b0_ref[...] + b1_ref[...])) * g_ref[...]).astype(BF16)

    t = _gelu(u_ref[...] * d_ref[...] + sf_ref[...] + sb_ref[...])
    os_ref[...] = (t * jax.nn.sigmoid(_dot(t.astype(BF16), gw_ref[...]) + gb_ref[...])).astype(BF16)

    od_ref[...] = ((hf_ref[...] + hb_ref[...]) * _gelu(gate_ref[...])).astype(BF16)


def _branch_out(rw, s5, lru, p, tm):
    n = rw[0].shape[0]
    w = BRANCH_W
    tok = pl.BlockSpec((tm, w), lambda i: (i, 0))
    vec = pl.BlockSpec((1, w), lambda i: (0, 0))
    mat = pl.BlockSpec((w, w), lambda i: (0, 0))
    return pl.pallas_call(
        _branch_out_kernel,
        grid=(n // tm,),
        in_specs=[tok] * 5 + [vec, vec, mat] + [tok] * 3 + [vec, mat, vec]
        + [tok, tok, pl.BlockSpec((tm, w), lambda i: (i, 1))],
        out_specs=[tok] * 3,
        out_shape=[jax.ShapeDtypeStruct((n, w), BF16)] * 3,
        compiler_params=_cparams("parallel"),
        name="branch_out",
    )(*rw, p["ln_w"], p["ln_b"], p["blk"], *s5, p["s5_d"], p["glu_w"], p["glu_b"], *lru)


def _rows_by_direction(p):
    return jnp.repeat(p.astype(F32), HALF, axis=0)


def _block_diag(blocks):
    n, a, b = blocks.shape
    eye = jnp.eye(n, dtype=blocks.dtype)
    return jnp.einsum('nab,nm->namb', blocks, eye).reshape(n * a, n * b)


def _by_slices(fn, per_dir, n_slices, per_slice):
    return jnp.stack([jnp.stack([fn(per_dir[dd, q * per_slice:(q + 1) * per_slice]) for q in range(n_slices)])
                      for dd in range(2)])


def _rope_tables(seq, nt):
    rows = seq // GRID_W
    row = jnp.repeat(jnp.arange(rows, dtype=F32), GRID_W)
    col = jnp.tile(jnp.arange(GRID_W, dtype=F32), rows)
    n_pair = MLA_ROPE // 4
    inv = ROPE_BASE ** (-jnp.arange(n_pair, dtype=F32) / n_pair)
    ang = jnp.concatenate([row[:, None] * inv, col[:, None] * inv], -1)
    cos = jnp.concatenate([jnp.cos(ang), jnp.ones((nt, MLA_ROPE // 2), F32)], axis=0)
    sin = jnp.concatenate([jnp.sin(ang), jnp.zeros((nt, MLA_ROPE // 2), F32)], axis=0)
    n = cos.shape[0]
    z32 = jnp.zeros((n, MLA_ROPE // 2), F32)
    z64 = jnp.zeros((n, MLA_ROPE), F32)
    one = jnp.ones((n, MLA_NOPE), F32)
    z128 = jnp.zeros((n, MLA_NOPE), F32)
    qc = MLA_SCALE * jnp.concatenate([one, cos, cos, z64], axis=1)
    qs1 = MLA_SCALE * jnp.concatenate([z128, -sin, z32, z64], axis=1)
    qs2 = MLA_SCALE * jnp.concatenate([z128, z32, sin, z64], axis=1)
    rep = lambda t: jnp.repeat(t, HALF, axis=0)
    kc = rep(jnp.concatenate([cos, cos, z64], axis=1))
    ks1 = rep(jnp.concatenate([-sin, z32, z64], axis=1))
    ks2 = rep(jnp.concatenate([z32, sin, z64], axis=1))
    return qc, qs1, qs2, kc, ks1, ks2


def _time_batch_to_batch_time(tm):
    nt = tm // HALF
    dst = jnp.arange(tm)
    src = (dst % nt) * HALF + dst // nt
    return (src[:, None] == jnp.arange(tm)[None, :]).astype(BF16)


def _s5_discretise(lam_re, lam_im, log_dt, b_re, b_im):
    lre = jnp.minimum(lam_re.astype(F32), -1e-4)
    lim = lam_im.astype(F32)
    dt = jnp.exp(log_dt.astype(F32))[..., None]
    mag = jnp.exp(lre * dt)
    ar, ai = mag * jnp.cos(lim * dt), mag * jnp.sin(lim * dt)
    den = lre * lre + lim * lim
    nr, ni = ar - 1.0, ai
    cr = (nr * lre + ni * lim) / den
    ci = (ni * lre - nr * lim) / den
    b_re, b_im = b_re.astype(F32), b_im.astype(F32)
    br = cr[..., None] * b_re - ci[..., None] * b_im
    bi = cr[..., None] * b_im + ci[..., None] * b_re
    return ar, ai, br, bi


def kernel(x, c, ctx, c_ctx, ada_w, ada_b, norm_w, ffn_w1, ffn_w3, ffn_w2, w_in, gate_b, mla_q_norm, mla_w_uq, mla_kv_norm, mla_w_ukv, rwkv_mu, rwkv_w0, rwkv_w2, rwkv_a0, rwkv_a2, rwkv_g2, rwkv_kk, rwkv_ka, rwkv_rk, rwkv_ln_w, rwkv_ln_b, s5_lam_re, s5_lam_im, s5_log_dt, s5_b_re, s5_b_im, s5_c_re, s5_c_im, s5_d, s5_glu_w, s5_glu_b, lru_conv_w, lru_conv_b, lru_wa, lru_ba, lru_wx, lru_bx, lru_lam, w_branch, w_out):
    batch, seq, d = x.shape
    ctx_len = ctx.shape[1]
    depth = ada_w.shape[0]
    assert batch == HALF and d == D_MODEL
    n_lat, n_ctx = batch * seq, batch * ctx_len
    w = BRANCH_W

    rows_big = _Rows(n_lat, n_ctx, 512)
    rows = _Rows(n_lat, n_ctx, 256)
    scan = _Scan(seq, ctx_len, 64)
    scan_rwkv = _Scan(seq, ctx_len, 32)

    w1 = ffn_w1.astype(BF16)
    w3 = ffn_w3.astype(BF16)
    w2 = ffn_w2.astype(BF16)
    perm = jnp.concatenate([jnp.arange(0, MLA_ROPE, 2), jnp.arange(1, MLA_ROPE, 2)])
    kr0 = MLA_Q_LORA + MLA_KV_LORA
    w_mla = jnp.concatenate([w_in[:, :, :kr0], w_in[:, :, kr0 + perm],
                             jnp.zeros((depth, d, MLA_IN_PAD - MLA_IN), F32)], axis=-1).astype(BF16)
    w_rwkv = w_in[:, :, O_RWKV:O_S5].astype(BF16)
    w_s5 = w_in[:, :, O_S5:O_LRU].astype(BF16)
    w_lru = w_in[:, :, O_LRU:MIX_IN].astype(BF16)
    w_gate = w_in[:, :, MIX_IN:].astype(BF16)
    w_br = w_branch.astype(BF16)
    w_o = w_out.astype(BF16)
    nw = norm_w.reshape(depth, 6, 1, d)
    gb = gate_b.reshape(depth, N_BRANCH, 1, d)

    hd = MLA_NOPE + MLA_ROPE
    qcols = jnp.concatenate([jnp.arange(MLA_NOPE), MLA_NOPE + perm])
    wq = mla_w_uq.reshape(depth, MLA_Q_LORA, MLA_HEADS, hd)[..., qcols]
    wq = jnp.concatenate([wq, jnp.zeros((depth, MLA_Q_LORA, MLA_HEADS, MLA_HEAD_PAD - hd), F32)], axis=-1)
    wq = wq.reshape(depth, MLA_Q_LORA, MLA_HEADS * MLA_HEAD_PAD).astype(BF16)
    wkv = mla_w_ukv.astype(BF16)
    qn = mla_q_norm.reshape(depth, 1, MLA_Q_LORA)
    kvn = mla_kv_norm.reshape(depth, 1, MLA_KV_LORA)
    tabs = _rope_tables(seq, rows.tm // HALF)
    perm_bt = _time_batch_to_batch_time(rows.tm)
    perm_tb = _time_batch_to_batch_time(rows_big.tm).T

    blk = _block_diag(jnp.ones((RWKV_HEADS, RWKV_HEAD, RWKV_HEAD), F32))

    cc = jnp.concatenate([c, c_ctx[None], jnp.zeros((ROWS - batch - 1, d), F32)], axis=0)
    mods = _modulation(cc, ada_w, ada_b).reshape(depth, ROWS, N_MOD, d)
    m_lat = jnp.concatenate([mods[:, :batch], mods[:, :batch]], axis=1).transpose(0, 2, 1, 3)
    m_ctx = jnp.broadcast_to(mods[:, batch][:, :, None, :], (depth, N_MOD, ROWS, d))
    mods = jnp.stack([m_lat, m_ctx], axis=1)

    xs = jnp.concatenate([x.transpose(1, 0, 2).reshape(n_lat, d), ctx.transpose(1, 0, 2).reshape(n_ctx, d)], axis=0)

    for l in range(depth):
        xs = _ffn_half(xs, mods, nw, w1, w3, w2, rows_big, l, 0, 0, 0, 1)
        z_mla, z_rwkv, z_s5, z_lru = _in_projection(xs, mods, nw, [w_mla, w_rwkv, w_s5, w_lru], rows, l)

        q, k, v = _mla_project(z_mla, qn, kvn, wq, wkv, perm_bt, tabs, rows, l)
        ya = _mla_attention(q, k, v, seq, ctx_len, 256)

        mu = rwkv_mu[l]
        wl = jnp.concatenate([
            jnp.concatenate([jnp.concatenate([rwkv_w2[l, dd], jnp.zeros((RWKV_DECAY_LORA, w), F32)], axis=1),
                             jnp.concatenate([jnp.zeros((RWKV_A_LORA, w), F32), rwkv_a2[l, dd]], axis=1)], axis=0)
            for dd in range(2)], axis=0).astype(BF16)
        rp = {"mua": _rows_by_direction(mu), "mub": _rows_by_direction(mu[::-1]),
              "w0": _rows_by_direction(rwkv_w0[l]), "a0": _rows_by_direction(rwkv_a0[l]),
              "kk": rwkv_kk[l].reshape(1, w), "ka": rwkv_ka[l].reshape(1, w),
              "rk": rwkv_rk[l].reshape(1, w), "wl": wl, "g2": rwkv_g2[l].astype(BF16), "blk": blk}
        rw = _rwkv_mixer(z_rwkv, rp, scan_rwkv)

        ar, ai, br, bi = _s5_discretise(s5_lam_re[l], s5_lam_im[l], s5_log_dt[l], s5_b_re[l], s5_b_im[l])
        gps = S5_GROUPS // S5_SLICES
        tdiag = lambda blocks: _block_diag(blocks.astype(F32).transpose(0, 2, 1))
        drive = lambda bm: (lambda bd: jnp.concatenate([bd[0], bd[1]], axis=1).astype(BF16))(
            _by_slices(tdiag, bm, S5_SLICES, gps))
        cre = _by_slices(tdiag, s5_c_re[l], S5_SLICES, gps)
        cim = _by_slices(tdiag, s5_c_im[l], S5_SLICES, gps)
        sp = {"ar": _rows_by_direction(ar.reshape(2, S5_W)), "ai": _rows_by_direction(ai.reshape(2, S5_W)),
              "wdr": drive(br), "wdi": drive(bi),
              "wc": jnp.concatenate([cre[0], cre[1], -cim[0], -cim[1]], axis=1).astype(BF16)}
        sf, sb = _s5_mixer(z_s5, sp, scan)

        diag = lambda blocks: _block_diag(blocks.astype(F32))
        lru_gate = lambda wm: (lambda bd: jnp.concatenate([bd[0], bd[1]], axis=1).astype(BF16))(
            _by_slices(diag, wm, w // LANE, LANE // LRU_BLOCK))
        fwd_rows = (jnp.arange(ROWS) < HALF).astype(F32)[None, :, None]
        cw = jnp.broadcast_to(lru_conv_w[l].astype(F32)[:, None, :], (LRU_CONV, ROWS, w))
        lp = {"cwf": cw * fwd_rows, "cwb": cw * (1.0 - fwd_rows), "cb": lru_conv_b[l].reshape(1, w),
              "wa": lru_gate(lru_wa[l]), "wx": lru_gate(lru_wx[l]),
              "ba": _rows_by_direction(lru_ba[l]), "bx": _rows_by_direction(lru_bx[l]),
              "sp": _rows_by_direction(jax.nn.softplus(-lru_lam[l].astype(F32)))}
        hf, hb = _lru_mixer(z_lru, lp, scan)

        bp = {"ln_w": rwkv_ln_w[l].reshape(1, w), "ln_b": rwkv_ln_b[l].reshape(1, w), "blk": blk,
              "s5_d": s5_d[l].reshape(1, w), "glu_w": s5_glu_w[l].astype(BF16), "glu_b": s5_glu_b[l].reshape(1, w)}
        yb_o, ys_o, yd_o = _branch_out(rw, (z_s5, sf, sb), (hf, hb, z_lru), bp, rows.tm)

        m = _gated_sum(xs, mods, nw, ya, yb_o, ys_o, yd_o, perm_tb, w_gate, gb, w_br, rows_big, l)
        xs = _out_projection(m, w_o, xs, mods, nw, rows_big, l)
        xs = _ffn_half(xs, mods, nw, w1, w3, w2, rows_big, l, 1, 2, 4, 5)

    return xs[:n_lat].reshape(seq, batch, d).transpose(1, 0, 2)
```

```python
import functools
import math

import jax
import jax.numpy as jnp
from jax import lax
from jax.experimental import pallas as pl
from jax.experimental.pallas import tpu as pltpu

F32 = jnp.float32
BF16 = jnp.bfloat16

D_MODEL = 2048
N_BRANCH = 4
BRANCH_W = 512
D_FF = 5632
FFN_RES = 0.5
N_MOD = 9
NORM_EPS = 1e-6
GRID_W = 64

MLA_HEADS = 4
MLA_NOPE = 128
MLA_ROPE = 64
MLA_V = 128
MLA_Q_LORA = 768
MLA_KV_LORA = 512
MLA_SCALE = (MLA_NOPE + MLA_ROPE) ** -0.5
ROPE_BASE = 10000.0
MLA_HEAD_PAD = 256
MLA_IN_PAD = 1408

RWKV_HEAD = 64
RWKV_HEADS = BRANCH_W // RWKV_HEAD
RWKV_DECAY_LORA = 64
RWKV_A_LORA = 64
RWKV_G_LORA = 128
RWKV_GN_EPS = 64e-5
RWKV_IN = 3 * BRANCH_W + RWKV_DECAY_LORA + RWKV_A_LORA + RWKV_G_LORA

S5_GROUP = 16
S5_GROUPS = BRANCH_W // S5_GROUP
S5_STATE = 64
S5_W = S5_GROUPS * S5_STATE

LRU_BLOCKS = 8
LRU_BLOCK = BRANCH_W // LRU_BLOCKS
LRU_CONV = 4
LRU_C = 8.0

MLA_IN = MLA_Q_LORA + MLA_KV_LORA + MLA_ROPE
O_RWKV = MLA_IN
O_S5 = O_RWKV + RWKV_IN
O_LRU = O_S5 + BRANCH_W
MIX_IN = O_LRU + 2 * BRANCH_W

ROWS = 8
HALF = ROWS // 2
LANE = 128
VMEM_LIMIT = 56 * 1024 * 1024


def _cparams(*sem):
    return pltpu.CompilerParams(dimension_semantics=sem, vmem_limit_bytes=VMEM_LIMIT)


def _rms(x, w):
    return x * lax.rsqrt(jnp.mean(x * x, axis=-1, keepdims=True) + NORM_EPS) * w


def _dot(a, b):
    return jnp.dot(a, b, preferred_element_type=F32)


def _gelu(x):
    return 0.5 * x * (1.0 + jnp.tanh(math.sqrt(2.0 / math.pi) * (x + 0.044715 * (x * x * x))))


def _rows8(x):
    return x.reshape(x.shape[0] // ROWS, ROWS, x.shape[1])


def _mod_in(x, g, mod_ref):
    return (_rows8(_rms(x, g)) * (1.0 + mod_ref[1]) + mod_ref[0]).reshape(x.shape)


def _mod_gate(y, g, mod_ref):
    return (_rows8(_rms(y, g)) * mod_ref[2]).reshape(y.shape)


def _mod_kernel(cc_ref, w_ref, b_ref, o_ref):
    cc = cc_ref[...]
    s = (cc * jax.nn.sigmoid(cc)).astype(BF16)
    o_ref[...] = _dot(s, w_ref[...].astype(BF16)) + b_ref[...]


def _modulation(cc, ada_w, ada_b):
    L, D, N = ada_w.shape
    tn = 1024
    return pl.pallas_call(
        _mod_kernel,
        grid=(L, N // tn),
        in_specs=[pl.BlockSpec((ROWS, D), lambda l, j: (0, 0)),
                  pl.BlockSpec((None, D, tn), lambda l, j: (l, 0, j)),
                  pl.BlockSpec((None, 1, tn), lambda l, j: (l, 0, j))],
        out_specs=pl.BlockSpec((None, ROWS, tn), lambda l, j: (l, 0, j)),
        out_shape=jax.ShapeDtypeStruct((L, ROWS, N), F32),
        compiler_params=_cparams("parallel", "parallel"),
        name="modulation",
    )(cc, ada_w, ada_b.reshape(L, 1, N))


class _Rows:
    def __init__(self, n_lat, n_ctx, tm):
        assert n_lat % tm == 0 and n_ctx % tm == 0
        self.tm = tm
        self.n_tiles = (n_lat + n_ctx) // tm
        self.lat_tiles = n_lat // tm

    def segment(self, i):
        return (i >= self.lat_tiles).astype(jnp.int32)

    def pos_tile(self, i):
        return jnp.minimum(i, self.lat_tiles)


def _mod_spec(rows, l, group, grid_rank):
    if grid_rank == 2:
        return pl.BlockSpec((None, None, 3, ROWS, D_MODEL), lambda i, j: (l, rows.segment(i), group, 0, 0))
    return pl.BlockSpec((None, None, 3, ROWS, D_MODEL), lambda i: (l, rows.segment(i), group, 0, 0))


def _nw_spec(l, idx, grid_rank):
    if grid_rank == 2:
        return pl.BlockSpec((None, None, 1, D_MODEL), lambda i, j: (l, idx, 0, 0))
    return pl.BlockSpec((None, None, 1, D_MODEL), lambda i: (l, idx, 0, 0))


def _ffn_kernel(x_ref, mod_ref, gpre_ref, gpost_ref, w1_ref, w3_ref, w2_ref, o_ref, h_ref, acc_ref):
    j = pl.program_id(1)

    @pl.when(j == 0)
    def _():
        h_ref[...] = _mod_in(x_ref[...], gpre_ref[...], mod_ref).astype(BF16)
        acc_ref[...] = jnp.zeros_like(acc_ref)

    h = h_ref[...]
    a = _dot(h, w1_ref[...])
    b = _dot(h, w3_ref[...])
    u = (a * jax.nn.sigmoid(a) * b).astype(BF16)
    acc_ref[...] += _dot(u, w2_ref[...])

    @pl.when(j == pl.num_programs(1) - 1)
    def _():
        o_ref[...] = x_ref[...] + FFN_RES * _mod_gate(acc_ref[...], gpost_ref[...], mod_ref)


def _ffn_half(x, mods, nw, w1, w3, w2, rows, l, hf, group, i_pre, i_post):
    n, d = x.shape
    tm, tf = rows.tm, 512
    return pl.pallas_call(
        _ffn_kernel,
        grid=(rows.n_tiles, D_FF // tf),
        in_specs=[pl.BlockSpec((tm, d), lambda i, j: (i, 0)),
                  _mod_spec(rows, l, group, 2),
                  _nw_spec(l, i_pre, 2), _nw_spec(l, i_post, 2),
                  pl.BlockSpec((None, None, d, tf), lambda i, j: (l, hf, 0, j)),
                  pl.BlockSpec((None, None, d, tf), lambda i, j: (l, hf, 0, j)),
                  pl.BlockSpec((None, None, tf, d), lambda i, j: (l, hf, j, 0))],
        out_specs=pl.BlockSpec((tm, d), lambda i, j: (i, 0)),
        out_shape=jax.ShapeDtypeStruct((n, d), F32),
        scratch_shapes=[pltpu.VMEM((tm, d), BF16), pltpu.VMEM((tm, d), F32)],
        compiler_params=_cparams("parallel", "arbitrary"),
        name="ffn_half",
    )(x, mods, nw, nw, w1, w3, w2)


def _inproj_kernel(x_ref, mod_ref, g_ref, wa_ref, wb_ref, ws_ref, wd_ref, oa_ref, ob_ref, os_ref, od_ref):
    h = _mod_in(x_ref[...], g_ref[...], mod_ref).astype(BF16)
    oa_ref[...] = _dot(h, wa_ref[...])
    ob_ref[...] = _dot(h, wb_ref[...])
    os_ref[...] = _dot(h, ws_ref[...])
    od_ref[...] = _dot(h, wd_ref[...])


def _in_projection(x, mods, nw, ws, rows, l):
    n, d = x.shape
    tm = rows.tm
    widths = [w.shape[-1] for w in ws]
    return pl.pallas_call(
        _inproj_kernel,
        grid=(rows.n_tiles,),
        in_specs=[pl.BlockSpec((tm, d), lambda i: (i, 0)), _mod_spec(rows, l, 1, 1), _nw_spec(l, 2, 1)]
        + [pl.BlockSpec((None, d, wd), lambda i: (l, 0, 0)) for wd in widths],
        out_specs=[pl.BlockSpec((tm, wd), lambda i: (i, 0)) for wd in widths],
        out_shape=[jax.ShapeDtypeStruct((n, wd), F32) for wd in widths],
        compiler_params=_cparams("parallel"),
        name="in_projection",
    )(x, mods, nw, *ws)


def _gate_kernel(x_ref, mod_ref, g_ref, ya_ref, yb_ref, ys_ref, yd_ref, pt_ref, wg_ref, gb_ref, wb_ref,
                 o_ref, h_ref, acc_ref, y_ref):
    k = pl.program_id(1)
    tm = x_ref.shape[0]

    @pl.when(k == 0)
    def _():
        h_ref[...] = _mod_in(x_ref[...], g_ref[...], mod_ref).astype(BF16)
        acc_ref[...] = jnp.zeros_like(acc_ref)
        y_ref[...] = _dot(pt_ref[...], ya_ref[...].reshape(tm, BRANCH_W)).astype(BF16)

    for kk, ref in ((1, yb_ref), (2, ys_ref), (3, yd_ref)):
        @pl.when(k == kk)
        def _(ref=ref):
            y_ref[...] = ref[...]

    zg = _dot(h_ref[...], wg_ref[...]) + gb_ref[...]
    acc_ref[...] += jax.nn.sigmoid(zg) * _dot(y_ref[...], wb_ref[...])

    @pl.when(k == pl.num_programs(1) - 1)
    def _():
        o_ref[...] = acc_ref[...].astype(BF16)


def _gated_sum(x, mods, nw, ya, yb, ys, yd, perm_t, w_gate, gate_b, w_branch, rows, l):
    n, d = x.shape
    tm = rows.tm
    tok = pl.BlockSpec((tm, BRANCH_W), lambda i, k: (i, 0))
    return pl.pallas_call(
        _gate_kernel,
        grid=(rows.n_tiles, N_BRANCH),
        in_specs=[pl.BlockSpec((tm, d), lambda i, k: (i, 0)),
                  _mod_spec(rows, l, 1, 2), _nw_spec(l, 2, 2),
                  pl.BlockSpec((HALF, tm // HALF, BRANCH_W), lambda i, k: (0, i, 0)),
                  tok, tok, tok,
                  pl.BlockSpec((tm, tm), lambda i, k: (0, 0)),
                  pl.BlockSpec((None, d, d), lambda i, k: (l, 0, k)),
                  pl.BlockSpec((None, None, 1, d), lambda i, k: (l, k, 0, 0)),
                  pl.BlockSpec((None, None, BRANCH_W, d), lambda i, k: (l, k, 0, 0))],
        out_specs=pl.BlockSpec((tm, d), lambda i, k: (i, 0)),
        out_shape=jax.ShapeDtypeStruct((n, d), BF16),
        scratch_shapes=[pltpu.VMEM((tm, d), BF16), pltpu.VMEM((tm, d), F32), pltpu.VMEM((tm, BRANCH_W), BF16)],
        compiler_params=_cparams("parallel", "arbitrary"),
        name="gated_sum",
    )(x, mods, nw, ya, yb, ys, yd, perm_t, w_gate, gate_b, w_branch)


def _outproj_kernel(m_ref, w_ref, x_ref, mod_ref, g_ref, o_ref):
    m = _dot(m_ref[...], w_ref[...])
    o_ref[...] = x_ref[...] + _mod_gate(m, g_ref[...], mod_ref)


def _out_projection(m, w_out, x, mods, nw, rows, l):
    n, d = x.shape
    tm = rows.tm
    return pl.pallas_call(
        _outproj_kernel,
        grid=(rows.n_tiles,),
        in_specs=[pl.BlockSpec((tm, d), lambda i: (i, 0)),
                  pl.BlockSpec((None, d, d), lambda i: (l, 0, 0)),
                  pl.BlockSpec((tm, d), lambda i: (i, 0)),
                  _mod_spec(rows, l, 1, 1), _nw_spec(l, 3, 1)],
        out_specs=pl.BlockSpec((tm, d), lambda i: (i, 0)),
        out_shape=jax.ShapeDtypeStruct((n, d), F32),
        compiler_params=_cparams("parallel"),
        name="out_projection",
    )(m, w_out, x, mods, nw)


def _rope_lanes(x, c, s_lo, s_hi, width):
    return x * c + pltpu.roll(x, width - MLA_ROPE // 2, 1) * s_lo + pltpu.roll(x, MLA_ROPE // 2, 1) * s_hi


def _mla_proj_kernel(z_ref, qn_ref, kvn_ref, wq_ref, wkv_ref, perm_ref, qc_ref, qs1_ref, qs2_ref,
                     kc_ref, ks1_ref, ks2_ref, q_ref, k_ref, v_ref):
    z = z_ref[...]
    tm = z.shape[0]
    nt = tm // HALF
    pm = perm_ref[...]
    cq = _dot(pm, _rms(z[:, :MLA_Q_LORA], qn_ref[...]).astype(BF16)).astype(BF16)
    ckv = _dot(pm, _rms(z[:, MLA_Q_LORA:MLA_Q_LORA + MLA_KV_LORA], kvn_ref[...]).astype(BF16)).astype(BF16)
    kr = z[:, MLA_Q_LORA + MLA_KV_LORA:]
    krr = _rope_lanes(kr, kc_ref[...], ks1_ref[...], ks2_ref[...], LANE).astype(BF16)
    krr = _dot(pm, krr).astype(BF16)
    q = _dot(cq, wq_ref[...])
    kv = _dot(ckv, wkv_ref[...])
    tile4 = lambda t: jnp.concatenate([t] * HALF, axis=0)
    qc, qs1, qs2 = tile4(qc_ref[...]), tile4(qs1_ref[...]), tile4(qs2_ref[...])
    for h in range(MLA_HEADS):
        qh = _rope_lanes(q[:, h * MLA_HEAD_PAD:(h + 1) * MLA_HEAD_PAD], qc, qs1, qs2, MLA_HEAD_PAD).astype(BF16)
        kn = kv[:, h * 256:h * 256 + MLA_NOPE].astype(BF16)
        vh = kv[:, h * 256 + MLA_NOPE:(h + 1) * 256].astype(BF16)
        for b in range(HALF):
            rs = slice(b * nt, (b + 1) * nt)
            q_ref[b, :, h * MLA_HEAD_PAD:(h + 1) * MLA_HEAD_PAD] = qh[rs]
            k_ref[b, :, h * MLA_HEAD_PAD:h * MLA_HEAD_PAD + MLA_NOPE] = kn[rs]
            k_ref[b, :, h * MLA_HEAD_PAD + MLA_NOPE:(h + 1) * MLA_HEAD_PAD] = krr[rs]
            v_ref[b, :, h * MLA_V:(h + 1) * MLA_V] = vh[rs]


def _mla_project(z, qn, kvn, wq, wkv, perm, tabs, rows, l):
    n = z.shape[0]
    tm = rows.tm
    nt = tm // HALF
    hq = MLA_HEADS * MLA_HEAD_PAD
    tab_q = pl.BlockSpec((nt, MLA_HEAD_PAD), lambda i: (rows.pos_tile(i), 0))
    tab_k = pl.BlockSpec((tm, LANE), lambda i: (rows.pos_tile(i), 0))
    out = lambda wd: pl.BlockSpec((HALF, nt, wd), lambda i: (0, i, 0))
    return pl.pallas_call(
        _mla_proj_kernel,
        grid=(rows.n_tiles,),
        in_specs=[pl.BlockSpec((tm, MLA_IN_PAD), lambda i: (i, 0)),
                  pl.BlockSpec((None, 1, MLA_Q_LORA), lambda i: (l, 0, 0)),
                  pl.BlockSpec((None, 1, MLA_KV_LORA), lambda i: (l, 0, 0)),
                  pl.BlockSpec((None, MLA_Q_LORA, hq), lambda i: (l, 0, 0)),
                  pl.BlockSpec((None, MLA_KV_LORA, hq), lambda i: (l, 0, 0)),
                  pl.BlockSpec((tm, tm), lambda i: (0, 0)),
                  tab_q, tab_q, tab_q, tab_k, tab_k, tab_k],
        out_specs=[out(hq), out(hq), out(MLA_HEADS * MLA_V)],
        out_shape=[jax.ShapeDtypeStruct((HALF, n // HALF, hq), BF16),
                   jax.ShapeDtypeStruct((HALF, n // HALF, hq), BF16),
                   jax.ShapeDtypeStruct((HALF, n // HALF, MLA_HEADS * MLA_V), BF16)],
        compiler_params=_cparams("parallel"),
        name="mla_project",
    )(z, qn, kvn, wq, wkv, perm, *tabs)


def _scores(q, k):
    return lax.dot_general(q, k, (((1,), (1,)), ((), ())), preferred_element_type=F32)


def _attn_kernel(q_ref, k1_ref, v1_ref, k2_ref, v2_ref, o_ref):
    q = q_ref[...]
    s1 = _scores(q, k1_ref[...])
    s2 = _scores(q, k2_ref[...])
    m = jnp.maximum(jnp.max(s1, axis=-1, keepdims=True), jnp.max(s2, axis=-1, keepdims=True))
    p1 = jnp.exp(s1 - m)
    p2 = jnp.exp(s2 - m)
    den = jnp.sum(p1, axis=-1, keepdims=True) + jnp.sum(p2, axis=-1, keepdims=True)
    o = _dot(p1.astype(BF16), v1_ref[...]) + _dot(p2.astype(BF16), v2_ref[...])
    o_ref[...] = (o / den).astype(BF16)


def _attn_ctx_kernel(q_ref, k_ref, v_ref, prev_ref, o_ref):
    del prev_ref
    s = _scores(q_ref[...], k_ref[...])
    p = jnp.exp(s - jnp.max(s, axis=-1, keepdims=True))
    o = _dot(p.astype(BF16), v_ref[...])
    o_ref[...] = (o / jnp.sum(p, axis=-1, keepdims=True)).astype(BF16)


def _mla_attention(q, k, v, seq, ctx_len, tq):
    batch, s_all, _ = q.shape
    cb = seq // ctx_len
    qt = seq // tq
    lat = pl.pallas_call(
        _attn_kernel,
        grid=(batch, MLA_HEADS, qt),
        in_specs=[pl.BlockSpec((None, tq, MLA_HEAD_PAD), lambda b, h, i: (b, i, h)),
                  pl.BlockSpec((None, seq, MLA_HEAD_PAD), lambda b, h, i: (b, 0, h)),
                  pl.BlockSpec((None, seq, MLA_V), lambda b, h, i: (b, 0, h)),
                  pl.BlockSpec((None, ctx_len, MLA_HEAD_PAD), lambda b, h, i: (b, cb, h)),
                  pl.BlockSpec((None, ctx_len, MLA_V), lambda b, h, i: (b, cb, h))],
        out_specs=pl.BlockSpec((None, tq, MLA_V), lambda b, h, i: (b, i, h)),
        out_shape=jax.ShapeDtypeStruct((batch, s_all, MLA_HEADS * MLA_V), BF16),
        compiler_params=_cparams("parallel", "parallel", "parallel"),
        name="mla_attention",
    )(q, k, v, k, v)
    return pl.pallas_call(
        _attn_ctx_kernel,
        grid=(batch, MLA_HEADS),
        in_specs=[pl.BlockSpec((None, ctx_len, MLA_HEAD_PAD), lambda b, h: (b, cb, h)),
                  pl.BlockSpec((None, ctx_len, MLA_HEAD_PAD), lambda b, h: (b, cb, h)),
                  pl.BlockSpec((None, ctx_len, MLA_V), lambda b, h: (b, cb, h)),
                  pl.BlockSpec(memory_space=pl.ANY)],
        out_specs=pl.BlockSpec((None, ctx_len, MLA_V), lambda b, h: (b, cb, h)),
        out_shape=jax.ShapeDtypeStruct((batch, s_all, MLA_HEADS * MLA_V), BF16),
        input_output_aliases={3: 0},
        compiler_params=_cparams("parallel", "parallel"),
        name="mla_attention_ctx",
    )(q, k, v, lat)


class _Scan:
    def __init__(self, seq, ctx_len, tc):
        assert seq % tc == 0 and ctx_len % tc == 0 and tc % 2 == 0
        self.tc = tc
        self.rows = tc * HALF
        self.cs = ctx_len // tc
        self.lat = seq // tc
        self.n = self.cs + self.lat
        self.tiles = tc // 2
        self.n_tiles = self.n * self.tiles

    def fwd(self, g):
        return jnp.where(g < self.cs, self.lat + g, g - self.cs)

    def bwd(self, g):
        return jnp.where(g < self.cs, self.lat + self.cs - 1 - g, self.n - 1 - g)

    def chunk_specs(self, cols, col_block=0):
        return [pl.BlockSpec((self.rows, cols), lambda g: (self.fwd(g), col_block)),
                pl.BlockSpec((self.rows, cols), lambda g: (self.bwd(g), col_block))]

    def halo_specs(self, cols, col_block=0):
        t, last = self.tiles, self.n_tiles - 1
        return [pl.BlockSpec((ROWS, cols), lambda g: (jnp.maximum(self.fwd(g) * t - 1, 0), col_block)),
                pl.BlockSpec((ROWS, cols), lambda g: (jnp.minimum((self.fwd(g) + 1) * t, last), col_block)),
                pl.BlockSpec((ROWS, cols), lambda g: (jnp.minimum((self.bwd(g) + 1) * t, last), col_block)),
                pl.BlockSpec((ROWS, cols), lambda g: (jnp.maximum(self.bwd(g) * t - 1, 0), col_block))]


def _flip0(x):
    n = x.shape[0]
    if n == 1:
        return x
    return jnp.concatenate([x[n - 1 - k:n - k] for k in range(n)], axis=0)


def _low_rows(shape):
    return lax.broadcasted_iota(jnp.int32, shape, len(shape) - 2) % ROWS < HALF


def _merge_dirs(df, db):
    c = df.shape[-1]
    n2 = df.shape[0] // ROWS
    t = df.reshape(n2, ROWS, c)
    u = _flip0(db.reshape(n2, ROWS, c))
    lo = _low_rows(t.shape)
    even = jnp.where(lo, t, u)
    odd = pltpu.roll(jnp.where(lo, u, t), HALF, 1)
    return jnp.stack([even, odd], axis=1).reshape(2 * n2, ROWS, c)


def _split_dirs(y):
    n, _, c = y.shape
    y2 = y.reshape(n // 2, 2 * ROWS, c)
    a, b = y2[:, :ROWS], y2[:, ROWS:]
    rb = pltpu.roll(b, HALF, 1)
    lo = _low_rows(a.shape)
    df = jnp.where(lo, a, rb).reshape(n * HALF, c)
    db = _flip0(jnp.where(lo, rb, a)).reshape(n * HALF, c)
    return df, db


def _backward_rows(shape):
    return lax.broadcasted_iota(jnp.int32, shape, len(shape) - 2) % ROWS >= HALF


def _by_direction(x, bwd):
    zero = jnp.zeros_like(x)
    return jnp.concatenate([jnp.where(bwd, zero, x), jnp.where(bwd, x, zero)], axis=-1)


def _halo_valid(g, n_chunks, seg_chunk):
    prev_ok = jnp.logical_and(g != 0, g != seg_chunk)
    next_ok = jnp.logical_and(g != seg_chunk - 1, g != n_chunks - 1)
    return prev_ok.astype(F32), next_ok.astype(F32)


def _head_sum(x, blk3):
    hi = x.astype(BF16)
    r1 = x - hi.astype(F32)
    mid = r1.astype(BF16)
    lo = (r1 - mid.astype(F32)).astype(BF16)
    return _dot(jnp.concatenate([hi, mid, lo], axis=-1), blk3)


RWKV_ACC = 4
RWKV_VQ = 4
RWKV_VROWS = RWKV_HEAD // RWKV_VQ
RWKV_PAIRS = RWKV_HEADS // 2
RWKV_SEQ = RWKV_PAIRS * ROWS


def _to_scan_layout(x):
    a = jnp.concatenate([x[:, q * LANE:(q + 1) * LANE] for q in range(RWKV_PAIRS)], axis=0)
    return jnp.concatenate([a] * RWKV_VQ, axis=0).T


def _from_scan_layout(ys):
    t = jnp.concatenate([y for y in ys for _ in range(RWKV_VQ)], axis=0).T.reshape(RWKV_VQ, RWKV_SEQ, LANE)
    quarter = (lax.broadcasted_iota(jnp.int32, (RWKV_SEQ, LANE), 1) // RWKV_VROWS) % RWKV_VQ
    ya = t[RWKV_VQ - 1]
    for iq in range(RWKV_VQ - 2, -1, -1):
        ya = jnp.where(quarter == iq, t[iq], ya)
    return jnp.concatenate([ya[q * ROWS:(q + 1) * ROWS] for q in range(RWKV_PAIRS)], axis=1)


def _rwkv_kernel(seg_chunk, zf_ref, zb_ref, zfp_ref, zfn_ref, zbn_ref, zbp_ref, mua_ref, mub_ref, w0_ref, a0_ref,
                 kk_ref, ka_ref, rk_ref, wl_ref, g2_ref, blk_ref,
                 yf_out, yb_out, bf_out, bb_out, g_out, *scratch):
    nat_refs, bufs_a, bufs_b = scratch[0:6], scratch[6:12], scratch[12:18]
    y_ref, p_ref = scratch[18], scratch[19]
    g = pl.program_id(0)

    @pl.when(g == 0)
    def _():
        p_ref[...] = jnp.zeros_like(p_ref)

    prev_ok, next_ok = _halo_valid(g, pl.num_programs(0), seg_chunk)
    z = _merge_dirs(zf_ref[...], zb_ref[...])
    tc = z.shape[0]
    zprev = jnp.concatenate([_merge_dirs(zfp_ref[...], zbn_ref[...])[1:2] * prev_ok, z[:-1]], axis=0)
    znext = jnp.concatenate([z[1:], _merge_dirs(zfn_ref[...], zbp_ref[...])[0:1] * next_ok], axis=0)
    zs = z + (zprev - z) * mua_ref[...] + (znext - z) * mub_ref[...]
    w = BRANCH_W
    r, k, v = zs[..., :w], zs[..., w:2 * w], zs[..., 2 * w:3 * w]
    wa = zs[..., 3 * w:3 * w + LANE]
    gl = zs[..., 3 * w + LANE:]
    lane = lax.broadcasted_iota(jnp.int32, wa.shape, 2)
    t = jnp.where(lane < RWKV_DECAY_LORA, jnp.tanh(wa), wa)
    bwd = _backward_rows(t.shape)
    lhs = _by_direction(t, bwd).reshape(tc * ROWS, 2 * LANE).astype(BF16)
    lo = _dot(lhs, wl_ref[...]).reshape(tc, ROWS, 2 * w)
    w_pre = w0_ref[...] + lo[..., :w]
    decay = jnp.exp(-math.exp(-0.5) * jax.nn.sigmoid(w_pre))
    asig = jax.nn.sigmoid(a0_ref[...] + lo[..., w:])
    blk = blk_ref[...]
    kkv = k * kk_ref[...]
    ssq = _head_sum((kkv * kkv).reshape(tc * ROWS, w), blk).reshape(tc, ROWS, w)
    kkn = kkv * lax.rsqrt(ssq + 1e-12)
    kd = k * (1.0 + (asig - 1.0) * ka_ref[...])
    gg = _dot(jax.nn.sigmoid(gl).reshape(tc * ROWS, RWKV_G_LORA).astype(BF16), g2_ref[...]).reshape(tc, ROWS, w)
    rkd = _head_sum((r * kd * rk_ref[...]).reshape(tc * ROWS, w), blk).reshape(tc, ROWS, w)
    g_out[...] = _split_dirs(gg)[0]
    bf_out[...], bb_out[...] = _split_dirs(rkd * v)

    for ref, val in zip(nat_refs, (decay, kkn, kkn * asig, kd, r, v)):
        ref[...] = val

    def stage(s, bufs):
        for src, dst in zip(nat_refs, bufs):
            dst[...] = _to_scan_layout(src[s])

    def advance(s, bufs):
        tw_ref, tkk_ref, tkb_ref, tkd_ref, tr_ref, tv_ref = bufs
        copy = lax.broadcasted_iota(jnp.int32, (RWKV_VROWS, LANE), 1) // RWKV_SEQ
        ys = []
        for par in range(2):
            base = par * RWKV_HEAD
            vt = tv_ref[base + (RWKV_VQ - 1) * RWKV_VROWS:base + RWKV_VQ * RWKV_VROWS, :]
            for iq in range(RWKV_VQ - 2, -1, -1):
                vt = jnp.where(copy == iq, tv_ref[base + iq * RWKV_VROWS:base + (iq + 1) * RWKV_VROWS, :], vt)
            acc = [None] * RWKV_ACC
            for j in range(RWKV_HEAD):
                term = p_ref[base + j] * tkk_ref[base + j:base + j + 1, :]
                acc[j % RWKV_ACC] = term if acc[j % RWKV_ACC] is None else acc[j % RWKV_ACC] + term
            sa = -((acc[0] + acc[1]) + (acc[2] + acc[3]))
            yacc = [None] * RWKV_ACC
            for j in range(RWKV_HEAD):
                row = slice(base + j, base + j + 1)
                pn = p_ref[base + j] * tw_ref[row, :] + sa * tkb_ref[row, :] + vt * tkd_ref[row, :]
                p_ref[base + j] = pn
                term = pn * tr_ref[row, :]
                yacc[j % RWKV_ACC] = term if yacc[j % RWKV_ACC] is None else yacc[j % RWKV_ACC] + term
            ys.append((yacc[0] + yacc[1]) + (yacc[2] + yacc[3]))
        y_ref[s] = _from_scan_layout(ys)

    def two_steps(h, carry):
        s = 2 * h
        stage(s + 1, bufs_b)
        advance(s, bufs_a)
        stage(jnp.minimum(s + 2, tc - 1), bufs_a)
        advance(s + 1, bufs_b)
        return carry

    stage(0, bufs_a)
    lax.fori_loop(0, tc // 2, two_steps, 0)
    yf_out[...], yb_out[...] = _split_dirs(y_ref[...])


def _rwkv_mixer(z, p, scan):
    n = z.shape[0]
    w = BRANCH_W
    row_spec = lambda cols: pl.BlockSpec((ROWS, cols), lambda g: (0, 0))
    vec = pl.BlockSpec((1, w), lambda g: (0, 0))
    fout = pl.BlockSpec((scan.rows, w), lambda g: (scan.fwd(g), 0))
    bout = pl.BlockSpec((scan.rows, w), lambda g: (scan.bwd(g), 0))
    chunk = pltpu.VMEM((scan.tc, ROWS, w), F32)
    one_step = pltpu.VMEM((LANE, LANE), F32)
    return pl.pallas_call(
        functools.partial(_rwkv_kernel, scan.cs),
        grid=(scan.n,),
        in_specs=scan.chunk_specs(RWKV_IN) + scan.halo_specs(RWKV_IN)
        + [row_spec(RWKV_IN), row_spec(RWKV_IN), row_spec(w), row_spec(w), vec, vec, vec,
           pl.BlockSpec((2 * LANE, 2 * w), lambda g: (0, 0)),
           pl.BlockSpec((RWKV_G_LORA, w), lambda g: (0, 0)),
           pl.BlockSpec((3 * w, w), lambda g: (0, 0))],
        out_specs=[fout, bout, fout, bout, fout],
        out_shape=[jax.ShapeDtypeStruct((n, w), F32)] * 5,
        scratch_shapes=[chunk] * 6 + [one_step] * 12 + [chunk, pltpu.VMEM((2 * RWKV_HEAD, RWKV_VROWS, LANE), F32)],
        compiler_params=_cparams("arbitrary"),
        name="rwkv_mixer",
    )(z, z, z, z, z, z, p["mua"], p["mub"], p["w0"], p["a0"], p["kk"], p["ka"], p["rk"], p["wl"], p["g2"], p["blk"])


S5_SLICES = BRANCH_W // LANE
S5_SLICE_W = S5_W // S5_SLICES
S5_COLS = 512


def _s5_kernel(uf_ref, ub_ref, ar_ref, ai_ref, wdr_ref, wdi_ref, wc_ref, yf_out, yb_out,
               dr_ref, di_ref, hr_ref, hi_ref):
    @pl.when(pl.program_id(0) == 0)
    def _():
        hr_ref[...] = jnp.zeros_like(hr_ref)
        hi_ref[...] = jnp.zeros_like(hi_ref)

    u = _merge_dirs(uf_ref[...], ub_ref[...])
    tc = u.shape[0]
    m = tc * ROWS
    u2 = u.reshape(m, BRANCH_W)
    bwd = _backward_rows((m, LANE))
    for q in range(S5_SLICES):
        lhs = _by_direction(u2[:, q * LANE:(q + 1) * LANE], bwd).astype(BF16)
        dr_ref[:, q * S5_SLICE_W:(q + 1) * S5_SLICE_W] = _dot(lhs, wdr_ref[q])
        di_ref[:, q * S5_SLICE_W:(q + 1) * S5_SLICE_W] = _dot(lhs, wdi_ref[q])

    for cg in range(S5_W // S5_COLS):
        cols = slice(cg * S5_COLS, (cg + 1) * S5_COLS)
        ar = ar_ref[:, cols]
        ai = ai_ref[:, cols]

        def step(s, carry, cols=cols, ar=ar, ai=ai):
            hr, hi = carry
            rows = pl.ds(pl.multiple_of(s * ROWS, ROWS), ROWS)
            nr = ar * hr - ai * hi + dr_ref[rows, cols]
            ni = ar * hi + ai * hr + di_ref[rows, cols]
            dr_ref[rows, cols] = nr
            di_ref[rows, cols] = ni
            return nr, ni

        hr, hi = lax.fori_loop(0, tc, step, (hr_ref[:, cols], hi_ref[:, cols]))
        hr_ref[:, cols] = hr
        hi_ref[:, cols] = hi

    ys = []
    for q in range(S5_SLICES):
        hre = dr_ref[:, q * S5_SLICE_W:(q + 1) * S5_SLICE_W]
        him = di_ref[:, q * S5_SLICE_W:(q + 1) * S5_SLICE_W]
        both = _dot(jnp.concatenate([hre, him], axis=-1).astype(BF16), wc_ref[q])
        ys.append(jnp.where(bwd, both[:, LANE:], both[:, :LANE]))
    y = jnp.concatenate(ys, axis=-1).reshape(tc, ROWS, BRANCH_W)
    yf_out[...], yb_out[...] = _split_dirs(y)


def _s5_mixer(z, p, scan):
    n = z.shape[0]
    m = scan.tc * ROWS
    w = BRANCH_W
    return pl.pallas_call(
        _s5_kernel,
        grid=(scan.n,),
        in_specs=scan.chunk_specs(w)
        + [pl.BlockSpec((ROWS, S5_W), lambda g: (0, 0)),
           pl.BlockSpec((ROWS, S5_W), lambda g: (0, 0)),
           pl.BlockSpec((S5_SLICES, 2 * LANE, S5_SLICE_W), lambda g: (0, 0, 0)),
           pl.BlockSpec((S5_SLICES, 2 * LANE, S5_SLICE_W), lambda g: (0, 0, 0)),
           pl.BlockSpec((S5_SLICES, 2 * S5_SLICE_W, 2 * LANE), lambda g: (0, 0, 0))],
        out_specs=[pl.BlockSpec((scan.rows, w), lambda g: (scan.fwd(g), 0)),
                   pl.BlockSpec((scan.rows, w), lambda g: (scan.bwd(g), 0))],
        out_shape=[jax.ShapeDtypeStruct((n, w), F32)] * 2,
        scratch_shapes=[pltpu.VMEM((m, S5_W), F32), pltpu.VMEM((m, S5_W), F32),
                        pltpu.VMEM((ROWS, S5_W), F32), pltpu.VMEM((ROWS, S5_W), F32)],
        compiler_params=_cparams("arbitrary"),
        name="s5_mixer",
    )(z, z, p["ar"], p["ai"], p["wdr"], p["wdi"], p["wc"])


LRU_HALO = 2


def _lru_kernel(seg_chunk, xf_ref, xb_ref, xfp_ref, xfn_ref, xbn_ref, xbp_ref, cwf_ref, cwb_ref, cb_ref,
                wa_ref, wx_ref, ba_ref, bx_ref, sp_ref, hf_out, hb_out, a_ref, b_ref, hs_ref, h_ref):
    g = pl.program_id(0)

    @pl.when(g == 0)
    def _():
        h_ref[...] = jnp.zeros_like(h_ref)

    prev_ok, next_ok = _halo_valid(g, pl.num_programs(0), seg_chunk)
    x = _merge_dirs(xf_ref[...], xb_ref[...])
    tc = x.shape[0]
    ext = jnp.concatenate([_merge_dirs(xfp_ref[...], xbn_ref[...]) * prev_ok, x,
                           _merge_dirs(xfn_ref[...], xbp_ref[...]) * next_ok], axis=0)
    xcv = cb_ref[...]
    for j in range(LRU_CONV):
        xcv = xcv + ext[j:j + tc] * cwf_ref[j] + ext[LRU_HALO * 2 - j:LRU_HALO * 2 - j + tc] * cwb_ref[j]
    m = tc * ROWS
    x2 = xcv.reshape(m, BRANCH_W)
    bwd = _backward_rows((m, LANE))
    ga, gx = [], []
    for q in range(BRANCH_W // LANE):
        lhs = _by_direction(x2[:, q * LANE:(q + 1) * LANE], bwd).astype(BF16)
        ga.append(_dot(lhs, wa_ref[q]))
        gx.append(_dot(lhs, wx_ref[q]))
    gr = jax.nn.sigmoid(jnp.concatenate(ga, axis=-1).reshape(tc, ROWS, BRANCH_W) + ba_ref[...])
    gi = jax.nn.sigmoid(jnp.concatenate(gx, axis=-1).reshape(tc, ROWS, BRANCH_W) + bx_ref[...])
    log_a = -LRU_C * gr * sp_ref[...]
    a_ref[...] = jnp.exp(log_a)
    b_ref[...] = jnp.sqrt(-jnp.tanh(log_a) * (jnp.exp(2.0 * log_a) + 1.0)) * gi * xcv

    def step(s, h):
        h = a_ref[s] * h + b_ref[s]
        hs_ref[s] = h
        return h

    h_ref[...] = lax.fori_loop(0, tc, step, h_ref[...])
    hf_out[...], hb_out[...] = _split_dirs(hs_ref[...])


def _lru_mixer(z, p, scan):
    n = z.shape[0]
    w = BRANCH_W
    row_spec = pl.BlockSpec((ROWS, w), lambda g: (0, 0))
    chunk = pltpu.VMEM((scan.tc, ROWS, w), F32)
    return pl.pallas_call(
        functools.partial(_lru_kernel, scan.cs),
        grid=(scan.n,),
        in_specs=scan.chunk_specs(w) + scan.halo_specs(w)
        + [pl.BlockSpec((LRU_CONV, ROWS, w), lambda g: (0, 0, 0)),
           pl.BlockSpec((LRU_CONV, ROWS, w), lambda g: (0, 0, 0)),
           pl.BlockSpec((1, w), lambda g: (0, 0)),
           pl.BlockSpec((w // LANE, 2 * LANE, LANE), lambda g: (0, 0, 0)),
           pl.BlockSpec((w // LANE, 2 * LANE, LANE), lambda g: (0, 0, 0)),
           row_spec, row_spec, row_spec],
        out_specs=[pl.BlockSpec((scan.rows, w), lambda g: (scan.fwd(g), 0)),
                   pl.BlockSpec((scan.rows, w), lambda g: (scan.bwd(g), 0))],
        out_shape=[jax.ShapeDtypeStruct((n, w), F32)] * 2,
        scratch_shapes=[chunk, chunk, chunk, pltpu.VMEM((ROWS, w), F32)],
        compiler_params=_cparams("arbitrary"),
        name="lru_mixer",
    )(z, z, z, z, z, z, p["cwf"], p["cwb"], p["cb"], p["wa"], p["wx"], p["ba"], p["bx"], p["sp"])


def _branch_out_kernel(yf_ref, yb_ref, b0_ref, b1_ref, g_ref, lnw_ref, lnb_ref, blk_ref,
                       u_ref, sf_ref, sb_ref, d_ref, gw_ref, gb_ref,
                       hf_ref, hb_ref, gate_ref, ob_ref, os_ref, od_ref):
    blk = blk_ref[...]
    y = yf_ref[...] + yb_ref[...]
    mean = _head_sum(y, blk) * (1.0 / RWKV_HEAD)
    yc = y - mean
    var = _head_sum(yc * yc, blk) * (1.0 / RWKV_HEAD)
    yn = yc * lax.rsqrt(var + RWKV_GN_EPS) * lnw_ref[...] + lnb_ref[...]
    ob_ref[...] = ((yn + (b0_ref[...] + b1_ref[...])) * g_ref[...]).astype(BF16)

    t = _gelu(u_ref[...] * d_ref[...] + sf_ref[...] + sb_ref[...])
    os_ref[...] = (t * jax.nn.sigmoid(_dot(t.astype(BF16), gw_ref[...]) + gb_ref[...])).astype(BF16)

    od_ref[...] = ((hf_ref[...] + hb_ref[...]) * _gelu(gate_ref[...])).astype(BF16)


def _branch_out(rw, s5, lru, p, tm):
    n = rw[0].shape[0]
    w = BRANCH_W
    tok = pl.BlockSpec((tm, w), lambda i: (i, 0))
    vec = pl.BlockSpec((1, w), lambda i: (0, 0))
    mat = pl.BlockSpec((w, w), lambda i: (0, 0))
    return pl.pallas_call(
        _branch_out_kernel,
        grid=(n // tm,),
        in_specs=[tok] * 5 + [vec, vec, pl.BlockSpec((3 * w, w), lambda i: (0, 0))] + [tok] * 3 + [vec, mat, vec]
        + [tok, tok, pl.BlockSpec((tm, w), lambda i: (i, 1))],
        out_specs=[tok] * 3,
        out_shape=[jax.ShapeDtypeStruct((n, w), BF16)] * 3,
        compiler_params=_cparams("parallel"),
        name="branch_out",
    )(*rw, p["ln_w"], p["ln_b"], p["blk"], *s5, p["s5_d"], p["glu_w"], p["glu_b"], *lru)


def _rows_by_direction(p):
    return jnp.repeat(p.astype(F32), HALF, axis=0)


def _block_diag(blocks):
    n, a, b = blocks.shape
    eye = jnp.eye(n, dtype=blocks.dtype)
    return jnp.einsum('nab,nm->namb', blocks, eye).reshape(n * a, n * b)


def _by_slices(fn, per_dir, n_slices, per_slice):
    return jnp.stack([jnp.stack([fn(per_dir[dd, q * per_slice:(q + 1) * per_slice]) for q in range(n_slices)])
                      for dd in range(2)])


def _rope_tables(seq, nt):
    rows = seq // GRID_W
    row = jnp.repeat(jnp.arange(rows, dtype=F32), GRID_W)
    col = jnp.tile(jnp.arange(GRID_W, dtype=F32), rows)
    n_pair = MLA_ROPE // 4
    inv = ROPE_BASE ** (-jnp.arange(n_pair, dtype=F32) / n_pair)
    ang = jnp.concatenate([row[:, None] * inv, col[:, None] * inv], -1)
    cos = jnp.concatenate([jnp.cos(ang), jnp.ones((nt, MLA_ROPE // 2), F32)], axis=0)
    sin = jnp.concatenate([jnp.sin(ang), jnp.zeros((nt, MLA_ROPE // 2), F32)], axis=0)
    n = cos.shape[0]
    z32 = jnp.zeros((n, MLA_ROPE // 2), F32)
    z64 = jnp.zeros((n, MLA_ROPE), F32)
    one = jnp.ones((n, MLA_NOPE), F32)
    z128 = jnp.zeros((n, MLA_NOPE), F32)
    qc = MLA_SCALE * jnp.concatenate([one, cos, cos, z64], axis=1)
    qs1 = MLA_SCALE * jnp.concatenate([z128, -sin, z32, z64], axis=1)
    qs2 = MLA_SCALE * jnp.concatenate([z128, z32, sin, z64], axis=1)
    rep = lambda t: jnp.repeat(t, HALF, axis=0)
    kc = rep(jnp.concatenate([cos, cos, z64], axis=1))
    ks1 = rep(jnp.concatenate([-sin, z32, z64], axis=1))
    ks2 = rep(jnp.concatenate([z32, sin, z64], axis=1))
    return qc, qs1, qs2, kc, ks1, ks2


def _time_batch_to_batch_time(tm):
    nt = tm // HALF
    dst = jnp.arange(tm)
    src = (dst % nt) * HALF + dst // nt
    return (src[:, None] == jnp.arange(tm)[None, :]).astype(BF16)


def _s5_discretise(lam_re, lam_im, log_dt, b_re, b_im):
    lre = jnp.minimum(lam_re.astype(F32), -1e-4)
    lim = lam_im.astype(F32)
    dt = jnp.exp(log_dt.astype(F32))[..., None]
    mag = jnp.exp(lre * dt)
    ar, ai = mag * jnp.cos(lim * dt), mag * jnp.sin(lim * dt)
    den = lre * lre + lim * lim
    nr, ni = ar - 1.0, ai
    cr = (nr * lre + ni * lim) / den
    ci = (ni * lre - nr * lim) / den
    b_re, b_im = b_re.astype(F32), b_im.astype(F32)
    br = cr[..., None] * b_re - ci[..., None] * b_im
    bi = cr[..., None] * b_im + ci[..., None] * b_re
    return ar, ai, br, bi


def kernel(x, c, ctx, c_ctx, ada_w, ada_b, norm_w, ffn_w1, ffn_w3, ffn_w2, w_in, gate_b, mla_q_norm, mla_w_uq, mla_kv_norm, mla_w_ukv, rwkv_mu, rwkv_w0, rwkv_w2, rwkv_a0, rwkv_a2, rwkv_g2, rwkv_kk, rwkv_ka, rwkv_rk, rwkv_ln_w, rwkv_ln_b, s5_lam_re, s5_lam_im, s5_log_dt, s5_b_re, s5_b_im, s5_c_re, s5_c_im, s5_d, s5_glu_w, s5_glu_b, lru_conv_w, lru_conv_b, lru_wa, lru_ba, lru_wx, lru_bx, lru_lam, w_branch, w_out):
    batch, seq, d = x.shape
    ctx_len = ctx.shape[1]
    depth = ada_w.shape[0]
    assert batch == HALF and d == D_MODEL
    n_lat, n_ctx = batch * seq, batch * ctx_len
    w = BRANCH_W

    rows_big = _Rows(n_lat, n_ctx, 512)
    rows = _Rows(n_lat, n_ctx, 256)
    scan = _Scan(seq, ctx_len, 64)

    w1 = ffn_w1.astype(BF16)
    w3 = ffn_w3.astype(BF16)
    w2 = ffn_w2.astype(BF16)
    perm = jnp.concatenate([jnp.arange(0, MLA_ROPE, 2), jnp.arange(1, MLA_ROPE, 2)])
    kr0 = MLA_Q_LORA + MLA_KV_LORA
    w_mla = jnp.concatenate([w_in[:, :, :kr0], w_in[:, :, kr0 + perm],
                             jnp.zeros((depth, d, MLA_IN_PAD - MLA_IN), F32)], axis=-1).astype(BF16)
    w_rwkv = w_in[:, :, O_RWKV:O_S5].astype(BF16)
    w_s5 = w_in[:, :, O_S5:O_LRU].astype(BF16)
    w_lru = w_in[:, :, O_LRU:MIX_IN].astype(BF16)
    w_gate = w_in[:, :, MIX_IN:].astype(BF16)
    w_br = w_branch.astype(BF16)
    w_o = w_out.astype(BF16)
    nw = norm_w.reshape(depth, 6, 1, d)
    gb = gate_b.reshape(depth, N_BRANCH, 1, d)

    hd = MLA_NOPE + MLA_ROPE
    qcols = jnp.concatenate([jnp.arange(MLA_NOPE), MLA_NOPE + perm])
    wq = mla_w_uq.reshape(depth, MLA_Q_LORA, MLA_HEADS, hd)[..., qcols]
    wq = jnp.concatenate([wq, jnp.zeros((depth, MLA_Q_LORA, MLA_HEADS, MLA_HEAD_PAD - hd), F32)], axis=-1)
    wq = wq.reshape(depth, MLA_Q_LORA, MLA_HEADS * MLA_HEAD_PAD).astype(BF16)
    wkv = mla_w_ukv.astype(BF16)
    qn = mla_q_norm.reshape(depth, 1, MLA_Q_LORA)
    kvn = mla_kv_norm.reshape(depth, 1, MLA_KV_LORA)
    tabs = _rope_tables(seq, rows.tm // HALF)
    perm_bt = _time_batch_to_batch_time(rows.tm)
    perm_tb = _time_batch_to_batch_time(rows_big.tm).T

    blk = _block_diag(jnp.ones((RWKV_HEADS, RWKV_HEAD, RWKV_HEAD), BF16))
    blk = jnp.concatenate([blk, blk, blk], axis=0)

    cc = jnp.concatenate([c, c_ctx[None], jnp.zeros((ROWS - batch - 1, d), F32)], axis=0)
    mods = _modulation(cc, ada_w, ada_b).reshape(depth, ROWS, N_MOD, d)
    m_lat = jnp.concatenate([mods[:, :batch], mods[:, :batch]], axis=1).transpose(0, 2, 1, 3)
    m_ctx = jnp.broadcast_to(mods[:, batch][:, :, None, :], (depth, N_MOD, ROWS, d))
    mods = jnp.stack([m_lat, m_ctx], axis=1)

    xs = jnp.concatenate([x.transpose(1, 0, 2).reshape(n_lat, d), ctx.transpose(1, 0, 2).reshape(n_ctx, d)], axis=0)

    for l in range(depth):
        xs = _ffn_half(xs, mods, nw, w1, w3, w2, rows_big, l, 0, 0, 0, 1)
        z_mla, z_rwkv, z_s5, z_lru = _in_projection(xs, mods, nw, [w_mla, w_rwkv, w_s5, w_lru], rows, l)

        q, k, v = _mla_project(z_mla, qn, kvn, wq, wkv, perm_bt, tabs, rows, l)
        ya = _mla_attention(q, k, v, seq, ctx_len, 512)

        mu = rwkv_mu[l]
        wl = jnp.concatenate([
            jnp.concatenate([jnp.concatenate([rwkv_w2[l, dd], jnp.zeros((RWKV_DECAY_LORA, w), F32)], axis=1),
                             jnp.concatenate([jnp.zeros((RWKV_A_LORA, w), F32), rwkv_a2[l, dd]], axis=1)], axis=0)
            for dd in range(2)], axis=0).astype(BF16)
        rp = {"mua": _rows_by_direction(mu), "mub": _rows_by_direction(mu[::-1]),
              "w0": _rows_by_direction(rwkv_w0[l]), "a0": _rows_by_direction(rwkv_a0[l]),
              "kk": rwkv_kk[l].reshape(1, w), "ka": rwkv_ka[l].reshape(1, w),
              "rk": rwkv_rk[l].reshape(1, w), "wl": wl, "g2": rwkv_g2[l].astype(BF16), "blk": blk}
        rw = _rwkv_mixer(z_rwkv, rp, scan)

        ar, ai, br, bi = _s5_discretise(s5_lam_re[l], s5_lam_im[l], s5_log_dt[l], s5_b_re[l], s5_b_im[l])
        gps = S5_GROUPS // S5_SLICES
        tdiag = lambda blocks: _block_diag(blocks.astype(F32).transpose(0, 2, 1))
        drive = lambda bm: (lambda bd: jnp.concatenate([bd[0], bd[1]], axis=1).astype(BF16))(
            _by_slices(tdiag, bm, S5_SLICES, gps))
        cre = _by_slices(tdiag, s5_c_re[l], S5_SLICES, gps)
        cim = _by_slices(tdiag, s5_c_im[l], S5_SLICES, gps)
        sp = {"ar": _rows_by_direction(ar.reshape(2, S5_W)), "ai": _rows_by_direction(ai.reshape(2, S5_W)),
              "wdr": drive(br), "wdi": drive(bi),
              "wc": jnp.concatenate([jnp.concatenate([cre[0], cre[1]], axis=2),
                                     jnp.concatenate([-cim[0], -cim[1]], axis=2)], axis=1).astype(BF16)}
        sf, sb = _s5_mixer(z_s5, sp, scan)

        diag = lambda blocks: _block_diag(blocks.astype(F32))
        lru_gate = lambda wm: (lambda bd: jnp.concatenate([bd[0], bd[1]], axis=1).astype(BF16))(
            _by_slices(diag, wm, w // LANE, LANE // LRU_BLOCK))
        fwd_rows = (jnp.arange(ROWS) < HALF).astype(F32)[None, :, None]
        cw = jnp.broadcast_to(lru_conv_w[l].astype(F32)[:, None, :], (LRU_CONV, ROWS, w))
        lp = {"cwf": cw * fwd_rows, "cwb": cw * (1.0 - fwd_rows), "cb": lru_conv_b[l].reshape(1, w),
              "wa": lru_gate(lru_wa[l]), "wx": lru_gate(lru_wx[l]),
              "ba": _rows_by_direction(lru_ba[l]), "bx": _rows_by_direction(lru_bx[l]),
              "sp": _rows_by_direction(jax.nn.softplus(-lru_lam[l].astype(F32)))}
        hf, hb = _lru_mixer(z_lru, lp, scan)

        bp = {"ln_w": rwkv_ln_w[l].reshape(1, w), "ln_b": rwkv_ln_b[l].reshape(1, w), "blk": blk,
              "s5_d": s5_d[l].reshape(1, w), "glu_w": s5_glu_w[l].astype(BF16), "glu_b": s5_glu_b[l].reshape(1, w)}
        yb_o, ys_o, yd_o = _branch_out(rw, (z_s5, sf, sb), (hf, hb, z_lru), bp, rows.tm)

        m = _gated_sum(xs, mods, nw, ya, yb_o, ys_o, yd_o, perm_tb, w_gate, gb, w_br, rows_big, l)
        xs = _out_projection(m, w_o, xs, mods, nw, rows_big, l)
        xs = _ffn_half(xs, mods, nw, w1, w3, w2, rows_big, l, 1, 2, 4, 5)

    return xs[:n_lat].reshape(seq, batch, d).transpose(1, 0, 2)
```

```python
import functools
import math

import jax
import jax.numpy as jnp
from jax import lax
from jax.experimental import pallas as pl
from jax.experimental.pallas import tpu as pltpu

F32 = jnp.float32
BF16 = jnp.bfloat16

D_MODEL = 2048
N_BRANCH = 4
BRANCH_W = 512
D_FF = 5632
FFN_RES = 0.5
N_MOD = 9
NORM_EPS = 1e-6
GRID_W = 64

MLA_HEADS = 4
MLA_NOPE = 128
MLA_ROPE = 64
MLA_V = 128
MLA_Q_LORA = 768
MLA_KV_LORA = 512
MLA_SCALE = (MLA_NOPE + MLA_ROPE) ** -0.5
ROPE_BASE = 10000.0
MLA_HEAD_PAD = 256
MLA_IN_PAD = 1408

RWKV_HEAD = 64
RWKV_HEADS = BRANCH_W // RWKV_HEAD
RWKV_DECAY_LORA = 64
RWKV_A_LORA = 64
RWKV_G_LORA = 128
RWKV_GN_EPS = 64e-5
RWKV_IN = 3 * BRANCH_W + RWKV_DECAY_LORA + RWKV_A_LORA + RWKV_G_LORA

S5_GROUP = 16
S5_GROUPS = BRANCH_W // S5_GROUP
S5_STATE = 64
S5_W = S5_GROUPS * S5_STATE

LRU_BLOCKS = 8
LRU_BLOCK = BRANCH_W // LRU_BLOCKS
LRU_CONV = 4
LRU_C = 8.0

MLA_IN = MLA_Q_LORA + MLA_KV_LORA + MLA_ROPE
O_RWKV = MLA_IN
O_S5 = O_RWKV + RWKV_IN
O_LRU = O_S5 + BRANCH_W
MIX_IN = O_LRU + 2 * BRANCH_W

ROWS = 8
HALF = ROWS // 2
LANE = 128
VMEM_LIMIT = 56 * 1024 * 1024


def _cparams(*sem):
    return pltpu.CompilerParams(dimension_semantics=sem, vmem_limit_bytes=VMEM_LIMIT)


def _rms(x, w):
    return x * lax.rsqrt(jnp.mean(x * x, axis=-1, keepdims=True) + NORM_EPS) * w


def _dot(a, b):
    return jnp.dot(a, b, preferred_element_type=F32)


def _gelu(x):
    return 0.5 * x * (1.0 + jnp.tanh(math.sqrt(2.0 / math.pi) * (x + 0.044715 * (x * x * x))))


def _rows8(x):
    return x.reshape(x.shape[0] // ROWS, ROWS, x.shape[1])


def _mod_in(x, g, mod_ref):
    return (_rows8(_rms(x, g)) * (1.0 + mod_ref[1]) + mod_ref[0]).reshape(x.shape)


def _mod_gate(y, g, mod_ref):
    return (_rows8(_rms(y, g)) * mod_ref[2]).reshape(y.shape)


def _mod_kernel(cc_ref, w_ref, b_ref, o_ref):
    cc = cc_ref[...]
    s = (cc * jax.nn.sigmoid(cc)).astype(BF16)
    o_ref[...] = _dot(s, w_ref[...].astype(BF16)) + b_ref[...]


def _modulation(cc, ada_w, ada_b):
    L, D, N = ada_w.shape
    tn = 1024
    return pl.pallas_call(
        _mod_kernel,
        grid=(L, N // tn),
        in_specs=[pl.BlockSpec((ROWS, D), lambda l, j: (0, 0)),
                  pl.BlockSpec((None, D, tn), lambda l, j: (l, 0, j)),
                  pl.BlockSpec((None, 1, tn), lambda l, j: (l, 0, j))],
        out_specs=pl.BlockSpec((None, ROWS, tn), lambda l, j: (l, 0, j)),
        out_shape=jax.ShapeDtypeStruct((L, ROWS, N), F32),
        compiler_params=_cparams("parallel", "parallel"),
        name="modulation",
    )(cc, ada_w, ada_b.reshape(L, 1, N))


class _Rows:
    def __init__(self, n_lat, n_ctx, tm):
        assert n_lat % tm == 0 and n_ctx % tm == 0
        self.tm = tm
        self.n_tiles = (n_lat + n_ctx) // tm
        self.lat_tiles = n_lat // tm

    def segment(self, i):
        return (i >= self.lat_tiles).astype(jnp.int32)

    def pos_tile(self, i):
        return jnp.minimum(i, self.lat_tiles)


def _mod_spec(rows, l, group, grid_rank):
    if grid_rank == 2:
        return pl.BlockSpec((None, None, 3, ROWS, D_MODEL), lambda i, j: (l, rows.segment(i), group, 0, 0))
    return pl.BlockSpec((None, None, 3, ROWS, D_MODEL), lambda i: (l, rows.segment(i), group, 0, 0))


def _nw_spec(l, idx, grid_rank):
    if grid_rank == 2:
        return pl.BlockSpec((None, None, 1, D_MODEL), lambda i, j: (l, idx, 0, 0))
    return pl.BlockSpec((None, None, 1, D_MODEL), lambda i: (l, idx, 0, 0))


def _ffn_kernel(x_ref, mod_ref, gpre_ref, gpost_ref, w1_ref, w3_ref, w2_ref, o_ref, h_ref, acc_ref):
    j = pl.program_id(1)

    @pl.when(j == 0)
    def _():
        h_ref[...] = _mod_in(x_ref[...], gpre_ref[...], mod_ref).astype(BF16)
        acc_ref[...] = jnp.zeros_like(acc_ref)

    h = h_ref[...]
    a = _dot(h, w1_ref[...])
    b = _dot(h, w3_ref[...])
    u = (a * jax.nn.sigmoid(a) * b).astype(BF16)
    acc_ref[...] += _dot(u, w2_ref[...])

    @pl.when(j == pl.num_programs(1) - 1)
    def _():
        o_ref[...] = x_ref[...] + FFN_RES * _mod_gate(acc_ref[...], gpost_ref[...], mod_ref)


def _ffn_half(x, mods, nw, w1, w3, w2, rows, l, hf, group, i_pre, i_post):
    n, d = x.shape
    tm, tf = rows.tm, 512
    return pl.pallas_call(
        _ffn_kernel,
        grid=(rows.n_tiles, D_FF // tf),
        in_specs=[pl.BlockSpec((tm, d), lambda i, j: (i, 0)),
                  _mod_spec(rows, l, group, 2),
                  _nw_spec(l, i_pre, 2), _nw_spec(l, i_post, 2),
                  pl.BlockSpec((None, None, d, tf), lambda i, j: (l, hf, 0, j)),
                  pl.BlockSpec((None, None, d, tf), lambda i, j: (l, hf, 0, j)),
                  pl.BlockSpec((None, None, tf, d), lambda i, j: (l, hf, j, 0))],
        out_specs=pl.BlockSpec((tm, d), lambda i, j: (i, 0)),
        out_shape=jax.ShapeDtypeStruct((n, d), F32),
        scratch_shapes=[pltpu.VMEM((tm, d), BF16), pltpu.VMEM((tm, d), F32)],
        compiler_params=_cparams("parallel", "arbitrary"),
        name="ffn_half",
    )(x, mods, nw, nw, w1, w3, w2)


MIX_WIDTHS = (MLA_IN_PAD, RWKV_IN, BRANCH_W, 2 * BRANCH_W)
MIX_PACKED = sum(MIX_WIDTHS)
GATE_BLOCK0 = -(-MIX_PACKED // D_MODEL)


def _inproj_kernel(x_ref, mod_ref, g_ref, w_ref, *o_refs):
    h = _mod_in(x_ref[...], g_ref[...], mod_ref).astype(BF16)
    lo = 0
    for o_ref, wd in zip(o_refs, MIX_WIDTHS):
        o_ref[...] = _dot(h, w_ref[:, lo:lo + wd])
        lo += wd


def _in_projection(x, mods, nw, w_packed, rows, l):
    n, d = x.shape
    tm = rows.tm
    return pl.pallas_call(
        _inproj_kernel,
        grid=(rows.n_tiles,),
        in_specs=[pl.BlockSpec((tm, d), lambda i: (i, 0)), _mod_spec(rows, l, 1, 1), _nw_spec(l, 2, 1),
                  pl.BlockSpec((None, d, MIX_PACKED), lambda i: (l, 0, 0))],
        out_specs=[pl.BlockSpec((tm, wd), lambda i: (i, 0)) for wd in MIX_WIDTHS],
        out_shape=[jax.ShapeDtypeStruct((n, wd), F32) for wd in MIX_WIDTHS],
        compiler_params=_cparams("parallel"),
        name="in_projection",
    )(x, mods, nw, w_packed)


def _gate_kernel(x_ref, mod_ref, g_ref, ya_ref, yb_ref, ys_ref, yd_ref, pt_ref, wg_ref, gb_ref, wb_ref,
                 o_ref, h_ref, acc_ref, y_ref):
    k = pl.program_id(1)
    tm = x_ref.shape[0]

    @pl.when(k == 0)
    def _():
        h_ref[...] = _mod_in(x_ref[...], g_ref[...], mod_ref).astype(BF16)
        acc_ref[...] = jnp.zeros_like(acc_ref)
        y_ref[...] = _dot(pt_ref[...], ya_ref[...].reshape(tm, BRANCH_W)).astype(BF16)

    for kk, ref in ((1, yb_ref), (2, ys_ref), (3, yd_ref)):
        @pl.when(k == kk)
        def _(ref=ref):
            y_ref[...] = ref[...]

    zg = _dot(h_ref[...], wg_ref[...]) + gb_ref[...]
    acc_ref[...] += jax.nn.sigmoid(zg) * _dot(y_ref[...], wb_ref[...])

    @pl.when(k == pl.num_programs(1) - 1)
    def _():
        o_ref[...] = acc_ref[...].astype(BF16)


def _gated_sum(x, mods, nw, ya, yb, ys, yd, perm_t, w_gate, gate_b, w_branch, rows, l):
    n, d = x.shape
    tm = rows.tm
    tok = pl.BlockSpec((tm, BRANCH_W), lambda i, k: (i, 0))
    return pl.pallas_call(
        _gate_kernel,
        grid=(rows.n_tiles, N_BRANCH),
        in_specs=[pl.BlockSpec((tm, d), lambda i, k: (i, 0)),
                  _mod_spec(rows, l, 1, 2), _nw_spec(l, 2, 2),
                  pl.BlockSpec((HALF, tm // HALF, BRANCH_W), lambda i, k: (0, i, 0)),
                  tok, tok, tok,
                  pl.BlockSpec((tm, tm), lambda i, k: (0, 0)),
                  pl.BlockSpec((None, d, d), lambda i, k: (l, 0, GATE_BLOCK0 + k)),
                  pl.BlockSpec((None, None, 1, d), lambda i, k: (l, k, 0, 0)),
                  pl.BlockSpec((None, None, BRANCH_W, d), lambda i, k: (l, k, 0, 0))],
        out_specs=pl.BlockSpec((tm, d), lambda i, k: (i, 0)),
        out_shape=jax.ShapeDtypeStruct((n, d), BF16),
        scratch_shapes=[pltpu.VMEM((tm, d), BF16), pltpu.VMEM((tm, d), F32), pltpu.VMEM((tm, BRANCH_W), BF16)],
        compiler_params=_cparams("parallel", "arbitrary"),
        name="gated_sum",
    )(x, mods, nw, ya, yb, ys, yd, perm_t, w_gate, gate_b, w_branch)


def _outproj_kernel(m_ref, w_ref, x_ref, mod_ref, g_ref, o_ref):
    m = _dot(m_ref[...], w_ref[...])
    o_ref[...] = x_ref[...] + _mod_gate(m, g_ref[...], mod_ref)


def _out_projection(m, w_out, x, mods, nw, rows, l):
    n, d = x.shape
    tm = rows.tm
    return pl.pallas_call(
        _outproj_kernel,
        grid=(rows.n_tiles,),
        in_specs=[pl.BlockSpec((tm, d), lambda i: (i, 0)),
                  pl.BlockSpec((None, d, d), lambda i: (l, 0, 0)),
                  pl.BlockSpec((tm, d), lambda i: (i, 0)),
                  _mod_spec(rows, l, 1, 1), _nw_spec(l, 3, 1)],
        out_specs=pl.BlockSpec((tm, d), lambda i: (i, 0)),
        out_shape=jax.ShapeDtypeStruct((n, d), F32),
        compiler_params=_cparams("parallel"),
        name="out_projection",
    )(m, w_out, x, mods, nw)


def _rope_lanes(x, c, s_lo, s_hi, width):
    return x * c + pltpu.roll(x, width - MLA_ROPE // 2, 1) * s_lo + pltpu.roll(x, MLA_ROPE // 2, 1) * s_hi


def _mla_proj_kernel(z_ref, qn_ref, kvn_ref, wq_ref, wkv_ref, perm_ref, qc_ref, qs1_ref, qs2_ref,
                     kc_ref, ks1_ref, ks2_ref, q_ref, k_ref, v_ref):
    z = z_ref[...]
    tm = z.shape[0]
    nt = tm // HALF
    pm = perm_ref[...]
    cq = _dot(pm, _rms(z[:, :MLA_Q_LORA], qn_ref[...]).astype(BF16)).astype(BF16)
    ckv = _dot(pm, _rms(z[:, MLA_Q_LORA:MLA_Q_LORA + MLA_KV_LORA], kvn_ref[...]).astype(BF16)).astype(BF16)
    kr = z[:, MLA_Q_LORA + MLA_KV_LORA:]
    krr = _rope_lanes(kr, kc_ref[...], ks1_ref[...], ks2_ref[...], LANE).astype(BF16)
    krr = _dot(pm, krr).astype(BF16)
    q = _dot(cq, wq_ref[...])
    kv = _dot(ckv, wkv_ref[...])
    tile4 = lambda t: jnp.concatenate([t] * HALF, axis=0)
    qc, qs1, qs2 = tile4(qc_ref[...]), tile4(qs1_ref[...]), tile4(qs2_ref[...])
    for h in range(MLA_HEADS):
        qh = _rope_lanes(q[:, h * MLA_HEAD_PAD:(h + 1) * MLA_HEAD_PAD], qc, qs1, qs2, MLA_HEAD_PAD).astype(BF16)
        kn = kv[:, h * 256:h * 256 + MLA_NOPE].astype(BF16)
        vh = kv[:, h * 256 + MLA_NOPE:(h + 1) * 256].astype(BF16)
        for b in range(HALF):
            rs = slice(b * nt, (b + 1) * nt)
            q_ref[b, :, h * MLA_HEAD_PAD:(h + 1) * MLA_HEAD_PAD] = qh[rs]
            k_ref[b, :, h * MLA_HEAD_PAD:h * MLA_HEAD_PAD + MLA_NOPE] = kn[rs]
            k_ref[b, :, h * MLA_HEAD_PAD + MLA_NOPE:(h + 1) * MLA_HEAD_PAD] = krr[rs]
            v_ref[b, :, h * MLA_V:(h + 1) * MLA_V] = vh[rs]


def _mla_project(z, qn, kvn, wq, wkv, perm, tabs, rows, l):
    n = z.shape[0]
    tm = rows.tm
    nt = tm // HALF
    hq = MLA_HEADS * MLA_HEAD_PAD
    tab_q = pl.BlockSpec((nt, MLA_HEAD_PAD), lambda i: (rows.pos_tile(i), 0))
    tab_k = pl.BlockSpec((tm, LANE), lambda i: (rows.pos_tile(i), 0))
    out = lambda wd: pl.BlockSpec((HALF, nt, wd), lambda i: (0, i, 0))
    return pl.pallas_call(
        _mla_proj_kernel,
        grid=(rows.n_tiles,),
        in_specs=[pl.BlockSpec((tm, MLA_IN_PAD), lambda i: (i, 0)),
                  pl.BlockSpec((None, 1, MLA_Q_LORA), lambda i: (l, 0, 0)),
                  pl.BlockSpec((None, 1, MLA_KV_LORA), lambda i: (l, 0, 0)),
                  pl.BlockSpec((None, MLA_Q_LORA, hq), lambda i: (l, 0, 0)),
                  pl.BlockSpec((None, MLA_KV_LORA, hq), lambda i: (l, 0, 0)),
                  pl.BlockSpec((tm, tm), lambda i: (0, 0)),
                  tab_q, tab_q, tab_q, tab_k, tab_k, tab_k],
        out_specs=[out(hq), out(hq), out(MLA_HEADS * MLA_V)],
        out_shape=[jax.ShapeDtypeStruct((HALF, n // HALF, hq), BF16),
                   jax.ShapeDtypeStruct((HALF, n // HALF, hq), BF16),
                   jax.ShapeDtypeStruct((HALF, n // HALF, MLA_HEADS * MLA_V), BF16)],
        compiler_params=_cparams("parallel"),
        name="mla_project",
    )(z, qn, kvn, wq, wkv, perm, *tabs)


def _scores(q, k):
    return lax.dot_general(q, k, (((1,), (1,)), ((), ())), preferred_element_type=F32)


def _attn_kernel(q_ref, k1_ref, v1_ref, k2_ref, v2_ref, o_ref):
    q = q_ref[...]
    s1 = _scores(q, k1_ref[...])
    s2 = _scores(q, k2_ref[...])
    m = jnp.maximum(jnp.max(s1, axis=-1, keepdims=True), jnp.max(s2, axis=-1, keepdims=True))
    p1 = jnp.exp(s1 - m)
    p2 = jnp.exp(s2 - m)
    den = jnp.sum(p1, axis=-1, keepdims=True) + jnp.sum(p2, axis=-1, keepdims=True)
    o = _dot(p1.astype(BF16), v1_ref[...]) + _dot(p2.astype(BF16), v2_ref[...])
    o_ref[...] = (o / den).astype(BF16)


def _attn_ctx_kernel(q_ref, k_ref, v_ref, prev_ref, o_ref):
    del prev_ref
    s = _scores(q_ref[...], k_ref[...])
    p = jnp.exp(s - jnp.max(s, axis=-1, keepdims=True))
    o = _dot(p.astype(BF16), v_ref[...])
    o_ref[...] = (o / jnp.sum(p, axis=-1, keepdims=True)).astype(BF16)


def _mla_attention(q, k, v, seq, ctx_len, tq):
    batch, s_all, _ = q.shape
    cb = seq // ctx_len
    qt = seq // tq
    lat = pl.pallas_call(
        _attn_kernel,
        grid=(batch, MLA_HEADS, qt),
        in_specs=[pl.BlockSpec((None, tq, MLA_HEAD_PAD), lambda b, h, i: (b, i, h)),
                  pl.BlockSpec((None, seq, MLA_HEAD_PAD), lambda b, h, i: (b, 0, h)),
                  pl.BlockSpec((None, seq, MLA_V), lambda b, h, i: (b, 0, h)),
                  pl.BlockSpec((None, ctx_len, MLA_HEAD_PAD), lambda b, h, i: (b, cb, h)),
                  pl.BlockSpec((None, ctx_len, MLA_V), lambda b, h, i: (b, cb, h))],
        out_specs=pl.BlockSpec((None, tq, MLA_V), lambda b, h, i: (b, i, h)),
        out_shape=jax.ShapeDtypeStruct((batch, s_all, MLA_HEADS * MLA_V), BF16),
        compiler_params=_cparams("parallel", "parallel", "parallel"),
        name="mla_attention",
    )(q, k, v, k, v)
    return pl.pallas_call(
        _attn_ctx_kernel,
        grid=(batch, MLA_HEADS),
        in_specs=[pl.BlockSpec((None, ctx_len, MLA_HEAD_PAD), lambda b, h: (b, cb, h)),
                  pl.BlockSpec((None, ctx_len, MLA_HEAD_PAD), lambda b, h: (b, cb, h)),
                  pl.BlockSpec((None, ctx_len, MLA_V), lambda b, h: (b, cb, h)),
                  pl.BlockSpec(memory_space=pl.ANY)],
        out_specs=pl.BlockSpec((None, ctx_len, MLA_V), lambda b, h: (b, cb, h)),
        out_shape=jax.ShapeDtypeStruct((batch, s_all, MLA_HEADS * MLA_V), BF16),
        input_output_aliases={3: 0},
        compiler_params=_cparams("parallel", "parallel"),
        name="mla_attention_ctx",
    )(q, k, v, lat)


class _Scan:
    def __init__(self, seq, ctx_len, tc):
        assert seq % tc == 0 and ctx_len % tc == 0 and tc % 2 == 0
        self.tc = tc
        self.rows = tc * HALF
        self.cs = ctx_len // tc
        self.lat = seq // tc
        self.n = self.cs + self.lat
        self.tiles = tc // 2
        self.n_tiles = self.n * self.tiles

    def fwd(self, g):
        return jnp.where(g < self.cs, self.lat + g, g - self.cs)

    def bwd(self, g):
        return jnp.where(g < self.cs, self.lat + self.cs - 1 - g, self.n - 1 - g)

    def chunk_specs(self, cols, col_block=0):
        return [pl.BlockSpec((self.rows, cols), lambda g: (self.fwd(g), col_block)),
                pl.BlockSpec((self.rows, cols), lambda g: (self.bwd(g), col_block))]

    def halo_specs(self, cols, col_block=0):
        t, last = self.tiles, self.n_tiles - 1
        return [pl.BlockSpec((ROWS, cols), lambda g: (jnp.maximum(self.fwd(g) * t - 1, 0), col_block)),
                pl.BlockSpec((ROWS, cols), lambda g: (jnp.minimum((self.fwd(g) + 1) * t, last), col_block)),
                pl.BlockSpec((ROWS, cols), lambda g: (jnp.minimum((self.bwd(g) + 1) * t, last), col_block)),
                pl.BlockSpec((ROWS, cols), lambda g: (jnp.maximum(self.bwd(g) * t - 1, 0), col_block))]


def _flip0(x):
    n = x.shape[0]
    if n == 1:
        return x
    return jnp.concatenate([x[n - 1 - k:n - k] for k in range(n)], axis=0)


def _low_rows(shape):
    return lax.broadcasted_iota(jnp.int32, shape, len(shape) - 2) % ROWS < HALF


def _merge_dirs(df, db):
    c = df.shape[-1]
    n2 = df.shape[0] // ROWS
    t = df.reshape(n2, ROWS, c)
    u = _flip0(db.reshape(n2, ROWS, c))
    lo = _low_rows(t.shape)
    even = jnp.where(lo, t, u)
    odd = pltpu.roll(jnp.where(lo, u, t), HALF, 1)
    return jnp.stack([even, odd], axis=1).reshape(2 * n2, ROWS, c)


def _split_dirs(y):
    n, _, c = y.shape
    y2 = y.reshape(n // 2, 2 * ROWS, c)
    a, b = y2[:, :ROWS], y2[:, ROWS:]
    rb = pltpu.roll(b, HALF, 1)
    lo = _low_rows(a.shape)
    df = jnp.where(lo, a, rb).reshape(n * HALF, c)
    db = _flip0(jnp.where(lo, rb, a)).reshape(n * HALF, c)
    return df, db


def _backward_rows(shape):
    return lax.broadcasted_iota(jnp.int32, shape, len(shape) - 2) % ROWS >= HALF


def _by_direction(x, bwd):
    zero = jnp.zeros_like(x)
    return jnp.concatenate([jnp.where(bwd, zero, x), jnp.where(bwd, x, zero)], axis=-1)


def _halo_valid(g, n_chunks, seg_chunk):
    prev_ok = jnp.logical_and(g != 0, g != seg_chunk)
    next_ok = jnp.logical_and(g != seg_chunk - 1, g != n_chunks - 1)
    return prev_ok.astype(F32), next_ok.astype(F32)


def _head_sum(x, blk3):
    hi = x.astype(BF16)
    r1 = x - hi.astype(F32)
    mid = r1.astype(BF16)
    lo = (r1 - mid.astype(F32)).astype(BF16)
    return _dot(jnp.concatenate([hi, mid, lo], axis=-1), blk3)


RWKV_ACC = 4
RWKV_VQ = 4
RWKV_VROWS = RWKV_HEAD // RWKV_VQ
RWKV_PAIRS = RWKV_HEADS // 2
RWKV_SEQ = RWKV_PAIRS * ROWS


def _to_scan_layout(x):
    a = jnp.concatenate([x[:, q * LANE:(q + 1) * LANE] for q in range(RWKV_PAIRS)], axis=0)
    return jnp.concatenate([a] * RWKV_VQ, axis=0).T


def _from_scan_layout(ys):
    t = jnp.concatenate([y for y in ys for _ in range(RWKV_VQ)], axis=0).T.reshape(RWKV_VQ, RWKV_SEQ, LANE)
    quarter = (lax.broadcasted_iota(jnp.int32, (RWKV_SEQ, LANE), 1) // RWKV_VROWS) % RWKV_VQ
    ya = t[RWKV_VQ - 1]
    for iq in range(RWKV_VQ - 2, -1, -1):
        ya = jnp.where(quarter == iq, t[iq], ya)
    return jnp.concatenate([ya[q * ROWS:(q + 1) * ROWS] for q in range(RWKV_PAIRS)], axis=1)


def _rwkv_kernel(seg_chunk, zf_ref, zb_ref, zfp_ref, zfn_ref, zbn_ref, zbp_ref, mua_ref, mub_ref, w0_ref, a0_ref,
                 kk_ref, ka_ref, rk_ref, wl_ref, g2_ref, blk_ref,
                 yf_out, yb_out, bf_out, bb_out, g_out, *scratch):
    nat_refs, bufs_a, bufs_b = scratch[0:6], scratch[6:12], scratch[12:18]
    y_ref, p_ref = scratch[18], scratch[19]
    g = pl.program_id(0)

    @pl.when(g == 0)
    def _():
        p_ref[...] = jnp.zeros_like(p_ref)

    prev_ok, next_ok = _halo_valid(g, pl.num_programs(0), seg_chunk)
    z = _merge_dirs(zf_ref[...], zb_ref[...])
    tc = z.shape[0]
    zprev = jnp.concatenate([_merge_dirs(zfp_ref[...], zbn_ref[...])[1:2] * prev_ok, z[:-1]], axis=0)
    znext = jnp.concatenate([z[1:], _merge_dirs(zfn_ref[...], zbp_ref[...])[0:1] * next_ok], axis=0)
    zs = z + (zprev - z) * mua_ref[...] + (znext - z) * mub_ref[...]
    w = BRANCH_W
    r, k, v = zs[..., :w], zs[..., w:2 * w], zs[..., 2 * w:3 * w]
    wa = zs[..., 3 * w:3 * w + LANE]
    gl = zs[..., 3 * w + LANE:]
    lane = lax.broadcasted_iota(jnp.int32, wa.shape, 2)
    t = jnp.where(lane < RWKV_DECAY_LORA, jnp.tanh(wa), wa)
    bwd = _backward_rows(t.shape)
    lhs = _by_direction(t, bwd).reshape(tc * ROWS, 2 * LANE).astype(BF16)
    lo = _dot(lhs, wl_ref[...]).reshape(tc, ROWS, 2 * w)
    w_pre = w0_ref[...] + lo[..., :w]
    decay = jnp.exp(-math.exp(-0.5) * jax.nn.sigmoid(w_pre))
    asig = jax.nn.sigmoid(a0_ref[...] + lo[..., w:])
    blk = blk_ref[...]
    kkv = k * kk_ref[...]
    ssq = _head_sum((kkv * kkv).reshape(tc * ROWS, w), blk).reshape(tc, ROWS, w)
    kkn = kkv * lax.rsqrt(ssq + 1e-12)
    kd = k * (1.0 + (asig - 1.0) * ka_ref[...])
    gg = _dot(jax.nn.sigmoid(gl).reshape(tc * ROWS, RWKV_G_LORA).astype(BF16), g2_ref[...]).reshape(tc, ROWS, w)
    rkd = _head_sum((r * kd * rk_ref[...]).reshape(tc * ROWS, w), blk).reshape(tc, ROWS, w)
    g_out[...] = _split_dirs(gg)[0]
    bf_out[...], bb_out[...] = _split_dirs(rkd * v)

    for ref, val in zip(nat_refs, (decay, kkn, kkn * asig, kd, r, v)):
        ref[...] = val

    def stage(s, bufs):
        for src, dst in zip(nat_refs, bufs):
            dst[...] = _to_scan_layout(src[s])

    def advance(s, bufs):
        tw_ref, tkk_ref, tkb_ref, tkd_ref, tr_ref, tv_ref = bufs
        copy = lax.broadcasted_iota(jnp.int32, (RWKV_VROWS, LANE), 1) // RWKV_SEQ
        ys = []
        for par in range(2):
            base = par * RWKV_HEAD
            vt = tv_ref[base + (RWKV_VQ - 1) * RWKV_VROWS:base + RWKV_VQ * RWKV_VROWS, :]
            for iq in range(RWKV_VQ - 2, -1, -1):
                vt = jnp.where(copy == iq, tv_ref[base + iq * RWKV_VROWS:base + (iq + 1) * RWKV_VROWS, :], vt)
            acc = [None] * RWKV_ACC
            for j in range(RWKV_HEAD):
                term = p_ref[base + j] * tkk_ref[base + j:base + j + 1, :]
                acc[j % RWKV_ACC] = term if acc[j % RWKV_ACC] is None else acc[j % RWKV_ACC] + term
            sa = -((acc[0] + acc[1]) + (acc[2] + acc[3]))
            yacc = [None] * RWKV_ACC
            for j in range(RWKV_HEAD):
                row = slice(base + j, base + j + 1)
                pn = p_ref[base + j] * tw_ref[row, :] + sa * tkb_ref[row, :] + vt * tkd_ref[row, :]
                p_ref[base + j] = pn
                term = pn * tr_ref[row, :]
                yacc[j % RWKV_ACC] = term if yacc[j % RWKV_ACC] is None else yacc[j % RWKV_ACC] + term
            ys.append((yacc[0] + yacc[1]) + (yacc[2] + yacc[3]))
        y_ref[s] = _from_scan_layout(ys)

    def two_steps(h, carry):
        s = 2 * h
        stage(s + 1, bufs_b)
        advance(s, bufs_a)
        stage(jnp.minimum(s + 2, tc - 1), bufs_a)
        advance(s + 1, bufs_b)
        return carry

    stage(0, bufs_a)
    lax.fori_loop(0, tc // 2, two_steps, 0)
    yf_out[...], yb_out[...] = _split_dirs(y_ref[...])


def _rwkv_mixer(z, p, scan):
    n = z.shape[0]
    w = BRANCH_W
    row_spec = lambda cols: pl.BlockSpec((ROWS, cols), lambda g: (0, 0))
    vec = pl.BlockSpec((1, w), lambda g: (0, 0))
    fout = pl.BlockSpec((scan.rows, w), lambda g: (scan.fwd(g), 0))
    bout = pl.BlockSpec((scan.rows, w), lambda g: (scan.bwd(g), 0))
    chunk = pltpu.VMEM((scan.tc, ROWS, w), F32)
    one_step = pltpu.VMEM((LANE, LANE), F32)
    return pl.pallas_call(
        functools.partial(_rwkv_kernel, scan.cs),
        grid=(scan.n,),
        in_specs=scan.chunk_specs(RWKV_IN) + scan.halo_specs(RWKV_IN)
        + [row_spec(RWKV_IN), row_spec(RWKV_IN), row_spec(w), row_spec(w), vec, vec, vec,
           pl.BlockSpec((2 * LANE, 2 * w), lambda g: (0, 0)),
           pl.BlockSpec((RWKV_G_LORA, w), lambda g: (0, 0)),
           pl.BlockSpec((3 * w, w), lambda g: (0, 0))],
        out_specs=[fout, bout, fout, bout, fout],
        out_shape=[jax.ShapeDtypeStruct((n, w), F32)] * 5,
        scratch_shapes=[chunk] * 6 + [one_step] * 12 + [chunk, pltpu.VMEM((2 * RWKV_HEAD, RWKV_VROWS, LANE), F32)],
        compiler_params=_cparams("arbitrary"),
        name="rwkv_mixer",
    )(z, z, z, z, z, z, p["mua"], p["mub"], p["w0"], p["a0"], p["kk"], p["ka"], p["rk"], p["wl"], p["g2"], p["blk"])


S5_SLICES = BRANCH_W // LANE
S5_SLICE_W = S5_W // S5_SLICES
S5_COLS = 512


def _s5_kernel(uf_ref, ub_ref, ar_ref, ai_ref, wdr_ref, wdi_ref, wc_ref, yf_out, yb_out,
               dr_ref, di_ref, hr_ref, hi_ref):
    @pl.when(pl.program_id(0) == 0)
    def _():
        hr_ref[...] = jnp.zeros_like(hr_ref)
        hi_ref[...] = jnp.zeros_like(hi_ref)

    u = _merge_dirs(uf_ref[...], ub_ref[...])
    tc = u.shape[0]
    m = tc * ROWS
    u2 = u.reshape(m, BRANCH_W)
    bwd = _backward_rows((m, LANE))
    for q in range(S5_SLICES):
        lhs = _by_direction(u2[:, q * LANE:(q + 1) * LANE], bwd).astype(BF16)
        dr_ref[:, q * S5_SLICE_W:(q + 1) * S5_SLICE_W] = _dot(lhs, wdr_ref[q])
        di_ref[:, q * S5_SLICE_W:(q + 1) * S5_SLICE_W] = _dot(lhs, wdi_ref[q])

    for cg in range(S5_W // S5_COLS):
        cols = slice(cg * S5_COLS, (cg + 1) * S5_COLS)
        ar = ar_ref[:, cols]
        ai = ai_ref[:, cols]

        def step(s, carry, cols=cols, ar=ar, ai=ai):
            hr, hi = carry
            rows = pl.ds(pl.multiple_of(s * ROWS, ROWS), ROWS)
            nr = ar * hr - ai * hi + dr_ref[rows, cols]
            ni = ar * hi + ai * hr + di_ref[rows, cols]
            dr_ref[rows, cols] = nr
            di_ref[rows, cols] = ni
            return nr, ni

        hr, hi = lax.fori_loop(0, tc, step, (hr_ref[:, cols], hi_ref[:, cols]))
        hr_ref[:, cols] = hr
        hi_ref[:, cols] = hi

    ys = []
    for q in range(S5_SLICES):
        hre = dr_ref[:, q * S5_SLICE_W:(q + 1) * S5_SLICE_W]
        him = di_ref[:, q * S5_SLICE_W:(q + 1) * S5_SLICE_W]
        both = _dot(jnp.concatenate([hre, him], axis=-1).astype(BF16), wc_ref[q])
        ys.append(jnp.where(bwd, both[:, LANE:], both[:, :LANE]))
    y = jnp.concatenate(ys, axis=-1).reshape(tc, ROWS, BRANCH_W)
    yf_out[...], yb_out[...] = _split_dirs(y)


def _s5_mixer(z, p, scan):
    n = z.shape[0]
    m = scan.tc * ROWS
    w = BRANCH_W
    return pl.pallas_call(
        _s5_kernel,
        grid=(scan.n,),
        in_specs=scan.chunk_specs(w)
        + [pl.BlockSpec((ROWS, S5_W), lambda g: (0, 0)),
           pl.BlockSpec((ROWS, S5_W), lambda g: (0, 0)),
           pl.BlockSpec((S5_SLICES, 2 * LANE, S5_SLICE_W), lambda g: (0, 0, 0)),
           pl.BlockSpec((S5_SLICES, 2 * LANE, S5_SLICE_W), lambda g: (0, 0, 0)),
           pl.BlockSpec((S5_SLICES, 2 * S5_SLICE_W, 2 * LANE), lambda g: (0, 0, 0))],
        out_specs=[pl.BlockSpec((scan.rows, w), lambda g: (scan.fwd(g), 0)),
                   pl.BlockSpec((scan.rows, w), lambda g: (scan.bwd(g), 0))],
        out_shape=[jax.ShapeDtypeStruct((n, w), F32)] * 2,
        scratch_shapes=[pltpu.VMEM((m, S5_W), F32), pltpu.VMEM((m, S5_W), F32),
                        pltpu.VMEM((ROWS, S5_W), F32), pltpu.VMEM((ROWS, S5_W), F32)],
        compiler_params=_cparams("arbitrary"),
        name="s5_mixer",
    )(z, z, p["ar"], p["ai"], p["wdr"], p["wdi"], p["wc"])


LRU_HALO = 2


def _lru_kernel(seg_chunk, xf_ref, xb_ref, xfp_ref, xfn_ref, xbn_ref, xbp_ref, cwf_ref, cwb_ref, cb_ref,
                wa_ref, wx_ref, ba_ref, bx_ref, sp_ref, hf_out, hb_out, a_ref, b_ref, hs_ref, h_ref):
    g = pl.program_id(0)

    @pl.when(g == 0)
    def _():
        h_ref[...] = jnp.zeros_like(h_ref)

    prev_ok, next_ok = _halo_valid(g, pl.num_programs(0), seg_chunk)
    x = _merge_dirs(xf_ref[...], xb_ref[...])
    tc = x.shape[0]
    ext = jnp.concatenate([_merge_dirs(xfp_ref[...], xbn_ref[...]) * prev_ok, x,
                           _merge_dirs(xfn_ref[...], xbp_ref[...]) * next_ok], axis=0)
    xcv = cb_ref[...]
    for j in range(LRU_CONV):
        xcv = xcv + ext[j:j + tc] * cwf_ref[j] + ext[LRU_HALO * 2 - j:LRU_HALO * 2 - j + tc] * cwb_ref[j]
    m = tc * ROWS
    x2 = xcv.reshape(m, BRANCH_W)
    bwd = _backward_rows((m, LANE))
    ga, gx = [], []
    for q in range(BRANCH_W // LANE):
        lhs = _by_direction(x2[:, q * LANE:(q + 1) * LANE], bwd).astype(BF16)
        ga.append(_dot(lhs, wa_ref[q]))
        gx.append(_dot(lhs, wx_ref[q]))
    gr = jax.nn.sigmoid(jnp.concatenate(ga, axis=-1).reshape(tc, ROWS, BRANCH_W) + ba_ref[...])
    gi = jax.nn.sigmoid(jnp.concatenate(gx, axis=-1).reshape(tc, ROWS, BRANCH_W) + bx_ref[...])
    log_a = -LRU_C * gr * sp_ref[...]
    a_ref[...] = jnp.exp(log_a)
    b_ref[...] = jnp.sqrt(-jnp.tanh(log_a) * (jnp.exp(2.0 * log_a) + 1.0)) * gi * xcv

    def step(s, h):
        h = a_ref[s] * h + b_ref[s]
        hs_ref[s] = h
        return h

    h_ref[...] = lax.fori_loop(0, tc, step, h_ref[...])
    hf_out[...], hb_out[...] = _split_dirs(hs_ref[...])


def _lru_mixer(z, p, scan):
    n = z.shape[0]
    w = BRANCH_W
    row_spec = pl.BlockSpec((ROWS, w), lambda g: (0, 0))
    chunk = pltpu.VMEM((scan.tc, ROWS, w), F32)
    return pl.pallas_call(
        functools.partial(_lru_kernel, scan.cs),
        grid=(scan.n,),
        in_specs=scan.chunk_specs(w) + scan.halo_specs(w)
        + [pl.BlockSpec((LRU_CONV, ROWS, w), lambda g: (0, 0, 0)),
           pl.BlockSpec((LRU_CONV, ROWS, w), lambda g: (0, 0, 0)),
           pl.BlockSpec((1, w), lambda g: (0, 0)),
           pl.BlockSpec((w // LANE, 2 * LANE, LANE), lambda g: (0, 0, 0)),
           pl.BlockSpec((w // LANE, 2 * LANE, LANE), lambda g: (0, 0, 0)),
           row_spec, row_spec, row_spec],
        out_specs=[pl.BlockSpec((scan.rows, w), lambda g: (scan.fwd(g), 0)),
                   pl.BlockSpec((scan.rows, w), lambda g: (scan.bwd(g), 0))],
        out_shape=[jax.ShapeDtypeStruct((n, w), F32)] * 2,
        scratch_shapes=[chunk, chunk, chunk, pltpu.VMEM((ROWS, w), F32)],
        compiler_params=_cparams("arbitrary"),
        name="lru_mixer",
    )(z, z, z, z, z, z, p["cwf"], p["cwb"], p["cb"], p["wa"], p["wx"], p["ba"], p["bx"], p["sp"])


def _branch_out_kernel(yf_ref, yb_ref, b0_ref, b1_ref, g_ref, lnw_ref, lnb_ref, blk_ref,
                       u_ref, sf_ref, sb_ref, d_ref, gw_ref, gb_ref,
                       hf_ref, hb_ref, gate_ref, ob_ref, os_ref, od_ref):
    blk = blk_ref[...]
    y = yf_ref[...] + yb_ref[...]
    mean = _head_sum(y, blk) * (1.0 / RWKV_HEAD)
    yc = y - mean
    var = _head_sum(yc * yc, blk) * (1.0 / RWKV_HEAD)
    yn = yc * lax.rsqrt(var + RWKV_GN_EPS) * lnw_ref[...] + lnb_ref[...]
    ob_ref[...] = ((yn + (b0_ref[...] + b1_ref[...])) * g_ref[...]).astype(BF16)

    t = _gelu(u_ref[...] * d_ref[...] + sf_ref[...] + sb_ref[...])
    os_ref[...] = (t * jax.nn.sigmoid(_dot(t.astype(BF16), gw_ref[...]) + gb_ref[...])).astype(BF16)

    od_ref[...] = ((hf_ref[...] + hb_ref[...]) * _gelu(gate_ref[...])).astype(BF16)


def _branch_out(rw, s5, lru, p, tm):
    n = rw[0].shape[0]
    w = BRANCH_W
    tok = pl.BlockSpec((tm, w), lambda i: (i, 0))
    vec = pl.BlockSpec((1, w), lambda i: (0, 0))
    mat = pl.BlockSpec((w, w), lambda i: (0, 0))
    return pl.pallas_call(
        _branch_out_kernel,
        grid=(n // tm,),
        in_specs=[tok] * 5 + [vec, vec, pl.BlockSpec((3 * w, w), lambda i: (0, 0))] + [tok] * 3 + [vec, mat, vec]
        + [tok, tok, pl.BlockSpec((tm, w), lambda i: (i, 1))],
        out_specs=[tok] * 3,
        out_shape=[jax.ShapeDtypeStruct((n, w), BF16)] * 3,
        compiler_params=_cparams("parallel"),
        name="branch_out",
    )(*rw, p["ln_w"], p["ln_b"], p["blk"], *s5, p["s5_d"], p["glu_w"], p["glu_b"], *lru)


def _rows_by_direction(p):
    return jnp.repeat(p.astype(F32), HALF, axis=0)


def _block_diag(blocks):
    n, a, b = blocks.shape
    eye = jnp.eye(n, dtype=blocks.dtype)
    return jnp.einsum('nab,nm->namb', blocks, eye).reshape(n * a, n * b)


def _by_slices(fn, per_dir, n_slices, per_slice):
    return jnp.stack([jnp.stack([fn(per_dir[dd, q * per_slice:(q + 1) * per_slice]) for q in range(n_slices)])
                      for dd in range(2)])


def _rope_tables(seq, nt):
    rows = seq // GRID_W
    row = jnp.repeat(jnp.arange(rows, dtype=F32), GRID_W)
    col = jnp.tile(jnp.arange(GRID_W, dtype=F32), rows)
    n_pair = MLA_ROPE // 4
    inv = ROPE_BASE ** (-jnp.arange(n_pair, dtype=F32) / n_pair)
    ang = jnp.concatenate([row[:, None] * inv, col[:, None] * inv], -1)
    cos = jnp.concatenate([jnp.cos(ang), jnp.ones((nt, MLA_ROPE // 2), F32)], axis=0)
    sin = jnp.concatenate([jnp.sin(ang), jnp.zeros((nt, MLA_ROPE // 2), F32)], axis=0)
    n = cos.shape[0]
    z32 = jnp.zeros((n, MLA_ROPE // 2), F32)
    z64 = jnp.zeros((n, MLA_ROPE), F32)
    one = jnp.ones((n, MLA_NOPE), F32)
    z128 = jnp.zeros((n, MLA_NOPE), F32)
    qc = MLA_SCALE * jnp.concatenate([one, cos, cos, z64], axis=1)
    qs1 = MLA_SCALE * jnp.concatenate([z128, -sin, z32, z64], axis=1)
    qs2 = MLA_SCALE * jnp.concatenate([z128, z32, sin, z64], axis=1)
    rep = lambda t: jnp.repeat(t, HALF, axis=0)
    kc = rep(jnp.concatenate([cos, cos, z64], axis=1))
    ks1 = rep(jnp.concatenate([-sin, z32, z64], axis=1))
    ks2 = rep(jnp.concatenate([z32, sin, z64], axis=1))
    return qc, qs1, qs2, kc, ks1, ks2


def _time_batch_to_batch_time(tm):
    nt = tm // HALF
    dst = jnp.arange(tm)
    src = (dst % nt) * HALF + dst // nt
    return (src[:, None] == jnp.arange(tm)[None, :]).astype(BF16)


def _s5_discretise(lam_re, lam_im, log_dt, b_re, b_im):
    lre = jnp.minimum(lam_re.astype(F32), -1e-4)
    lim = lam_im.astype(F32)
    dt = jnp.exp(log_dt.astype(F32))[..., None]
    mag = jnp.exp(lre * dt)
    ar, ai = mag * jnp.cos(lim * dt), mag * jnp.sin(lim * dt)
    den = lre * lre + lim * lim
    nr, ni = ar - 1.0, ai
    cr = (nr * lre + ni * lim) / den
    ci = (ni * lre - nr * lim) / den
    b_re, b_im = b_re.astype(F32), b_im.astype(F32)
    br = cr[..., None] * b_re - ci[..., None] * b_im
    bi = cr[..., None] * b_im + ci[..., None] * b_re
    return ar, ai, br, bi


def kernel(x, c, ctx, c_ctx, ada_w, ada_b, norm_w, ffn_w1, ffn_w3, ffn_w2, w_in, gate_b, mla_q_norm, mla_w_uq, mla_kv_norm, mla_w_ukv, rwkv_mu, rwkv_w0, rwkv_w2, rwkv_a0, rwkv_a2, rwkv_g2, rwkv_kk, rwkv_ka, rwkv_rk, rwkv_ln_w, rwkv_ln_b, s5_lam_re, s5_lam_im, s5_log_dt, s5_b_re, s5_b_im, s5_c_re, s5_c_im, s5_d, s5_glu_w, s5_glu_b, lru_conv_w, lru_conv_b, lru_wa, lru_ba, lru_wx, lru_bx, lru_lam, w_branch, w_out):
    batch, seq, d = x.shape
    ctx_len = ctx.shape[1]
    depth = ada_w.shape[0]
    assert batch == HALF and d == D_MODEL
    n_lat, n_ctx = batch * seq, batch * ctx_len
    w = BRANCH_W

    rows_big = _Rows(n_lat, n_ctx, 512)
    rows = _Rows(n_lat, n_ctx, 256)
    scan = _Scan(seq, ctx_len, 64)

    w1 = ffn_w1.astype(BF16)
    w3 = ffn_w3.astype(BF16)
    w2 = ffn_w2.astype(BF16)
    perm = jnp.concatenate([jnp.arange(0, MLA_ROPE, 2), jnp.arange(1, MLA_ROPE, 2)])
    kr0 = MLA_Q_LORA + MLA_KV_LORA
    w_packed = jnp.concatenate(
        [w_in[:, :, :kr0], w_in[:, :, kr0 + perm], jnp.zeros((depth, d, MLA_IN_PAD - MLA_IN), F32),
         w_in[:, :, O_RWKV:MIX_IN], jnp.zeros((depth, d, GATE_BLOCK0 * D_MODEL - MIX_PACKED), F32),
         w_in[:, :, MIX_IN:]], axis=-1).astype(BF16)
    w_br = w_branch.astype(BF16)
    w_o = w_out.astype(BF16)
    nw = norm_w.reshape(depth, 6, 1, d)
    gb = gate_b.reshape(depth, N_BRANCH, 1, d)

    hd = MLA_NOPE + MLA_ROPE
    qcols = jnp.concatenate([jnp.arange(MLA_NOPE), MLA_NOPE + perm])
    wq = mla_w_uq.reshape(depth, MLA_Q_LORA, MLA_HEADS, hd)[..., qcols]
    wq = jnp.concatenate([wq, jnp.zeros((depth, MLA_Q_LORA, MLA_HEADS, MLA_HEAD_PAD - hd), F32)], axis=-1)
    wq = wq.reshape(depth, MLA_Q_LORA, MLA_HEADS * MLA_HEAD_PAD).astype(BF16)
    wkv = mla_w_ukv.astype(BF16)
    qn = mla_q_norm.reshape(depth, 1, MLA_Q_LORA)
    kvn = mla_kv_norm.reshape(depth, 1, MLA_KV_LORA)
    tabs = _rope_tables(seq, rows.tm // HALF)
    perm_bt = _time_batch_to_batch_time(rows.tm)
    perm_tb = _time_batch_to_batch_time(rows_big.tm).T

    blk = _block_diag(jnp.ones((RWKV_HEADS, RWKV_HEAD, RWKV_HEAD), BF16))
    blk = jnp.concatenate([blk, blk, blk], axis=0)

    cc = jnp.concatenate([c, c_ctx[None], jnp.zeros((ROWS - batch - 1, d), F32)], axis=0)
    mods = _modulation(cc, ada_w, ada_b).reshape(depth, ROWS, N_MOD, d)
    m_lat = jnp.concatenate([mods[:, :batch], mods[:, :batch]], axis=1).transpose(0, 2, 1, 3)
    m_ctx = jnp.broadcast_to(mods[:, batch][:, :, None, :], (depth, N_MOD, ROWS, d))
    mods = jnp.stack([m_lat, m_ctx], axis=1)

    xs = jnp.concatenate([x.transpose(1, 0, 2).reshape(n_lat, d), ctx.transpose(1, 0, 2).reshape(n_ctx, d)], axis=0)

    for l in range(depth):
        xs = _ffn_half(xs, mods, nw, w1, w3, w2, rows_big, l, 0, 0, 0, 1)
        z_mla, z_rwkv, z_s5, z_lru = _in_projection(xs, mods, nw, w_packed, rows, l)

        q, k, v = _mla_project(z_mla, qn, kvn, wq, wkv, perm_bt, tabs, rows, l)
        ya = _mla_attention(q, k, v, seq, ctx_len, 256)

        mu = rwkv_mu[l]
        wl = jnp.concatenate([
            jnp.concatenate([jnp.concatenate([rwkv_w2[l, dd], jnp.zeros((RWKV_DECAY_LORA, w), F32)], axis=1),
                             jnp.concatenate([jnp.zeros((RWKV_A_LORA, w), F32), rwkv_a2[l, dd]], axis=1)], axis=0)
            for dd in range(2)], axis=0).astype(BF16)
        rp = {"mua": _rows_by_direction(mu), "mub": _rows_by_direction(mu[::-1]),
              "w0": _rows_by_direction(rwkv_w0[l]), "a0": _rows_by_direction(rwkv_a0[l]),
              "kk": rwkv_kk[l].reshape(1, w), "ka": rwkv_ka[l].reshape(1, w),
              "rk": rwkv_rk[l].reshape(1, w), "wl": wl, "g2": rwkv_g2[l].astype(BF16), "blk": blk}
        rw = _rwkv_mixer(z_rwkv, rp, scan)

        ar, ai, br, bi = _s5_discretise(s5_lam_re[l], s5_lam_im[l], s5_log_dt[l], s5_b_re[l], s5_b_im[l])
        gps = S5_GROUPS // S5_SLICES
        tdiag = lambda blocks: _block_diag(blocks.astype(F32).transpose(0, 2, 1))
        drive = lambda bm: (lambda bd: jnp.concatenate([bd[0], bd[1]], axis=1).astype(BF16))(
            _by_slices(tdiag, bm, S5_SLICES, gps))
        cre = _by_slices(tdiag, s5_c_re[l], S5_SLICES, gps)
        cim = _by_slices(tdiag, s5_c_im[l], S5_SLICES, gps)
        sp = {"ar": _rows_by_direction(ar.reshape(2, S5_W)), "ai": _rows_by_direction(ai.reshape(2, S5_W)),
              "wdr": drive(br), "wdi": drive(bi),
              "wc": jnp.concatenate([jnp.concatenate([cre[0], cre[1]], axis=2),
                                     jnp.concatenate([-cim[0], -cim[1]], axis=2)], axis=1).astype(BF16)}
        sf, sb = _s5_mixer(z_s5, sp, scan)

        diag = lambda blocks: _block_diag(blocks.astype(F32))
        lru_gate = lambda wm: (lambda bd: jnp.concatenate([bd[0], bd[1]], axis=1).astype(BF16))(
            _by_slices(diag, wm, w // LANE, LANE // LRU_BLOCK))
        fwd_rows = (jnp.arange(ROWS) < HALF).astype(F32)[None, :, None]
        cw = jnp.broadcast_to(lru_conv_w[l].astype(F32)[:, None, :], (LRU_CONV, ROWS, w))
        lp = {"cwf": cw * fwd_rows, "cwb": cw * (1.0 - fwd_rows), "cb": lru_conv_b[l].reshape(1, w),
              "wa": lru_gate(lru_wa[l]), "wx": lru_gate(lru_wx[l]),
              "ba": _rows_by_direction(lru_ba[l]), "bx": _rows_by_direction(lru_bx[l]),
              "sp": _rows_by_direction(jax.nn.softplus(-lru_lam[l].astype(F32)))}
        hf, hb = _lru_mixer(z_lru, lp, scan)

        bp = {"ln_w": rwkv_ln_w[l].reshape(1, w), "ln_b": rwkv_ln_b[l].reshape(1, w), "blk": blk,
              "s5_d": s5_d[l].reshape(1, w), "glu_w": s5_glu_w[l].astype(BF16), "glu_b": s5_glu_b[l].reshape(1, w)}
        yb_o, ys_o, yd_o = _branch_out(rw, (z_s5, sf, sb), (hf, hb, z_lru), bp, rows.tm)

        m = _gated_sum(xs, mods, nw, ya, yb_o, ys_o, yd_o, perm_tb, w_packed, gb, w_br, rows_big, l)
        xs = _out_projection(m, w_o, xs, mods, nw, rows_big, l)
        xs = _ffn_half(xs, mods, nw, w1, w3, w2, rows_big, l, 1, 2, 4, 5)

    return xs[:n_lat].reshape(seq, batch, d).transpose(1, 0, 2)
```

```python
import functools
import math

import jax
import jax.numpy as jnp
from jax import lax
from jax.experimental import pallas as pl
from jax.experimental.pallas import tpu as pltpu

F32 = jnp.float32
BF16 = jnp.bfloat16

D_MODEL = 2048
N_BRANCH = 4
BRANCH_W = 512
D_FF = 5632
FFN_RES = 0.5
N_MOD = 9
NORM_EPS = 1e-6
GRID_W = 64

MLA_HEADS = 4
MLA_NOPE = 128
MLA_ROPE = 64
MLA_V = 128
MLA_Q_LORA = 768
MLA_KV_LORA = 512
MLA_SCALE = (MLA_NOPE + MLA_ROPE) ** -0.5
ROPE_BASE = 10000.0
MLA_HEAD_PAD = 256
MLA_IN_PAD = 1408

RWKV_HEAD = 64
RWKV_HEADS = BRANCH_W // RWKV_HEAD
RWKV_DECAY_LORA = 64
RWKV_A_LORA = 64
RWKV_G_LORA = 128
RWKV_GN_EPS = 64e-5
RWKV_IN = 3 * BRANCH_W + RWKV_DECAY_LORA + RWKV_A_LORA + RWKV_G_LORA

S5_GROUP = 16
S5_GROUPS = BRANCH_W // S5_GROUP
S5_STATE = 64
S5_W = S5_GROUPS * S5_STATE

LRU_BLOCKS = 8
LRU_BLOCK = BRANCH_W // LRU_BLOCKS
LRU_CONV = 4
LRU_C = 8.0

MLA_IN = MLA_Q_LORA + MLA_KV_LORA + MLA_ROPE
O_RWKV = MLA_IN
O_S5 = O_RWKV + RWKV_IN
O_LRU = O_S5 + BRANCH_W
MIX_IN = O_LRU + 2 * BRANCH_W

ROWS = 8
HALF = ROWS // 2
LANE = 128
VMEM_LIMIT = 56 * 1024 * 1024


def _cparams(*sem):
    return pltpu.CompilerParams(dimension_semantics=sem, vmem_limit_bytes=VMEM_LIMIT)


def _rms(x, w):
    return x * lax.rsqrt(jnp.mean(x * x, axis=-1, keepdims=True) + NORM_EPS) * w


def _dot(a, b):
    return jnp.dot(a, b, preferred_element_type=F32)


def _gelu(x):
    return 0.5 * x * (1.0 + jnp.tanh(math.sqrt(2.0 / math.pi) * (x + 0.044715 * (x * x * x))))


def _rows8(x):
    return x.reshape(x.shape[0] // ROWS, ROWS, x.shape[1])


def _mod_in(x, g, mod_ref):
    return (_rows8(_rms(x, g)) * (1.0 + mod_ref[1]) + mod_ref[0]).reshape(x.shape)


def _mod_gate(y, g, mod_ref):
    return (_rows8(_rms(y, g)) * mod_ref[2]).reshape(y.shape)


def _mod_kernel(cc_ref, w_ref, b_ref, o_ref):
    cc = cc_ref[...]
    s = (cc * jax.nn.sigmoid(cc)).astype(BF16)
    o_ref[...] = _dot(s, w_ref[...].astype(BF16)) + b_ref[...]


def _modulation(cc, ada_w, ada_b):
    L, D, N = ada_w.shape
    tn = 1024
    return pl.pallas_call(
        _mod_kernel,
        grid=(L, N // tn),
        in_specs=[pl.BlockSpec((ROWS, D), lambda l, j: (0, 0)),
                  pl.BlockSpec((None, D, tn), lambda l, j: (l, 0, j)),
                  pl.BlockSpec((None, 1, tn), lambda l, j: (l, 0, j))],
        out_specs=pl.BlockSpec((None, ROWS, tn), lambda l, j: (l, 0, j)),
        out_shape=jax.ShapeDtypeStruct((L, ROWS, N), F32),
        compiler_params=_cparams("parallel", "parallel"),
        name="modulation",
    )(cc, ada_w, ada_b.reshape(L, 1, N))


def _cast_kernel(x_ref, o_ref):
    o_ref[...] = x_ref[...].astype(BF16)


def _cast_bf16(w, rows_per_block):
    x = w.reshape(-1, w.shape[-1])
    n, c = x.shape
    return pl.pallas_call(
        _cast_kernel,
        grid=(n // rows_per_block,),
        in_specs=[pl.BlockSpec((rows_per_block, c), lambda i: (i, 0))],
        out_specs=pl.BlockSpec((rows_per_block, c), lambda i: (i, 0)),
        out_shape=jax.ShapeDtypeStruct((n, c), BF16),
        compiler_params=_cparams("parallel"),
        name="cast_bf16",
    )(x).reshape(w.shape)


MIX_WIDTHS = (MLA_IN_PAD, RWKV_IN, BRANCH_W, 2 * BRANCH_W)
MIX_PACKED = sum(MIX_WIDTHS)
MLA_KR0 = MLA_Q_LORA + MLA_KV_LORA


def _pack_kernel(w_ref, p_ref, mix_ref, gate_ref):
    x = w_ref[...]
    mix_ref[:, :MLA_KR0] = x[:, :MLA_KR0].astype(BF16)
    mix_ref[:, MLA_KR0:MLA_IN] = _dot(x[:, MLA_KR0:MLA_IN].astype(BF16), p_ref[...]).astype(BF16)
    mix_ref[:, MLA_IN:MLA_IN_PAD] = jnp.zeros((x.shape[0], MLA_IN_PAD - MLA_IN), BF16)
    mix_ref[:, MLA_IN_PAD:] = x[:, MLA_IN:MIX_IN].astype(BF16)
    gate_ref[...] = x[:, MIX_IN:].astype(BF16)


def _pack_w_in(w_in, rope_perm):
    depth, d, n_in = w_in.shape
    tr = 256
    n_gate = n_in - MIX_IN
    return pl.pallas_call(
        _pack_kernel,
        grid=(depth, d // tr),
        in_specs=[pl.BlockSpec((None, tr, n_in), lambda l, i: (l, i, 0)),
                  pl.BlockSpec((MLA_ROPE, MLA_ROPE), lambda l, i: (0, 0))],
        out_specs=[pl.BlockSpec((None, tr, MIX_PACKED), lambda l, i: (l, i, 0)),
                   pl.BlockSpec((None, tr, n_gate), lambda l, i: (l, i, 0))],
        out_shape=[jax.ShapeDtypeStruct((depth, d, MIX_PACKED), BF16),
                   jax.ShapeDtypeStruct((depth, d, n_gate), BF16)],
        compiler_params=_cparams("parallel", "parallel"),
        name="pack_w_in",
    )(w_in, rope_perm)


class _Rows:
    def __init__(self, n_lat, n_ctx, tm):
        assert n_lat % tm == 0 and n_ctx % tm == 0
        self.tm = tm
        self.n_tiles = (n_lat + n_ctx) // tm
        self.lat_tiles = n_lat // tm

    def segment(self, i):
        return (i >= self.lat_tiles).astype(jnp.int32)

    def pos_tile(self, i):
        return jnp.minimum(i, self.lat_tiles)


def _mod_spec(rows, l, group, grid_rank):
    if grid_rank == 2:
        return pl.BlockSpec((None, None, 3, ROWS, D_MODEL), lambda i, j: (l, rows.segment(i), group, 0, 0))
    return pl.BlockSpec((None, None, 3, ROWS, D_MODEL), lambda i: (l, rows.segment(i), group, 0, 0))


def _nw_spec(l, idx, grid_rank):
    if grid_rank == 2:
        return pl.BlockSpec((None, None, 1, D_MODEL), lambda i, j: (l, idx, 0, 0))
    return pl.BlockSpec((None, None, 1, D_MODEL), lambda i: (l, idx, 0, 0))


def _ffn_kernel(x_ref, mod_ref, gpre_ref, gpost_ref, w1_ref, w3_ref, w2_ref, o_ref, h_ref, acc_ref):
    j = pl.program_id(1)

    @pl.when(j == 0)
    def _():
        h_ref[...] = _mod_in(x_ref[...], gpre_ref[...], mod_ref).astype(BF16)
        acc_ref[...] = jnp.zeros_like(acc_ref)

    h = h_ref[...]
    a = _dot(h, w1_ref[...])
    b = _dot(h, w3_ref[...])
    u = (a * jax.nn.sigmoid(a) * b).astype(BF16)
    acc_ref[...] += _dot(u, w2_ref[...])

    @pl.when(j == pl.num_programs(1) - 1)
    def _():
        o_ref[...] = x_ref[...] + FFN_RES * _mod_gate(acc_ref[...], gpost_ref[...], mod_ref)


def _ffn_half(x, mods, nw, w1, w3, w2, rows, l, hf, group, i_pre, i_post):
    n, d = x.shape
    tm, tf = rows.tm, 512
    return pl.pallas_call(
        _ffn_kernel,
        grid=(rows.n_tiles, D_FF // tf),
        in_specs=[pl.BlockSpec((tm, d), lambda i, j: (i, 0)),
                  _mod_spec(rows, l, group, 2),
                  _nw_spec(l, i_pre, 2), _nw_spec(l, i_post, 2),
                  pl.BlockSpec((None, None, d, tf), lambda i, j: (l, hf, 0, j)),
                  pl.BlockSpec((None, None, d, tf), lambda i, j: (l, hf, 0, j)),
                  pl.BlockSpec((None, None, tf, d), lambda i, j: (l, hf, j, 0))],
        out_specs=pl.BlockSpec((tm, d), lambda i, j: (i, 0)),
        out_shape=jax.ShapeDtypeStruct((n, d), F32),
        scratch_shapes=[pltpu.VMEM((tm, d), BF16), pltpu.VMEM((tm, d), F32)],
        compiler_params=_cparams("parallel", "arbitrary"),
        name="ffn_half",
    )(x, mods, nw, nw, w1, w3, w2)


def _inproj_kernel(x_ref, mod_ref, g_ref, w_ref, *o_refs):
    h = _mod_in(x_ref[...], g_ref[...], mod_ref).astype(BF16)
    lo = 0
    for o_ref, wd in zip(o_refs, MIX_WIDTHS):
        o_ref[...] = _dot(h, w_ref[:, lo:lo + wd])
        lo += wd


def _in_projection(x, mods, nw, w_packed, rows, l):
    n, d = x.shape
    tm = rows.tm
    return pl.pallas_call(
        _inproj_kernel,
        grid=(rows.n_tiles,),
        in_specs=[pl.BlockSpec((tm, d), lambda i: (i, 0)), _mod_spec(rows, l, 1, 1), _nw_spec(l, 2, 1),
                  pl.BlockSpec((None, d, MIX_PACKED), lambda i: (l, 0, 0))],
        out_specs=[pl.BlockSpec((tm, wd), lambda i: (i, 0)) for wd in MIX_WIDTHS],
        out_shape=[jax.ShapeDtypeStruct((n, wd), F32) for wd in MIX_WIDTHS],
        compiler_params=_cparams("parallel"),
        name="in_projection",
    )(x, mods, nw, w_packed)


def _gate_kernel(x_ref, mod_ref, g_ref, ya_ref, yb_ref, ys_ref, yd_ref, pt_ref, wg_ref, gb_ref, wb_ref,
                 o_ref, h_ref, acc_ref, y_ref):
    k = pl.program_id(1)
    tm = x_ref.shape[0]

    @pl.when(k == 0)
    def _():
        h_ref[...] = _mod_in(x_ref[...], g_ref[...], mod_ref).astype(BF16)
        acc_ref[...] = jnp.zeros_like(acc_ref)
        y_ref[...] = _dot(pt_ref[...], ya_ref[...].reshape(tm, BRANCH_W)).astype(BF16)

    for kk, ref in ((1, yb_ref), (2, ys_ref), (3, yd_ref)):
        @pl.when(k == kk)
        def _(ref=ref):
            y_ref[...] = ref[...]

    zg = _dot(h_ref[...], wg_ref[...]) + gb_ref[...]
    acc_ref[...] += jax.nn.sigmoid(zg) * _dot(y_ref[...], wb_ref[...])

    @pl.when(k == pl.num_programs(1) - 1)
    def _():
        o_ref[...] = acc_ref[...].astype(BF16)


def _gated_sum(x, mods, nw, ya, yb, ys, yd, perm_t, w_gate, gate_b, w_branch, rows, l):
    n, d = x.shape
    tm = rows.tm
    tok = pl.BlockSpec((tm, BRANCH_W), lambda i, k: (i, 0))
    return pl.pallas_call(
        _gate_kernel,
        grid=(rows.n_tiles, N_BRANCH),
        in_specs=[pl.BlockSpec((tm, d), lambda i, k: (i, 0)),
                  _mod_spec(rows, l, 1, 2), _nw_spec(l, 2, 2),
                  pl.BlockSpec((HALF, tm // HALF, BRANCH_W), lambda i, k: (0, i, 0)),
                  tok, tok, tok,
                  pl.BlockSpec((tm, tm), lambda i, k: (0, 0)),
                  pl.BlockSpec((None, d, d), lambda i, k: (l, 0, k)),
                  pl.BlockSpec((None, None, 1, d), lambda i, k: (l, k, 0, 0)),
                  pl.BlockSpec((None, None, BRANCH_W, d), lambda i, k: (l, k, 0, 0))],
        out_specs=pl.BlockSpec((tm, d), lambda i, k: (i, 0)),
        out_shape=jax.ShapeDtypeStruct((n, d), BF16),
        scratch_shapes=[pltpu.VMEM((tm, d), BF16), pltpu.VMEM((tm, d), F32), pltpu.VMEM((tm, BRANCH_W), BF16)],
        compiler_params=_cparams("parallel", "arbitrary"),
        name="gated_sum",
    )(x, mods, nw, ya, yb, ys, yd, perm_t, w_gate, gate_b, w_branch)


def _outproj_kernel(m_ref, w_ref, x_ref, mod_ref, g_ref, o_ref):
    m = _dot(m_ref[...], w_ref[...])
    o_ref[...] = x_ref[...] + _mod_gate(m, g_ref[...], mod_ref)


def _out_projection(m, w_out, x, mods, nw, rows, l):
    n, d = x.shape
    tm = rows.tm
    return pl.pallas_call(
        _outproj_kernel,
        grid=(rows.n_tiles,),
        in_specs=[pl.BlockSpec((tm, d), lambda i: (i, 0)),
                  pl.BlockSpec((None, d, d), lambda i: (l, 0, 0)),
                  pl.BlockSpec((tm, d), lambda i: (i, 0)),
                  _mod_spec(rows, l, 1, 1), _nw_spec(l, 3, 1)],
        out_specs=pl.BlockSpec((tm, d), lambda i: (i, 0)),
        out_shape=jax.ShapeDtypeStruct((n, d), F32),
        compiler_params=_cparams("parallel"),
        name="out_projection",
    )(m, w_out, x, mods, nw)


def _rope_lanes(x, c, s_lo, s_hi, width):
    return x * c + pltpu.roll(x, width - MLA_ROPE // 2, 1) * s_lo + pltpu.roll(x, MLA_ROPE // 2, 1) * s_hi


def _mla_proj_kernel(z_ref, qn_ref, kvn_ref, wq_ref, wkv_ref, perm_ref, qc_ref, qs1_ref, qs2_ref,
                     kc_ref, ks1_ref, ks2_ref, q_ref, k_ref, v_ref):
    z = z_ref[...]
    tm = z.shape[0]
    nt = tm // HALF
    pm = perm_ref[...]
    cq = _dot(pm, _rms(z[:, :MLA_Q_LORA], qn_ref[...]).astype(BF16)).astype(BF16)
    ckv = _dot(pm, _rms(z[:, MLA_Q_LORA:MLA_Q_LORA + MLA_KV_LORA], kvn_ref[...]).astype(BF16)).astype(BF16)
    kr = z[:, MLA_Q_LORA + MLA_KV_LORA:]
    krr = _rope_lanes(kr, kc_ref[...], ks1_ref[...], ks2_ref[...], LANE).astype(BF16)
    krr = _dot(pm, krr).astype(BF16)
    q = _dot(cq, wq_ref[...])
    kv = _dot(ckv, wkv_ref[...])
    tile4 = lambda t: jnp.concatenate([t] * HALF, axis=0)
    qc, qs1, qs2 = tile4(qc_ref[...]), tile4(qs1_ref[...]), tile4(qs2_ref[...])
    for h in range(MLA_HEADS):
        qh = _rope_lanes(q[:, h * MLA_HEAD_PAD:(h + 1) * MLA_HEAD_PAD], qc, qs1, qs2, MLA_HEAD_PAD).astype(BF16)
        kn = kv[:, h * 256:h * 256 + MLA_NOPE].astype(BF16)
        vh = kv[:, h * 256 + MLA_NOPE:(h + 1) * 256].astype(BF16)
        for b in range(HALF):
            rs = slice(b * nt, (b + 1) * nt)
            q_ref[b, :, h * MLA_HEAD_PAD:(h + 1) * MLA_HEAD_PAD] = qh[rs]
            k_ref[b, :, h * MLA_HEAD_PAD:h * MLA_HEAD_PAD + MLA_NOPE] = kn[rs]
            k_ref[b, :, h * MLA_HEAD_PAD + MLA_NOPE:(h + 1) * MLA_HEAD_PAD] = krr[rs]
            v_ref[b, :, h * MLA_V:(h + 1) * MLA_V] = vh[rs]


def _mla_project(z, qn, kvn, wq, wkv, perm, tabs, rows, l):
    n = z.shape[0]
    tm = rows.tm
    nt = tm // HALF
    hq = MLA_HEADS * MLA_HEAD_PAD
    tab_q = pl.BlockSpec((nt, MLA_HEAD_PAD), lambda i: (rows.pos_tile(i), 0))
    tab_k = pl.BlockSpec((tm, LANE), lambda i: (rows.pos_tile(i), 0))
    out = lambda wd: pl.BlockSpec((HALF, nt, wd), lambda i: (0, i, 0))
    return pl.pallas_call(
        _mla_proj_kernel,
        grid=(rows.n_tiles,),
        in_specs=[pl.BlockSpec((tm, MLA_IN_PAD), lambda i: (i, 0)),
                  pl.BlockSpec((None, 1, MLA_Q_LORA), lambda i: (l, 0, 0)),
                  pl.BlockSpec((None, 1, MLA_KV_LORA), lambda i: (l, 0, 0)),
                  pl.BlockSpec((None, MLA_Q_LORA, hq), lambda i: (l, 0, 0)),
                  pl.BlockSpec((None, MLA_KV_LORA, hq), lambda i: (l, 0, 0)),
                  pl.BlockSpec((tm, tm), lambda i: (0, 0)),
                  tab_q, tab_q, tab_q, tab_k, tab_k, tab_k],
        out_specs=[out(hq), out(hq), out(MLA_HEADS * MLA_V)],
        out_shape=[jax.ShapeDtypeStruct((HALF, n // HALF, hq), BF16),
                   jax.ShapeDtypeStruct((HALF, n // HALF, hq), BF16),
                   jax.ShapeDtypeStruct((HALF, n // HALF, MLA_HEADS * MLA_V), BF16)],
        compiler_params=_cparams("parallel"),
        name="mla_project",
    )(z, qn, kvn, wq, wkv, perm, *tabs)


def _scores(q, k):
    return lax.dot_general(q, k, (((1,), (1,)), ((), ())), preferred_element_type=F32)


def _attn_kernel(q_ref, k1_ref, v1_ref, k2_ref, v2_ref, o_ref):
    q = q_ref[...]
    s1 = _scores(q, k1_ref[...])
    s2 = _scores(q, k2_ref[...])
    m = jnp.maximum(jnp.max(s1, axis=-1, keepdims=True), jnp.max(s2, axis=-1, keepdims=True))
    p1 = jnp.exp(s1 - m)
    p2 = jnp.exp(s2 - m)
    den = jnp.sum(p1, axis=-1, keepdims=True) + jnp.sum(p2, axis=-1, keepdims=True)
    o = _dot(p1.astype(BF16), v1_ref[...]) + _dot(p2.astype(BF16), v2_ref[...])
    o_ref[...] = (o / den).astype(BF16)


def _attn_ctx_kernel(q_ref, k_ref, v_ref, prev_ref, o_ref):
    del prev_ref
    s = _scores(q_ref[...], k_ref[...])
    p = jnp.exp(s - jnp.max(s, axis=-1, keepdims=True))
    o = _dot(p.astype(BF16), v_ref[...])
    o_ref[...] = (o / jnp.sum(p, axis=-1, keepdims=True)).astype(BF16)


def _mla_attention(q, k, v, seq, ctx_len, tq):
    batch, s_all, _ = q.shape
    cb = seq // ctx_len
    qt = seq // tq
    lat = pl.pallas_call(
        _attn_kernel,
        grid=(batch, MLA_HEADS, qt),
        in_specs=[pl.BlockSpec((None, tq, MLA_HEAD_PAD), lambda b, h, i: (b, i, h)),
                  pl.BlockSpec((None, seq, MLA_HEAD_PAD), lambda b, h, i: (b, 0, h)),
                  pl.BlockSpec((None, seq, MLA_V), lambda b, h, i: (b, 0, h)),
                  pl.BlockSpec((None, ctx_len, MLA_HEAD_PAD), lambda b, h, i: (b, cb, h)),
                  pl.BlockSpec((None, ctx_len, MLA_V), lambda b, h, i: (b, cb, h))],
        out_specs=pl.BlockSpec((None, tq, MLA_V), lambda b, h, i: (b, i, h)),
        out_shape=jax.ShapeDtypeStruct((batch, s_all, MLA_HEADS * MLA_V), BF16),
        compiler_params=_cparams("parallel", "parallel", "parallel"),
        name="mla_attention",
    )(q, k, v, k, v)
    return pl.pallas_call(
        _attn_ctx_kernel,
        grid=(batch, MLA_HEADS),
        in_specs=[pl.BlockSpec((None, ctx_len, MLA_HEAD_PAD), lambda b, h: (b, cb, h)),
                  pl.BlockSpec((None, ctx_len, MLA_HEAD_PAD), lambda b, h: (b, cb, h)),
                  pl.BlockSpec((None, ctx_len, MLA_V), lambda b, h: (b, cb, h)),
                  pl.BlockSpec(memory_space=pl.ANY)],
        out_specs=pl.BlockSpec((None, ctx_len, MLA_V), lambda b, h: (b, cb, h)),
        out_shape=jax.ShapeDtypeStruct((batch, s_all, MLA_HEADS * MLA_V), BF16),
        input_output_aliases={3: 0},
        compiler_params=_cparams("parallel", "parallel"),
        name="mla_attention_ctx",
    )(q, k, v, lat)


class _Scan:
    def __init__(self, seq, ctx_len, tc):
        assert seq % tc == 0 and ctx_len % tc == 0 and tc % 2 == 0
        self.tc = tc
        self.rows = tc * HALF
        self.cs = ctx_len // tc
        self.lat = seq // tc
        self.n = self.cs + self.lat
        self.tiles = tc // 2
        self.n_tiles = self.n * self.tiles

    def fwd(self, g):
        return jnp.where(g < self.cs, self.lat + g, g - self.cs)

    def bwd(self, g):
        return jnp.where(g < self.cs, self.lat + self.cs - 1 - g, self.n - 1 - g)

    def chunk_specs(self, cols, col_block=0):
        return [pl.BlockSpec((self.rows, cols), lambda g: (self.fwd(g), col_block)),
                pl.BlockSpec((self.rows, cols), lambda g: (self.bwd(g), col_block))]

    def halo_specs(self, cols, col_block=0):
        t, last = self.tiles, self.n_tiles - 1
        return [pl.BlockSpec((ROWS, cols), lambda g: (jnp.maximum(self.fwd(g) * t - 1, 0), col_block)),
                pl.BlockSpec((ROWS, cols), lambda g: (jnp.minimum((self.fwd(g) + 1) * t, last), col_block)),
                pl.BlockSpec((ROWS, cols), lambda g: (jnp.minimum((self.bwd(g) + 1) * t, last), col_block)),
                pl.BlockSpec((ROWS, cols), lambda g: (jnp.maximum(self.bwd(g) * t - 1, 0), col_block))]


def _flip0(x):
    n = x.shape[0]
    if n == 1:
        return x
    return jnp.concatenate([x[n - 1 - k:n - k] for k in range(n)], axis=0)


def _low_rows(shape):
    return lax.broadcasted_iota(jnp.int32, shape, len(shape) - 2) % ROWS < HALF


def _merge_dirs(df, db):
    c = df.shape[-1]
    n2 = df.shape[0] // ROWS
    t = df.reshape(n2, ROWS, c)
    u = _flip0(db.reshape(n2, ROWS, c))
    lo = _low_rows(t.shape)
    even = jnp.where(lo, t, u)
    odd = pltpu.roll(jnp.where(lo, u, t), HALF, 1)
    return jnp.stack([even, odd], axis=1).reshape(2 * n2, ROWS, c)


def _split_dirs(y):
    n, _, c = y.shape
    y2 = y.reshape(n // 2, 2 * ROWS, c)
    a, b = y2[:, :ROWS], y2[:, ROWS:]
    rb = pltpu.roll(b, HALF, 1)
    lo = _low_rows(a.shape)
    df = jnp.where(lo, a, rb).reshape(n * HALF, c)
    db = _flip0(jnp.where(lo, rb, a)).reshape(n * HALF, c)
    return df, db


def _backward_rows(shape):
    return lax.broadcasted_iota(jnp.int32, shape, len(shape) - 2) % ROWS >= HALF


def _by_direction(x, bwd):
    zero = jnp.zeros_like(x)
    return jnp.concatenate([jnp.where(bwd, zero, x), jnp.where(bwd, x, zero)], axis=-1)


def _halo_valid(g, n_chunks, seg_chunk):
    prev_ok = jnp.logical_and(g != 0, g != seg_chunk)
    next_ok = jnp.logical_and(g != seg_chunk - 1, g != n_chunks - 1)
    return prev_ok.astype(F32), next_ok.astype(F32)


def _head_sum(x, blk3):
    hi = x.astype(BF16)
    r1 = x - hi.astype(F32)
    mid = r1.astype(BF16)
    lo = (r1 - mid.astype(F32)).astype(BF16)
    return _dot(jnp.concatenate([hi, mid, lo], axis=-1), blk3)


RWKV_ACC = 4
RWKV_VQ = 4
RWKV_VROWS = RWKV_HEAD // RWKV_VQ
RWKV_PAIRS = RWKV_HEADS // 2
RWKV_SEQ = RWKV_PAIRS * ROWS


def _to_scan_layout(x):
    a = jnp.concatenate([x[:, q * LANE:(q + 1) * LANE] for q in range(RWKV_PAIRS)], axis=0)
    return jnp.concatenate([a] * RWKV_VQ, axis=0).T


def _from_scan_layout(ys):
    t = jnp.concatenate([y for y in ys for _ in range(RWKV_VQ)], axis=0).T.reshape(RWKV_VQ, RWKV_SEQ, LANE)
    quarter = (lax.broadcasted_iota(jnp.int32, (RWKV_SEQ, LANE), 1) // RWKV_VROWS) % RWKV_VQ
    ya = t[RWKV_VQ - 1]
    for iq in range(RWKV_VQ - 2, -1, -1):
        ya = jnp.where(quarter == iq, t[iq], ya)
    return jnp.concatenate([ya[q * ROWS:(q + 1) * ROWS] for q in range(RWKV_PAIRS)], axis=1)


def _rwkv_kernel(seg_chunk, zf_ref, zb_ref, zfp_ref, zfn_ref, zbn_ref, zbp_ref, mua_ref, mub_ref, w0_ref, a0_ref,
                 kk_ref, ka_ref, rk_ref, wl_ref, g2_ref, blk_ref,
                 yf_out, yb_out, bf_out, bb_out, g_out, *scratch):
    nat_refs, bufs_a, bufs_b = scratch[0:6], scratch[6:12], scratch[12:18]
    y_ref, p_ref = scratch[18], scratch[19]
    g = pl.program_id(0)

    @pl.when(g == 0)
    def _():
        p_ref[...] = jnp.zeros_like(p_ref)

    prev_ok, next_ok = _halo_valid(g, pl.num_programs(0), seg_chunk)
    z = _merge_dirs(zf_ref[...], zb_ref[...])
    tc = z.shape[0]
    zprev = jnp.concatenate([_merge_dirs(zfp_ref[...], zbn_ref[...])[1:2] * prev_ok, z[:-1]], axis=0)
    znext = jnp.concatenate([z[1:], _merge_dirs(zfn_ref[...], zbp_ref[...])[0:1] * next_ok], axis=0)
    zs = z + (zprev - z) * mua_ref[...] + (znext - z) * mub_ref[...]
    w = BRANCH_W
    r, k, v = zs[..., :w], zs[..., w:2 * w], zs[..., 2 * w:3 * w]
    wa = zs[..., 3 * w:3 * w + LANE]
    gl = zs[..., 3 * w + LANE:]
    lane = lax.broadcasted_iota(jnp.int32, wa.shape, 2)
    t = jnp.where(lane < RWKV_DECAY_LORA, jnp.tanh(wa), wa)
    bwd = _backward_rows(t.shape)
    lhs = _by_direction(t, bwd).reshape(tc * ROWS, 2 * LANE).astype(BF16)
    lo = _dot(lhs, wl_ref[...]).reshape(tc, ROWS, 2 * w)
    w_pre = w0_ref[...] + lo[..., :w]
    decay = jnp.exp(-math.exp(-0.5) * jax.nn.sigmoid(w_pre))
    asig = jax.nn.sigmoid(a0_ref[...] + lo[..., w:])
    blk = blk_ref[...]
    kkv = k * kk_ref[...]
    ssq = _head_sum((kkv * kkv).reshape(tc * ROWS, w), blk).reshape(tc, ROWS, w)
    kkn = kkv * lax.rsqrt(ssq + 1e-12)
    kd = k * (1.0 + (asig - 1.0) * ka_ref[...])
    gg = _dot(jax.nn.sigmoid(gl).reshape(tc * ROWS, RWKV_G_LORA).astype(BF16), g2_ref[...]).reshape(tc, ROWS, w)
    rkd = _head_sum((r * kd * rk_ref[...]).reshape(tc * ROWS, w), blk).reshape(tc, ROWS, w)
    g_out[...] = _split_dirs(gg)[0]
    bf_out[...], bb_out[...] = _split_dirs(rkd * v)

    for ref, val in zip(nat_refs, (decay, kkn, kkn * asig, kd, r, v)):
        ref[...] = val

    def stage(s, bufs):
        for src, dst in zip(nat_refs, bufs):
            dst[...] = _to_scan_layout(src[s])

    def advance(s, bufs):
        tw_ref, tkk_ref, tkb_ref, tkd_ref, tr_ref, tv_ref = bufs
        copy = lax.broadcasted_iota(jnp.int32, (RWKV_VROWS, LANE), 1) // RWKV_SEQ
        ys = []
        for par in range(2):
            base = par * RWKV_HEAD
            vt = tv_ref[base + (RWKV_VQ - 1) * RWKV_VROWS:base + RWKV_VQ * RWKV_VROWS, :]
            for iq in range(RWKV_VQ - 2, -1, -1):
                vt = jnp.where(copy == iq, tv_ref[base + iq * RWKV_VROWS:base + (iq + 1) * RWKV_VROWS, :], vt)
            acc = [None] * RWKV_ACC
            for j in range(RWKV_HEAD):
                term = p_ref[base + j] * tkk_ref[base + j:base + j + 1, :]
                acc[j % RWKV_ACC] = term if acc[j % RWKV_ACC] is None else acc[j % RWKV_ACC] + term
            sa = -((acc[0] + acc[1]) + (acc[2] + acc[3]))
            yacc = [None] * RWKV_ACC
            for j in range(RWKV_HEAD):
                row = slice(base + j, base + j + 1)
                pn = p_ref[base + j] * tw_ref[row, :] + sa * tkb_ref[row, :] + vt * tkd_ref[row, :]
                p_ref[base + j] = pn
                term = pn * tr_ref[row, :]
                yacc[j % RWKV_ACC] = term if yacc[j % RWKV_ACC] is None else yacc[j % RWKV_ACC] + term
            ys.append((yacc[0] + yacc[1]) + (yacc[2] + yacc[3]))
        y_ref[s] = _from_scan_layout(ys)

    def two_steps(h, carry):
        s = 2 * h
        stage(s + 1, bufs_b)
        advance(s, bufs_a)
        stage(jnp.minimum(s + 2, tc - 1), bufs_a)
        advance(s + 1, bufs_b)
        return carry

    stage(0, bufs_a)
    lax.fori_loop(0, tc // 2, two_steps, 0)
    yf_out[...], yb_out[...] = _split_dirs(y_ref[...])


def _rwkv_mixer(z, p, scan):
    n = z.shape[0]
    w = BRANCH_W
    row_spec = lambda cols: pl.BlockSpec((ROWS, cols), lambda g: (0, 0))
    vec = pl.BlockSpec((1, w), lambda g: (0, 0))
    fout = pl.BlockSpec((scan.rows, w), lambda g: (scan.fwd(g), 0))
    bout = pl.BlockSpec((scan.rows, w), lambda g: (scan.bwd(g), 0))
    chunk = pltpu.VMEM((scan.tc, ROWS, w), F32)
    one_step = pltpu.VMEM((LANE, LANE), F32)
    return pl.pallas_call(
        functools.partial(_rwkv_kernel, scan.cs),
        grid=(scan.n,),
        in_specs=scan.chunk_specs(RWKV_IN) + scan.halo_specs(RWKV_IN)
        + [row_spec(RWKV_IN), row_spec(RWKV_IN), row_spec(w), row_spec(w), vec, vec, vec,
           pl.BlockSpec((2 * LANE, 2 * w), lambda g: (0, 0)),
           pl.BlockSpec((RWKV_G_LORA, w), lambda g: (0, 0)),
           pl.BlockSpec((3 * w, w), lambda g: (0, 0))],
        out_specs=[fout, bout, fout, bout, fout],
        out_shape=[jax.ShapeDtypeStruct((n, w), F32)] * 5,
        scratch_shapes=[chunk] * 6 + [one_step] * 12 + [chunk, pltpu.VMEM((2 * RWKV_HEAD, RWKV_VROWS, LANE), F32)],
        compiler_params=_cparams("arbitrary"),
        name="rwkv_mixer",
    )(z, z, z, z, z, z, p["mua"], p["mub"], p["w0"], p["a0"], p["kk"], p["ka"], p["rk"], p["wl"], p["g2"], p["blk"])


S5_SLICES = BRANCH_W // LANE
S5_SLICE_W = S5_W // S5_SLICES
S5_COLS = 512


def _s5_kernel(uf_ref, ub_ref, ar_ref, ai_ref, wdr_ref, wdi_ref, wc_ref, yf_out, yb_out,
               dr_ref, di_ref, hr_ref, hi_ref):
    @pl.when(pl.program_id(0) == 0)
    def _():
        hr_ref[...] = jnp.zeros_like(hr_ref)
        hi_ref[...] = jnp.zeros_like(hi_ref)

    u = _merge_dirs(uf_ref[...], ub_ref[...])
    tc = u.shape[0]
    m = tc * ROWS
    u2 = u.reshape(m, BRANCH_W)
    bwd = _backward_rows((m, LANE))
    for q in range(S5_SLICES):
        lhs = _by_direction(u2[:, q * LANE:(q + 1) * LANE], bwd).astype(BF16)
        dr_ref[:, q * S5_SLICE_W:(q + 1) * S5_SLICE_W] = _dot(lhs, wdr_ref[q])
        di_ref[:, q * S5_SLICE_W:(q + 1) * S5_SLICE_W] = _dot(lhs, wdi_ref[q])

    for cg in range(S5_W // S5_COLS):
        cols = slice(cg * S5_COLS, (cg + 1) * S5_COLS)
        ar = ar_ref[:, cols]
        ai = ai_ref[:, cols]

        def step(s, carry, cols=cols, ar=ar, ai=ai):
            hr, hi = carry
            rows = pl.ds(pl.multiple_of(s * ROWS, ROWS), ROWS)
            nr = ar * hr - ai * hi + dr_ref[rows, cols]
            ni = ar * hi + ai * hr + di_ref[rows, cols]
            dr_ref[rows, cols] = nr
            di_ref[rows, cols] = ni
            return nr, ni

        hr, hi = lax.fori_loop(0, tc, step, (hr_ref[:, cols], hi_ref[:, cols]))
        hr_ref[:, cols] = hr
        hi_ref[:, cols] = hi

    ys = []
    for q in range(S5_SLICES):
        hre = dr_ref[:, q * S5_SLICE_W:(q + 1) * S5_SLICE_W]
        him = di_ref[:, q * S5_SLICE_W:(q + 1) * S5_SLICE_W]
        both = _dot(jnp.concatenate([hre, him], axis=-1).astype(BF16), wc_ref[q])
        ys.append(jnp.where(bwd, both[:, LANE:], both[:, :LANE]))
    y = jnp.concatenate(ys, axis=-1).reshape(tc, ROWS, BRANCH_W)
    yf_out[...], yb_out[...] = _split_dirs(y)


def _s5_mixer(z, p, scan):
    n = z.shape[0]
    m = scan.tc * ROWS
    w = BRANCH_W
    return pl.pallas_call(
        _s5_kernel,
        grid=(scan.n,),
        in_specs=scan.chunk_specs(w)
        + [pl.BlockSpec((ROWS, S5_W), lambda g: (0, 0)),
           pl.BlockSpec((ROWS, S5_W), lambda g: (0, 0)),
           pl.BlockSpec((S5_SLICES, 2 * LANE, S5_SLICE_W), lambda g: (0, 0, 0)),
           pl.BlockSpec((S5_SLICES, 2 * LANE, S5_SLICE_W), lambda g: (0, 0, 0)),
           pl.BlockSpec((S5_SLICES, 2 * S5_SLICE_W, 2 * LANE), lambda g: (0, 0, 0))],
        out_specs=[pl.BlockSpec((scan.rows, w), lambda g: (scan.fwd(g), 0)),
                   pl.BlockSpec((scan.rows, w), lambda g: (scan.bwd(g), 0))],
        out_shape=[jax.ShapeDtypeStruct((n, w), F32)] * 2,
        scratch_shapes=[pltpu.VMEM((m, S5_W), F32), pltpu.VMEM((m, S5_W), F32),
                        pltpu.VMEM((ROWS, S5_W), F32), pltpu.VMEM((ROWS, S5_W), F32)],
        compiler_params=_cparams("arbitrary"),
        name="s5_mixer",
    )(z, z, p["ar"], p["ai"], p["wdr"], p["wdi"], p["wc"])


LRU_HALO = 2


def _lru_kernel(seg_chunk, xf_ref, xb_ref, xfp_ref, xfn_ref, xbn_ref, xbp_ref, cwf_ref, cwb_ref, cb_ref,
                wa_ref, wx_ref, ba_ref, bx_ref, sp_ref, hf_out, hb_out, a_ref, b_ref, hs_ref, h_ref):
    g = pl.program_id(0)

    @pl.when(g == 0)
    def _():
        h_ref[...] = jnp.zeros_like(h_ref)

    prev_ok, next_ok = _halo_valid(g, pl.num_programs(0), seg_chunk)
    x = _merge_dirs(xf_ref[...], xb_ref[...])
    tc = x.shape[0]
    ext = jnp.concatenate([_merge_dirs(xfp_ref[...], xbn_ref[...]) * prev_ok, x,
                           _merge_dirs(xfn_ref[...], xbp_ref[...]) * next_ok], axis=0)
    xcv = cb_ref[...]
    for j in range(LRU_CONV):
        xcv = xcv + ext[j:j + tc] * cwf_ref[j] + ext[LRU_HALO * 2 - j:LRU_HALO * 2 - j + tc] * cwb_ref[j]
    m = tc * ROWS
    x2 = xcv.reshape(m, BRANCH_W)
    bwd = _backward_rows((m, LANE))
    ga, gx = [], []
    for q in range(BRANCH_W // LANE):
        lhs = _by_direction(x2[:, q * LANE:(q + 1) * LANE], bwd).astype(BF16)
        ga.append(_dot(lhs, wa_ref[q]))
        gx.append(_dot(lhs, wx_ref[q]))
    gr = jax.nn.sigmoid(jnp.concatenate(ga, axis=-1).reshape(tc, ROWS, BRANCH_W) + ba_ref[...])
    gi = jax.nn.sigmoid(jnp.concatenate(gx, axis=-1).reshape(tc, ROWS, BRANCH_W) + bx_ref[...])
    log_a = -LRU_C * gr * sp_ref[...]
    a_ref[...] = jnp.exp(log_a)
    b_ref[...] = jnp.sqrt(-jnp.tanh(log_a) * (jnp.exp(2.0 * log_a) + 1.0)) * gi * xcv

    def step(s, h):
        h = a_ref[s] * h + b_ref[s]
        hs_ref[s] = h
        return h

    h_ref[...] = lax.fori_loop(0, tc, step, h_ref[...])
    hf_out[...], hb_out[...] = _split_dirs(hs_ref[...])


def _lru_mixer(z, p, scan):
    n = z.shape[0]
    w = BRANCH_W
    row_spec = pl.BlockSpec((ROWS, w), lambda g: (0, 0))
    chunk = pltpu.VMEM((scan.tc, ROWS, w), F32)
    return pl.pallas_call(
        functools.partial(_lru_kernel, scan.cs),
        grid=(scan.n,),
        in_specs=scan.chunk_specs(w) + scan.halo_specs(w)
        + [pl.BlockSpec((LRU_CONV, ROWS, w), lambda g: (0, 0, 0)),
           pl.BlockSpec((LRU_CONV, ROWS, w), lambda g: (0, 0, 0)),
           pl.BlockSpec((1, w), lambda g: (0, 0)),
           pl.BlockSpec((w // LANE, 2 * LANE, LANE), lambda g: (0, 0, 0)),
           pl.BlockSpec((w // LANE, 2 * LANE, LANE), lambda g: (0, 0, 0)),
           row_spec, row_spec, row_spec],
        out_specs=[pl.BlockSpec((scan.rows, w), lambda g: (scan.fwd(g), 0)),
                   pl.BlockSpec((scan.rows, w), lambda g: (scan.bwd(g), 0))],
        out_shape=[jax.ShapeDtypeStruct((n, w), F32)] * 2,
        scratch_shapes=[chunk, chunk, chunk, pltpu.VMEM((ROWS, w), F32)],
        compiler_params=_cparams("arbitrary"),
        name="lru_mixer",
    )(z, z, z, z, z, z, p["cwf"], p["cwb"], p["cb"], p["wa"], p["wx"], p["ba"], p["bx"], p["sp"])


def _branch_out_kernel(yf_ref, yb_ref, b0_ref, b1_ref, g_ref, lnw_ref, lnb_ref, blk_ref,
                       u_ref, sf_ref, sb_ref, d_ref, gw_ref, gb_ref,
                       hf_ref, hb_ref, gate_ref, ob_ref, os_ref, od_ref):
    blk = blk_ref[...]
    y = yf_ref[...] + yb_ref[...]
    mean = _head_sum(y, blk) * (1.0 / RWKV_HEAD)
    yc = y - mean
    var = _head_sum(yc * yc, blk) * (1.0 / RWKV_HEAD)
    yn = yc * lax.rsqrt(var + RWKV_GN_EPS) * lnw_ref[...] + lnb_ref[...]
    ob_ref[...] = ((yn + (b0_ref[...] + b1_ref[...])) * g_ref[...]).astype(BF16)

    t = _gelu(u_ref[...] * d_ref[...] + sf_ref[...] + sb_ref[...])
    os_ref[...] = (t * jax.nn.sigmoid(_dot(t.astype(BF16), gw_ref[...]) + gb_ref[...])).astype(BF16)

    od_ref[...] = ((hf_ref[...] + hb_ref[...]) * _gelu(gate_ref[...])).astype(BF16)


def _branch_out(rw, s5, lru, p, tm):
    n = rw[0].shape[0]
    w = BRANCH_W
    tok = pl.BlockSpec((tm, w), lambda i: (i, 0))
    vec = pl.BlockSpec((1, w), lambda i: (0, 0))
    mat = pl.BlockSpec((w, w), lambda i: (0, 0))
    return pl.pallas_call(
        _branch_out_kernel,
        grid=(n // tm,),
        in_specs=[tok] * 5 + [vec, vec, pl.BlockSpec((3 * w, w), lambda i: (0, 0))] + [tok] * 3 + [vec, mat, vec]
        + [tok, tok, pl.BlockSpec((tm, w), lambda i: (i, 1))],
        out_specs=[tok] * 3,
        out_shape=[jax.ShapeDtypeStruct((n, w), BF16)] * 3,
        compiler_params=_cparams("parallel"),
        name="branch_out",
    )(*rw, p["ln_w"], p["ln_b"], p["blk"], *s5, p["s5_d"], p["glu_w"], p["glu_b"], *lru)


def _rows_by_direction(p):
    return jnp.repeat(p.astype(F32), HALF, axis=0)


def _block_diag(blocks):
    n, a, b = blocks.shape
    eye = jnp.eye(n, dtype=blocks.dtype)
    return jnp.einsum('nab,nm->namb', blocks, eye).reshape(n * a, n * b)


def _by_slices(fn, per_dir, n_slices, per_slice):
    return jnp.stack([jnp.stack([fn(per_dir[dd, q * per_slice:(q + 1) * per_slice]) for q in range(n_slices)])
                      for dd in range(2)])


def _rope_tables(seq, nt):
    rows = seq // GRID_W
    row = jnp.repeat(jnp.arange(rows, dtype=F32), GRID_W)
    col = jnp.tile(jnp.arange(GRID_W, dtype=F32), rows)
    n_pair = MLA_ROPE // 4
    inv = ROPE_BASE ** (-jnp.arange(n_pair, dtype=F32) / n_pair)
    ang = jnp.concatenate([row[:, None] * inv, col[:, None] * inv], -1)
    cos = jnp.concatenate([jnp.cos(ang), jnp.ones((nt, MLA_ROPE // 2), F32)], axis=0)
    sin = jnp.concatenate([jnp.sin(ang), jnp.zeros((nt, MLA_ROPE // 2), F32)], axis=0)
    n = cos.shape[0]
    z32 = jnp.zeros((n, MLA_ROPE // 2), F32)
    z64 = jnp.zeros((n, MLA_ROPE), F32)
    one = jnp.ones((n, MLA_NOPE), F32)
    z128 = jnp.zeros((n, MLA_NOPE), F32)
    qc = MLA_SCALE * jnp.concatenate([one, cos, cos, z64], axis=1)
    qs1 = MLA_SCALE * jnp.concatenate([z128, -sin, z32, z64], axis=1)
    qs2 = MLA_SCALE * jnp.concatenate([z128, z32, sin, z64], axis=1)
    rep = lambda t: jnp.repeat(t, HALF, axis=0)
    kc = rep(jnp.concatenate([cos, cos, z64], axis=1))
    ks1 = rep(jnp.concatenate([-sin, z32, z64], axis=1))
    ks2 = rep(jnp.concatenate([z32, sin, z64], axis=1))
    return qc, qs1, qs2, kc, ks1, ks2


def _time_batch_to_batch_time(tm):
    nt = tm // HALF
    dst = jnp.arange(tm)
    src = (dst % nt) * HALF + dst // nt
    return (src[:, None] == jnp.arange(tm)[None, :]).astype(BF16)


def _s5_discretise(lam_re, lam_im, log_dt, b_re, b_im):
    lre = jnp.minimum(lam_re.astype(F32), -1e-4)
    lim = lam_im.astype(F32)
    dt = jnp.exp(log_dt.astype(F32))[..., None]
    mag = jnp.exp(lre * dt)
    ar, ai = mag * jnp.cos(lim * dt), mag * jnp.sin(lim * dt)
    den = lre * lre + lim * lim
    nr, ni = ar - 1.0, ai
    cr = (nr * lre + ni * lim) / den
    ci = (ni * lre - nr * lim) / den
    b_re, b_im = b_re.astype(F32), b_im.astype(F32)
    br = cr[..., None] * b_re - ci[..., None] * b_im
    bi = cr[..., None] * b_im + ci[..., None] * b_re
    return ar, ai, br, bi


def kernel(x, c, ctx, c_ctx, ada_w, ada_b, norm_w, ffn_w1, ffn_w3, ffn_w2, w_in, gate_b, mla_q_norm, mla_w_uq, mla_kv_norm, mla_w_ukv, rwkv_mu, rwkv_w0, rwkv_w2, rwkv_a0, rwkv_a2, rwkv_g2, rwkv_kk, rwkv_ka, rwkv_rk, rwkv_ln_w, rwkv_ln_b, s5_lam_re, s5_lam_im, s5_log_dt, s5_b_re, s5_b_im, s5_c_re, s5_c_im, s5_d, s5_glu_w, s5_glu_b, lru_conv_w, lru_conv_b, lru_wa, lru_ba, lru_wx, lru_bx, lru_lam, w_branch, w_out):
    batch, seq, d = x.shape
    ctx_len = ctx.shape[1]
    depth = ada_w.shape[0]
    assert batch == HALF and d == D_MODEL
    n_lat, n_ctx = batch * seq, batch * ctx_len
    w = BRANCH_W

    rows_big = _Rows(n_lat, n_ctx, 512)
    rows = _Rows(n_lat, n_ctx, 256)
    scan = _Scan(seq, ctx_len, 64)

    w1 = _cast_bf16(ffn_w1, 512)
    w3 = _cast_bf16(ffn_w3, 512)
    w2 = _cast_bf16(ffn_w2, 1024)
    perm = jnp.concatenate([jnp.arange(0, MLA_ROPE, 2), jnp.arange(1, MLA_ROPE, 2)])
    rope_perm = (jnp.arange(MLA_ROPE)[:, None] == perm[None, :]).astype(BF16)
    w_mix, w_gate = _pack_w_in(w_in, rope_perm)
    w_br = w_branch.astype(BF16)
    w_o = w_out.astype(BF16)
    nw = norm_w.reshape(depth, 6, 1, d)
    gb = gate_b.reshape(depth, N_BRANCH, 1, d)

    hd = MLA_NOPE + MLA_ROPE
    qcols = jnp.concatenate([jnp.arange(MLA_NOPE), MLA_NOPE + perm])
    wq = mla_w_uq.reshape(depth, MLA_Q_LORA, MLA_HEADS, hd)[..., qcols]
    wq = jnp.concatenate([wq, jnp.zeros((depth, MLA_Q_LORA, MLA_HEADS, MLA_HEAD_PAD - hd), F32)], axis=-1)
    wq = wq.reshape(depth, MLA_Q_LORA, MLA_HEADS * MLA_HEAD_PAD).astype(BF16)
    wkv = mla_w_ukv.astype(BF16)
    qn = mla_q_norm.reshape(depth, 1, MLA_Q_LORA)
    kvn = mla_kv_norm.reshape(depth, 1, MLA_KV_LORA)
    tabs = _rope_tables(seq, rows.tm // HALF)
    perm_bt = _time_batch_to_batch_time(rows.tm)
    perm_tb = _time_batch_to_batch_time(rows_big.tm).T

    blk = _block_diag(jnp.ones((RWKV_HEADS, RWKV_HEAD, RWKV_HEAD), BF16))
    blk = jnp.concatenate([blk, blk, blk], axis=0)

    cc = jnp.concatenate([c, c_ctx[None], jnp.zeros((ROWS - batch - 1, d), F32)], axis=0)
    mods = _modulation(cc, ada_w, ada_b).reshape(depth, ROWS, N_MOD, d)
    m_lat = jnp.concatenate([mods[:, :batch], mods[:, :batch]], axis=1).transpose(0, 2, 1, 3)
    m_ctx = jnp.broadcast_to(mods[:, batch][:, :, None, :], (depth, N_MOD, ROWS, d))
    mods = jnp.stack([m_lat, m_ctx], axis=1)

    xs = jnp.concatenate([x.transpose(1, 0, 2).reshape(n_lat, d), ctx.transpose(1, 0, 2).reshape(n_ctx, d)], axis=0)

    for l in range(depth):
        xs = _ffn_half(xs, mods, nw, w1, w3, w2, rows_big, l, 0, 0, 0, 1)
        z_mla, z_rwkv, z_s5, z_lru = _in_projection(xs, mods, nw, w_mix, rows, l)

        q, k, v = _mla_project(z_mla, qn, kvn, wq, wkv, perm_bt, tabs, rows, l)
        ya = _mla_attention(q, k, v, seq, ctx_len, 256)

        mu = rwkv_mu[l]
        wl = jnp.concatenate([
            jnp.concatenate([jnp.concatenate([rwkv_w2[l, dd], jnp.zeros((RWKV_DECAY_LORA, w), F32)], axis=1),
                             jnp.concatenate([jnp.zeros((RWKV_A_LORA, w), F32), rwkv_a2[l, dd]], axis=1)], axis=0)
            for dd in range(2)], axis=0).astype(BF16)
        rp = {"mua": _rows_by_direction(mu), "mub": _rows_by_direction(mu[::-1]),
              "w0": _rows_by_direction(rwkv_w0[l]), "a0": _rows_by_direction(rwkv_a0[l]),
              "kk": rwkv_kk[l].reshape(1, w), "ka": rwkv_ka[l].reshape(1, w),
              "rk": rwkv_rk[l].reshape(1, w), "wl": wl, "g2": rwkv_g2[l].astype(BF16), "blk": blk}
        rw = _rwkv_mixer(z_rwkv, rp, scan)

        ar, ai, br, bi = _s5_discretise(s5_lam_re[l], s5_lam_im[l], s5_log_dt[l], s5_b_re[l], s5_b_im[l])
        gps = S5_GROUPS // S5_SLICES
        tdiag = lambda blocks: _block_diag(blocks.astype(F32).transpose(0, 2, 1))
        drive = lambda bm: (lambda bd: jnp.concatenate([bd[0], bd[1]], axis=1).astype(BF16))(
            _by_slices(tdiag, bm, S5_SLICES, gps))
        cre = _by_slices(tdiag, s5_c_re[l], S5_SLICES, gps)
        cim = _by_slices(tdiag, s5_c_im[l], S5_SLICES, gps)
        sp = {"ar": _rows_by_direction(ar.reshape(2, S5_W)), "ai": _rows_by_direction(ai.reshape(2, S5_W)),
              "wdr": drive(br), "wdi": drive(bi),
              "wc": jnp.concatenate([jnp.concatenate([cre[0], cre[1]], axis=2),
                                     jnp.concatenate([-cim[0], -cim[1]], axis=2)], axis=1).astype(BF16)}
        sf, sb = _s5_mixer(z_s5, sp, scan)

        diag = lambda blocks: _block_diag(blocks.astype(F32))
        lru_gate = lambda wm: (lambda bd: jnp.concatenate([bd[0], bd[1]], axis=1).astype(BF16))(
            _by_slices(diag, wm, w // LANE, LANE // LRU_BLOCK))
        fwd_rows = (jnp.arange(ROWS) < HALF).astype(F32)[None, :, None]
        cw = jnp.broadcast_to(lru_conv_w[l].astype(F32)[:, None, :], (LRU_CONV, ROWS, w))
        lp = {"cwf": cw * fwd_rows, "cwb": cw * (1.0 - fwd_rows), "cb": lru_conv_b[l].reshape(1, w),
              "wa": lru_gate(lru_wa[l]), "wx": lru_gate(lru_wx[l]),
              "ba": _rows_by_direction(lru_ba[l]), "bx": _rows_by_direction(lru_bx[l]),
              "sp": _rows_by_direction(jax.nn.softplus(-lru_lam[l].astype(F32)))}
        hf, hb = _lru_mixer(z_lru, lp, scan)

        bp = {"ln_w": rwkv_ln_w[l].reshape(1, w), "ln_b": rwkv_ln_b[l].reshape(1, w), "blk": blk,
              "s5_d": s5_d[l].reshape(1, w), "glu_w": s5_glu_w[l].astype(BF16), "glu_b": s5_glu_b[l].reshape(1, w)}
        yb_o, ys_o, yd_o = _branch_out(rw, (z_s5, sf, sb), (hf, hb, z_lru), bp, rows.tm)

        m = _gated_sum(xs, mods, nw, ya, yb_o, ys_o, yd_o, perm_tb, w_gate, gb, w_br, rows_big, l)
        xs = _out_projection(m, w_o, xs, mods, nw, rows_big, l)
        xs = _ffn_half(xs, mods, nw, w1, w3, w2, rows_big, l, 1, 2, 4, 5)

    return xs[:n_lat].reshape(seq, batch, d).transpose(1, 0, 2)
```

```python
import functools
import math

import jax
import jax.numpy as jnp
from jax import lax
from jax.experimental import pallas as pl
from jax.experimental.pallas import tpu as pltpu

F32 = jnp.float32
BF16 = jnp.bfloat16

D_MODEL = 2048
N_BRANCH = 4
BRANCH_W = 512
D_FF = 5632
FFN_RES = 0.5
N_MOD = 9
NORM_EPS = 1e-6
GRID_W = 64

MLA_HEADS = 4
MLA_NOPE = 128
MLA_ROPE = 64
MLA_V = 128
MLA_Q_LORA = 768
MLA_KV_LORA = 512
MLA_SCALE = (MLA_NOPE + MLA_ROPE) ** -0.5
ROPE_BASE = 10000.0
MLA_HEAD_PAD = 256
MLA_IN_PAD = 1408

RWKV_HEAD = 64
RWKV_HEADS = BRANCH_W // RWKV_HEAD
RWKV_DECAY_LORA = 64
RWKV_A_LORA = 64
RWKV_G_LORA = 128
RWKV_GN_EPS = 64e-5
RWKV_IN = 3 * BRANCH_W + RWKV_DECAY_LORA + RWKV_A_LORA + RWKV_G_LORA

S5_GROUP = 16
S5_GROUPS = BRANCH_W // S5_GROUP
S5_STATE = 64
S5_W = S5_GROUPS * S5_STATE

LRU_BLOCKS = 8
LRU_BLOCK = BRANCH_W // LRU_BLOCKS
LRU_CONV = 4
LRU_C = 8.0

MLA_IN = MLA_Q_LORA + MLA_KV_LORA + MLA_ROPE
O_RWKV = MLA_IN
O_S5 = O_RWKV + RWKV_IN
O_LRU = O_S5 + BRANCH_W
MIX_IN = O_LRU + 2 * BRANCH_W

ROWS = 8
HALF = ROWS // 2
LANE = 128
VMEM_LIMIT = 56 * 1024 * 1024


def _cparams(*sem):
    return pltpu.CompilerParams(dimension_semantics=sem, vmem_limit_bytes=VMEM_LIMIT)


def _rms(x, w):
    return x * lax.rsqrt(jnp.mean(x * x, axis=-1, keepdims=True) + NORM_EPS) * w


def _dot(a, b):
    return jnp.dot(a, b, preferred_element_type=F32)


def _gelu(x):
    return 0.5 * x * (1.0 + jnp.tanh(math.sqrt(2.0 / math.pi) * (x + 0.044715 * (x * x * x))))


def _rows8(x):
    return x.reshape(x.shape[0] // ROWS, ROWS, x.shape[1])


def _mod_in(x, g, mod_ref):
    return (_rows8(_rms(x, g)) * (1.0 + mod_ref[1]) + mod_ref[0]).reshape(x.shape)


def _mod_gate(y, g, mod_ref):
    return (_rows8(_rms(y, g)) * mod_ref[2]).reshape(y.shape)


def _mod_kernel(cc_ref, w_ref, b_ref, o_ref):
    cc = cc_ref[...]
    s = (cc * jax.nn.sigmoid(cc)).astype(BF16)
    o_ref[...] = _dot(s, w_ref[...].astype(BF16)) + b_ref[...]


def _modulation(cc, ada_w, ada_b):
    L, D, N = ada_w.shape
    tn = 1024
    return pl.pallas_call(
        _mod_kernel,
        grid=(L, N // tn),
        in_specs=[pl.BlockSpec((ROWS, D), lambda l, j: (0, 0)),
                  pl.BlockSpec((None, D, tn), lambda l, j: (l, 0, j)),
                  pl.BlockSpec((None, 1, tn), lambda l, j: (l, 0, j))],
        out_specs=pl.BlockSpec((None, ROWS, tn), lambda l, j: (l, 0, j)),
        out_shape=jax.ShapeDtypeStruct((L, ROWS, N), F32),
        compiler_params=_cparams("parallel", "parallel"),
        name="modulation",
    )(cc, ada_w, ada_b.reshape(L, 1, N))


def _cast_kernel(x_ref, o_ref):
    o_ref[...] = x_ref[...].astype(BF16)


def _cast_bf16(w, rows_per_block):
    x = w.reshape(-1, w.shape[-1])
    n, c = x.shape
    return pl.pallas_call(
        _cast_kernel,
        grid=(n // rows_per_block,),
        in_specs=[pl.BlockSpec((rows_per_block, c), lambda i: (i, 0))],
        out_specs=pl.BlockSpec((rows_per_block, c), lambda i: (i, 0)),
        out_shape=jax.ShapeDtypeStruct((n, c), BF16),
        compiler_params=_cparams("parallel"),
        name="cast_bf16",
    )(x).reshape(w.shape)


MIX_WIDTHS = (MLA_IN_PAD, RWKV_IN, BRANCH_W, 2 * BRANCH_W)
MIX_PACKED = sum(MIX_WIDTHS)
MLA_KR0 = MLA_Q_LORA + MLA_KV_LORA


def _pack_kernel(w_ref, p_ref, mix_ref, gate_ref):
    x = w_ref[...]
    mix_ref[:, :MLA_KR0] = x[:, :MLA_KR0].astype(BF16)
    mix_ref[:, MLA_KR0:MLA_IN] = _dot(x[:, MLA_KR0:MLA_IN].astype(BF16), p_ref[...]).astype(BF16)
    mix_ref[:, MLA_IN:MLA_IN_PAD] = jnp.zeros((x.shape[0], MLA_IN_PAD - MLA_IN), BF16)
    mix_ref[:, MLA_IN_PAD:] = x[:, MLA_IN:MIX_IN].astype(BF16)
    gate_ref[...] = x[:, MIX_IN:].astype(BF16)


def _pack_w_in(w_in, rope_perm):
    depth, d, n_in = w_in.shape
    tr = 256
    n_gate = n_in - MIX_IN
    return pl.pallas_call(
        _pack_kernel,
        grid=(depth, d // tr),
        in_specs=[pl.BlockSpec((None, tr, n_in), lambda l, i: (l, i, 0)),
                  pl.BlockSpec((MLA_ROPE, MLA_ROPE), lambda l, i: (0, 0))],
        out_specs=[pl.BlockSpec((None, tr, MIX_PACKED), lambda l, i: (l, i, 0)),
                   pl.BlockSpec((None, tr, n_gate), lambda l, i: (l, i, 0))],
        out_shape=[jax.ShapeDtypeStruct((depth, d, MIX_PACKED), BF16),
                   jax.ShapeDtypeStruct((depth, d, n_gate), BF16)],
        compiler_params=_cparams("parallel", "parallel"),
        name="pack_w_in",
    )(w_in, rope_perm)


class _Rows:
    def __init__(self, n_lat, n_ctx, tm):
        assert n_lat % tm == 0 and n_ctx % tm == 0
        self.tm = tm
        self.n_tiles = (n_lat + n_ctx) // tm
        self.lat_tiles = n_lat // tm

    def segment(self, i):
        return (i >= self.lat_tiles).astype(jnp.int32)

    def pos_tile(self, i):
        return jnp.minimum(i, self.lat_tiles)


def _mod_spec(rows, l, group, grid_rank):
    if grid_rank == 2:
        return pl.BlockSpec((None, None, 3, ROWS, D_MODEL), lambda i, j: (l, rows.segment(i), group, 0, 0))
    return pl.BlockSpec((None, None, 3, ROWS, D_MODEL), lambda i: (l, rows.segment(i), group, 0, 0))


def _nw_spec(l, idx, grid_rank):
    if grid_rank == 2:
        return pl.BlockSpec((None, None, 1, D_MODEL), lambda i, j: (l, idx, 0, 0))
    return pl.BlockSpec((None, None, 1, D_MODEL), lambda i: (l, idx, 0, 0))


def _ffn_kernel(x_ref, mod_ref, gpre_ref, gpost_ref, w1_ref, w3_ref, w2_ref, o_ref, h_ref, acc_ref):
    j = pl.program_id(1)

    @pl.when(j == 0)
    def _():
        h_ref[...] = _mod_in(x_ref[...], gpre_ref[...], mod_ref).astype(BF16)
        acc_ref[...] = jnp.zeros_like(acc_ref)

    h = h_ref[...]
    a = _dot(h, w1_ref[...])
    b = _dot(h, w3_ref[...])
    u = (a * jax.nn.sigmoid(a) * b).astype(BF16)
    acc_ref[...] += _dot(u, w2_ref[...])

    @pl.when(j == pl.num_programs(1) - 1)
    def _():
        o_ref[...] = x_ref[...] + FFN_RES * _mod_gate(acc_ref[...], gpost_ref[...], mod_ref)


def _ffn_half(x, mods, nw, w1, w3, w2, rows, l, hf, group, i_pre, i_post):
    n, d = x.shape
    tm, tf = rows.tm, 512
    return pl.pallas_call(
        _ffn_kernel,
        grid=(rows.n_tiles, D_FF // tf),
        in_specs=[pl.BlockSpec((tm, d), lambda i, j: (i, 0)),
                  _mod_spec(rows, l, group, 2),
                  _nw_spec(l, i_pre, 2), _nw_spec(l, i_post, 2),
                  pl.BlockSpec((None, None, d, tf), lambda i, j: (l, hf, 0, j)),
                  pl.BlockSpec((None, None, d, tf), lambda i, j: (l, hf, 0, j)),
                  pl.BlockSpec((None, None, tf, d), lambda i, j: (l, hf, j, 0))],
        out_specs=pl.BlockSpec((tm, d), lambda i, j: (i, 0)),
        out_shape=jax.ShapeDtypeStruct((n, d), F32),
        scratch_shapes=[pltpu.VMEM((tm, d), BF16), pltpu.VMEM((tm, d), F32)],
        compiler_params=_cparams("parallel", "arbitrary"),
        name="ffn_half",
    )(x, mods, nw, nw, w1, w3, w2)


def _inproj_kernel(x_ref, mod_ref, g_ref, w_ref, *o_refs):
    h = _mod_in(x_ref[...], g_ref[...], mod_ref).astype(BF16)
    lo = 0
    for o_ref, wd in zip(o_refs, MIX_WIDTHS):
        o_ref[...] = _dot(h, w_ref[:, lo:lo + wd])
        lo += wd


def _in_projection(x, mods, nw, w_packed, rows, l):
    n, d = x.shape
    tm = rows.tm
    return pl.pallas_call(
        _inproj_kernel,
        grid=(rows.n_tiles,),
        in_specs=[pl.BlockSpec((tm, d), lambda i: (i, 0)), _mod_spec(rows, l, 1, 1), _nw_spec(l, 2, 1),
                  pl.BlockSpec((None, d, MIX_PACKED), lambda i: (l, 0, 0))],
        out_specs=[pl.BlockSpec((tm, wd), lambda i: (i, 0)) for wd in MIX_WIDTHS],
        out_shape=[jax.ShapeDtypeStruct((n, wd), F32) for wd in MIX_WIDTHS],
        compiler_params=_cparams("parallel"),
        name="in_projection",
    )(x, mods, nw, w_packed)


def _gate_kernel(x_ref, mod_ref, g_ref, ya_ref, yb_ref, ys_ref, yd_ref, pt_ref, wg_ref, gb_ref, wb_ref,
                 o_ref, h_ref, acc_ref, y_ref):
    k = pl.program_id(1)
    tm = x_ref.shape[0]

    @pl.when(k == 0)
    def _():
        h_ref[...] = _mod_in(x_ref[...], g_ref[...], mod_ref).astype(BF16)
        acc_ref[...] = jnp.zeros_like(acc_ref)
        y_ref[...] = _dot(pt_ref[...], ya_ref[...].reshape(tm, BRANCH_W)).astype(BF16)

    for kk, ref in ((1, yb_ref), (2, ys_ref), (3, yd_ref)):
        @pl.when(k == kk)
        def _(ref=ref):
            y_ref[...] = ref[...]

    zg = _dot(h_ref[...], wg_ref[...]) + gb_ref[...]
    acc_ref[...] += jax.nn.sigmoid(zg) * _dot(y_ref[...], wb_ref[...])

    @pl.when(k == pl.num_programs(1) - 1)
    def _():
        o_ref[...] = acc_ref[...].astype(BF16)


def _gated_sum(x, mods, nw, ya, yb, ys, yd, perm_t, w_gate, gate_b, w_branch, rows, l):
    n, d = x.shape
    tm = rows.tm
    tok = pl.BlockSpec((tm, BRANCH_W), lambda i, k: (i, 0))
    return pl.pallas_call(
        _gate_kernel,
        grid=(rows.n_tiles, N_BRANCH),
        in_specs=[pl.BlockSpec((tm, d), lambda i, k: (i, 0)),
                  _mod_spec(rows, l, 1, 2), _nw_spec(l, 2, 2),
                  pl.BlockSpec((HALF, tm // HALF, BRANCH_W), lambda i, k: (0, i, 0)),
                  tok, tok, tok,
                  pl.BlockSpec((tm, tm), lambda i, k: (0, 0)),
                  pl.BlockSpec((None, d, d), lambda i, k: (l, 0, k)),
                  pl.BlockSpec((None, None, 1, d), lambda i, k: (l, k, 0, 0)),
                  pl.BlockSpec((None, None, BRANCH_W, d), lambda i, k: (l, k, 0, 0))],
        out_specs=pl.BlockSpec((tm, d), lambda i, k: (i, 0)),
        out_shape=jax.ShapeDtypeStruct((n, d), BF16),
        scratch_shapes=[pltpu.VMEM((tm, d), BF16), pltpu.VMEM((tm, d), F32), pltpu.VMEM((tm, BRANCH_W), BF16)],
        compiler_params=_cparams("parallel", "arbitrary"),
        name="gated_sum",
    )(x, mods, nw, ya, yb, ys, yd, perm_t, w_gate, gate_b, w_branch)


def _outproj_kernel(m_ref, w_ref, x_ref, mod_ref, g_ref, o_ref):
    m = _dot(m_ref[...], w_ref[...])
    o_ref[...] = x_ref[...] + _mod_gate(m, g_ref[...], mod_ref)


def _out_projection(m, w_out, x, mods, nw, rows, l):
    n, d = x.shape
    tm = rows.tm
    return pl.pallas_call(
        _outproj_kernel,
        grid=(rows.n_tiles,),
        in_specs=[pl.BlockSpec((tm, d), lambda i: (i, 0)),
                  pl.BlockSpec((None, d, d), lambda i: (l, 0, 0)),
                  pl.BlockSpec((tm, d), lambda i: (i, 0)),
                  _mod_spec(rows, l, 1, 1), _nw_spec(l, 3, 1)],
        out_specs=pl.BlockSpec((tm, d), lambda i: (i, 0)),
        out_shape=jax.ShapeDtypeStruct((n, d), F32),
        compiler_params=_cparams("parallel"),
        name="out_projection",
    )(m, w_out, x, mods, nw)


def _rope_lanes(x, c, s_lo, s_hi, width):
    return x * c + pltpu.roll(x, width - MLA_ROPE // 2, 1) * s_lo + pltpu.roll(x, MLA_ROPE // 2, 1) * s_hi


def _mla_proj_kernel(z_ref, qn_ref, kvn_ref, wq_ref, wkv_ref, perm_ref, qc_ref, qs1_ref, qs2_ref,
                     kc_ref, ks1_ref, ks2_ref, q_ref, k_ref, v_ref):
    z = z_ref[...]
    tm = z.shape[0]
    nt = tm // HALF
    pm = perm_ref[...]
    cq = _dot(pm, _rms(z[:, :MLA_Q_LORA], qn_ref[...]).astype(BF16)).astype(BF16)
    ckv = _dot(pm, _rms(z[:, MLA_Q_LORA:MLA_Q_LORA + MLA_KV_LORA], kvn_ref[...]).astype(BF16)).astype(BF16)
    kr = z[:, MLA_Q_LORA + MLA_KV_LORA:]
    krr = _rope_lanes(kr, kc_ref[...], ks1_ref[...], ks2_ref[...], LANE).astype(BF16)
    krr = _dot(pm, krr).astype(BF16)
    q = _dot(cq, wq_ref[...])
    kv = _dot(ckv, wkv_ref[...])
    tile4 = lambda t: jnp.concatenate([t] * HALF, axis=0)
    qc, qs1, qs2 = tile4(qc_ref[...]), tile4(qs1_ref[...]), tile4(qs2_ref[...])
    for h in range(MLA_HEADS):
        qh = _rope_lanes(q[:, h * MLA_HEAD_PAD:(h + 1) * MLA_HEAD_PAD], qc, qs1, qs2, MLA_HEAD_PAD).astype(BF16)
        kn = kv[:, h * 256:h * 256 + MLA_NOPE].astype(BF16)
        vh = kv[:, h * 256 + MLA_NOPE:(h + 1) * 256].astype(BF16)
        for b in range(HALF):
            rs = slice(b * nt, (b + 1) * nt)
            q_ref[b, :, h * MLA_HEAD_PAD:(h + 1) * MLA_HEAD_PAD] = qh[rs]
            k_ref[b, :, h * MLA_HEAD_PAD:h * MLA_HEAD_PAD + MLA_NOPE] = kn[rs]
            k_ref[b, :, h * MLA_HEAD_PAD + MLA_NOPE:(h + 1) * MLA_HEAD_PAD] = krr[rs]
            v_ref[b, :, h * MLA_V:(h + 1) * MLA_V] = vh[rs]


def _mla_project(z, qn, kvn, wq, wkv, perm, tabs, rows, l):
    n = z.shape[0]
    tm = rows.tm
    nt = tm // HALF
    hq = MLA_HEADS * MLA_HEAD_PAD
    tab_q = pl.BlockSpec((nt, MLA_HEAD_PAD), lambda i: (rows.pos_tile(i), 0))
    tab_k = pl.BlockSpec((tm, LANE), lambda i: (rows.pos_tile(i), 0))
    out = lambda wd: pl.BlockSpec((HALF, nt, wd), lambda i: (0, i, 0))
    return pl.pallas_call(
        _mla_proj_kernel,
        grid=(rows.n_tiles,),
        in_specs=[pl.BlockSpec((tm, MLA_IN_PAD), lambda i: (i, 0)),
                  pl.BlockSpec((None, 1, MLA_Q_LORA), lambda i: (l, 0, 0)),
                  pl.BlockSpec((None, 1, MLA_KV_LORA), lambda i: (l, 0, 0)),
                  pl.BlockSpec((None, MLA_Q_LORA, hq), lambda i: (l, 0, 0)),
                  pl.BlockSpec((None, MLA_KV_LORA, hq), lambda i: (l, 0, 0)),
                  pl.BlockSpec((tm, tm), lambda i: (0, 0)),
                  tab_q, tab_q, tab_q, tab_k, tab_k, tab_k],
        out_specs=[out(hq), out(hq), out(MLA_HEADS * MLA_V)],
        out_shape=[jax.ShapeDtypeStruct((HALF, n // HALF, hq), BF16),
                   jax.ShapeDtypeStruct((HALF, n // HALF, hq), BF16),
                   jax.ShapeDtypeStruct((HALF, n // HALF, MLA_HEADS * MLA_V), BF16)],
        compiler_params=_cparams("parallel"),
        name="mla_project",
    )(z, qn, kvn, wq, wkv, perm, *tabs)


def _scores(q, k):
    return lax.dot_general(q, k, (((1,), (1,)), ((), ())), preferred_element_type=F32)


def _attn_kernel(q_ref, k1_ref, v1_ref, k2_ref, v2_ref, o_ref):
    q = q_ref[...]
    s1 = _scores(q, k1_ref[...])
    s2 = _scores(q, k2_ref[...])
    m = jnp.maximum(jnp.max(s1, axis=-1, keepdims=True), jnp.max(s2, axis=-1, keepdims=True))
    p1 = jnp.exp(s1 - m)
    p2 = jnp.exp(s2 - m)
    den = jnp.sum(p1, axis=-1, keepdims=True) + jnp.sum(p2, axis=-1, keepdims=True)
    o = _dot(p1.astype(BF16), v1_ref[...]) + _dot(p2.astype(BF16), v2_ref[...])
    o_ref[...] = (o / den).astype(BF16)


def _attn_ctx_kernel(q_ref, k_ref, v_ref, prev_ref, o_ref):
    del prev_ref
    s = _scores(q_ref[...], k_ref[...])
    p = jnp.exp(s - jnp.max(s, axis=-1, keepdims=True))
    o = _dot(p.astype(BF16), v_ref[...])
    o_ref[...] = (o / jnp.sum(p, axis=-1, keepdims=True)).astype(BF16)


def _mla_attention(q, k, v, seq, ctx_len, tq):
    batch, s_all, _ = q.shape
    cb = seq // ctx_len
    qt = seq // tq
    lat = pl.pallas_call(
        _attn_kernel,
        grid=(batch, MLA_HEADS, qt),
        in_specs=[pl.BlockSpec((None, tq, MLA_HEAD_PAD), lambda b, h, i: (b, i, h)),
                  pl.BlockSpec((None, seq, MLA_HEAD_PAD), lambda b, h, i: (b, 0, h)),
                  pl.BlockSpec((None, seq, MLA_V), lambda b, h, i: (b, 0, h)),
                  pl.BlockSpec((None, ctx_len, MLA_HEAD_PAD), lambda b, h, i: (b, cb, h)),
                  pl.BlockSpec((None, ctx_len, MLA_V), lambda b, h, i: (b, cb, h))],
        out_specs=pl.BlockSpec((None, tq, MLA_V), lambda b, h, i: (b, i, h)),
        out_shape=jax.ShapeDtypeStruct((batch, s_all, MLA_HEADS * MLA_V), BF16),
        compiler_params=_cparams("parallel", "parallel", "parallel"),
        name="mla_attention",
    )(q, k, v, k, v)
    return pl.pallas_call(
        _attn_ctx_kernel,
        grid=(batch, MLA_HEADS),
        in_specs=[pl.BlockSpec((None, ctx_len, MLA_HEAD_PAD), lambda b, h: (b, cb, h)),
                  pl.BlockSpec((None, ctx_len, MLA_HEAD_PAD), lambda b, h: (b, cb, h)),
                  pl.BlockSpec((None, ctx_len, MLA_V), lambda b, h: (b, cb, h)),
                  pl.BlockSpec(memory_space=pl.ANY)],
        out_specs=pl.BlockSpec((None, ctx_len, MLA_V), lambda b, h: (b, cb, h)),
        out_shape=jax.ShapeDtypeStruct((batch, s_all, MLA_HEADS * MLA_V), BF16),
        input_output_aliases={3: 0},
        compiler_params=_cparams("parallel", "parallel"),
        name="mla_attention_ctx",
    )(q, k, v, lat)


class _Scan:
    def __init__(self, seq, ctx_len, tc):
        assert seq % tc == 0 and ctx_len % tc == 0 and tc % 2 == 0
        self.tc = tc
        self.rows = tc * HALF
        self.cs = ctx_len // tc
        self.lat = seq // tc
        self.n = self.cs + self.lat
        self.tiles = tc // 2
        self.n_tiles = self.n * self.tiles

    def fwd(self, g):
        return jnp.where(g < self.cs, self.lat + g, g - self.cs)

    def bwd(self, g):
        return jnp.where(g < self.cs, self.lat + self.cs - 1 - g, self.n - 1 - g)

    def chunk_specs(self, cols, col_block=0):
        return [pl.BlockSpec((self.rows, cols), lambda g: (self.fwd(g), col_block)),
                pl.BlockSpec((self.rows, cols), lambda g: (self.bwd(g), col_block))]

    def halo_specs(self, cols, col_block=0):
        t, last = self.tiles, self.n_tiles - 1
        return [pl.BlockSpec((ROWS, cols), lambda g: (jnp.maximum(self.fwd(g) * t - 1, 0), col_block)),
                pl.BlockSpec((ROWS, cols), lambda g: (jnp.minimum((self.fwd(g) + 1) * t, last), col_block)),
                pl.BlockSpec((ROWS, cols), lambda g: (jnp.minimum((self.bwd(g) + 1) * t, last), col_block)),
                pl.BlockSpec((ROWS, cols), lambda g: (jnp.maximum(self.bwd(g) * t - 1, 0), col_block))]


def _flip0(x):
    n = x.shape[0]
    if n == 1:
        return x
    return jnp.concatenate([x[n - 1 - k:n - k] for k in range(n)], axis=0)


def _low_rows(shape):
    return lax.broadcasted_iota(jnp.int32, shape, len(shape) - 2) % ROWS < HALF


def _merge_dirs(df, db):
    c = df.shape[-1]
    n2 = df.shape[0] // ROWS
    t = df.reshape(n2, ROWS, c)
    u = _flip0(db.reshape(n2, ROWS, c))
    lo = _low_rows(t.shape)
    even = jnp.where(lo, t, u)
    odd = pltpu.roll(jnp.where(lo, u, t), HALF, 1)
    return jnp.stack([even, odd], axis=1).reshape(2 * n2, ROWS, c)


def _split_dirs(y):
    n, _, c = y.shape
    y2 = y.reshape(n // 2, 2 * ROWS, c)
    a, b = y2[:, :ROWS], y2[:, ROWS:]
    rb = pltpu.roll(b, HALF, 1)
    lo = _low_rows(a.shape)
    df = jnp.where(lo, a, rb).reshape(n * HALF, c)
    db = _flip0(jnp.where(lo, rb, a)).reshape(n * HALF, c)
    return df, db


def _backward_rows(shape):
    return lax.broadcasted_iota(jnp.int32, shape, len(shape) - 2) % ROWS >= HALF


def _by_direction(x, bwd):
    zero = jnp.zeros_like(x)
    return jnp.concatenate([jnp.where(bwd, zero, x), jnp.where(bwd, x, zero)], axis=-1)


def _halo_valid(g, n_chunks, seg_chunk):
    prev_ok = jnp.logical_and(g != 0, g != seg_chunk)
    next_ok = jnp.logical_and(g != seg_chunk - 1, g != n_chunks - 1)
    return prev_ok.astype(F32), next_ok.astype(F32)


def _head_sum(x, blk3):
    hi = x.astype(BF16)
    r1 = x - hi.astype(F32)
    mid = r1.astype(BF16)
    lo = (r1 - mid.astype(F32)).astype(BF16)
    return _dot(jnp.concatenate([hi, mid, lo], axis=-1), blk3)


RWKV_ACC = 4
RWKV_VQ = 4
RWKV_VROWS = RWKV_HEAD // RWKV_VQ
RWKV_PAIRS = RWKV_HEADS // 2
RWKV_SEQ = RWKV_PAIRS * ROWS


def _to_scan_layout(x):
    a = jnp.concatenate([x[:, q * LANE:(q + 1) * LANE] for q in range(RWKV_PAIRS)], axis=0)
    return jnp.concatenate([a] * RWKV_VQ, axis=0).T


def _from_scan_layout(ys):
    t = jnp.concatenate([y for y in ys for _ in range(RWKV_VQ)], axis=0).T.reshape(RWKV_VQ, RWKV_SEQ, LANE)
    quarter = (lax.broadcasted_iota(jnp.int32, (RWKV_SEQ, LANE), 1) // RWKV_VROWS) % RWKV_VQ
    ya = t[RWKV_VQ - 1]
    for iq in range(RWKV_VQ - 2, -1, -1):
        ya = jnp.where(quarter == iq, t[iq], ya)
    return jnp.concatenate([ya[q * ROWS:(q + 1) * ROWS] for q in range(RWKV_PAIRS)], axis=1)


def _rwkv_kernel(seg_chunk, zf_ref, zb_ref, zfp_ref, zfn_ref, zbn_ref, zbp_ref, mua_ref, mub_ref, w0_ref, a0_ref,
                 kk_ref, ka_ref, rk_ref, wl_ref, g2_ref, blk_ref,
                 yf_out, yb_out, bf_out, bb_out, g_out, *scratch):
    nat_refs, bufs_a, bufs_b = scratch[0:5], scratch[5:10], scratch[10:15]
    wend_ref, y_ref, p_ref = scratch[15], scratch[16], scratch[17]
    g = pl.program_id(0)

    @pl.when(g == 0)
    def _():
        p_ref[...] = jnp.zeros_like(p_ref)

    prev_ok, next_ok = _halo_valid(g, pl.num_programs(0), seg_chunk)
    z = _merge_dirs(zf_ref[...], zb_ref[...])
    tc = z.shape[0]
    zprev = jnp.concatenate([_merge_dirs(zfp_ref[...], zbn_ref[...])[1:2] * prev_ok, z[:-1]], axis=0)
    znext = jnp.concatenate([z[1:], _merge_dirs(zfn_ref[...], zbp_ref[...])[0:1] * next_ok], axis=0)
    zs = z + (zprev - z) * mua_ref[...] + (znext - z) * mub_ref[...]
    w = BRANCH_W
    r, k, v = zs[..., :w], zs[..., w:2 * w], zs[..., 2 * w:3 * w]
    wa = zs[..., 3 * w:3 * w + LANE]
    gl = zs[..., 3 * w + LANE:]
    lane = lax.broadcasted_iota(jnp.int32, wa.shape, 2)
    t = jnp.where(lane < RWKV_DECAY_LORA, jnp.tanh(wa), wa)
    bwd = _backward_rows(t.shape)
    lhs = _by_direction(t, bwd).reshape(tc * ROWS, 2 * LANE).astype(BF16)
    lo = _dot(lhs, wl_ref[...]).reshape(tc, ROWS, 2 * w)
    w_pre = w0_ref[...] + lo[..., :w]
    decay = jnp.exp(-math.exp(-0.5) * jax.nn.sigmoid(w_pre))
    asig = jax.nn.sigmoid(a0_ref[...] + lo[..., w:])
    blk = blk_ref[...]
    kkv = k * kk_ref[...]
    ssq = _head_sum((kkv * kkv).reshape(tc * ROWS, w), blk).reshape(tc, ROWS, w)
    kkn = kkv * lax.rsqrt(ssq + 1e-12)
    kd = k * (1.0 + (asig - 1.0) * ka_ref[...])
    gg = _dot(jax.nn.sigmoid(gl).reshape(tc * ROWS, RWKV_G_LORA).astype(BF16), g2_ref[...]).reshape(tc, ROWS, w)
    rkd = _head_sum((r * kd * rk_ref[...]).reshape(tc * ROWS, w), blk).reshape(tc, ROWS, w)
    g_out[...] = _split_dirs(gg)[0]
    bf_out[...], bb_out[...] = _split_dirs(rkd * v)

    cum = [decay[0]]
    for s in range(1, tc):
        cum.append(cum[-1] * decay[s])
    wc = jnp.stack(cum, axis=0)
    w_before = jnp.concatenate([jnp.ones_like(wc[:1]), wc[:-1]], axis=0)
    inv = 1.0 / wc
    for ref, val in zip(nat_refs, (kkn * w_before, kkn * asig * inv, kd * inv, r * wc, v)):
        ref[...] = val
    wend_ref[...] = _to_scan_layout(cum[-1])

    def stage(s, bufs):
        for src, dst in zip(nat_refs, bufs):
            dst[...] = _to_scan_layout(src[s])

    def advance(s, bufs):
        tkk_ref, tkb_ref, tkd_ref, tr_ref, tv_ref = bufs
        copy = lax.broadcasted_iota(jnp.int32, (RWKV_VROWS, LANE), 1) // RWKV_SEQ
        ys = []
        for par in range(2):
            base = par * RWKV_HEAD
            vt = tv_ref[base + (RWKV_VQ - 1) * RWKV_VROWS:base + RWKV_VQ * RWKV_VROWS, :]
            for iq in range(RWKV_VQ - 2, -1, -1):
                vt = jnp.where(copy == iq, tv_ref[base + iq * RWKV_VROWS:base + (iq + 1) * RWKV_VROWS, :], vt)
            acc = [None] * RWKV_ACC
            for j in range(RWKV_HEAD):
                term = p_ref[base + j] * tkk_ref[base + j:base + j + 1, :]
                acc[j % RWKV_ACC] = term if acc[j % RWKV_ACC] is None else acc[j % RWKV_ACC] + term
            sa = -((acc[0] + acc[1]) + (acc[2] + acc[3]))
            yacc = [None] * RWKV_ACC
            for j in range(RWKV_HEAD):
                row = slice(base + j, base + j + 1)
                pn = p_ref[base + j] + sa * tkb_ref[row, :] + vt * tkd_ref[row, :]
                p_ref[base + j] = pn
                term = pn * tr_ref[row, :]
                yacc[j % RWKV_ACC] = term if yacc[j % RWKV_ACC] is None else yacc[j % RWKV_ACC] + term
            ys.append((yacc[0] + yacc[1]) + (yacc[2] + yacc[3]))
        y_ref[s] = _from_scan_layout(ys)

    def two_steps(h, carry):
        s = 2 * h
        stage(s + 1, bufs_b)
        advance(s, bufs_a)
        stage(jnp.minimum(s + 2, tc - 1), bufs_a)
        advance(s + 1, bufs_b)
        return carry

    stage(0, bufs_a)
    lax.fori_loop(0, tc // 2, two_steps, 0)
    for row in range(2 * RWKV_HEAD):
        p_ref[row] = p_ref[row] * wend_ref[row:row + 1, :]
    yf_out[...], yb_out[...] = _split_dirs(y_ref[...])


def _rwkv_mixer(z, p, scan):
    n = z.shape[0]
    w = BRANCH_W
    row_spec = lambda cols: pl.BlockSpec((ROWS, cols), lambda g: (0, 0))
    vec = pl.BlockSpec((1, w), lambda g: (0, 0))
    fout = pl.BlockSpec((scan.rows, w), lambda g: (scan.fwd(g), 0))
    bout = pl.BlockSpec((scan.rows, w), lambda g: (scan.bwd(g), 0))
    chunk = pltpu.VMEM((scan.tc, ROWS, w), F32)
    one_step = pltpu.VMEM((LANE, LANE), F32)
    return pl.pallas_call(
        functools.partial(_rwkv_kernel, scan.cs),
        grid=(scan.n,),
        in_specs=scan.chunk_specs(RWKV_IN) + scan.halo_specs(RWKV_IN)
        + [row_spec(RWKV_IN), row_spec(RWKV_IN), row_spec(w), row_spec(w), vec, vec, vec,
           pl.BlockSpec((2 * LANE, 2 * w), lambda g: (0, 0)),
           pl.BlockSpec((RWKV_G_LORA, w), lambda g: (0, 0)),
           pl.BlockSpec((3 * w, w), lambda g: (0, 0))],
        out_specs=[fout, bout, fout, bout, fout],
        out_shape=[jax.ShapeDtypeStruct((n, w), F32)] * 5,
        scratch_shapes=[chunk] * 5 + [one_step] * 11 + [chunk, pltpu.VMEM((2 * RWKV_HEAD, RWKV_VROWS, LANE), F32)],
        compiler_params=_cparams("arbitrary"),
        name="rwkv_mixer",
    )(z, z, z, z, z, z, p["mua"], p["mub"], p["w0"], p["a0"], p["kk"], p["ka"], p["rk"], p["wl"], p["g2"], p["blk"])


S5_SLICES = BRANCH_W // LANE
S5_SLICE_W = S5_W // S5_SLICES
S5_COLS = 512


def _s5_kernel(uf_ref, ub_ref, ar_ref, ai_ref, wdr_ref, wdi_ref, wc_ref, yf_out, yb_out,
               dr_ref, di_ref, hr_ref, hi_ref):
    @pl.when(pl.program_id(0) == 0)
    def _():
        hr_ref[...] = jnp.zeros_like(hr_ref)
        hi_ref[...] = jnp.zeros_like(hi_ref)

    u = _merge_dirs(uf_ref[...], ub_ref[...])
    tc = u.shape[0]
    m = tc * ROWS
    u2 = u.reshape(m, BRANCH_W)
    bwd = _backward_rows((m, LANE))
    for q in range(S5_SLICES):
        lhs = _by_direction(u2[:, q * LANE:(q + 1) * LANE], bwd).astype(BF16)
        dr_ref[:, q * S5_SLICE_W:(q + 1) * S5_SLICE_W] = _dot(lhs, wdr_ref[q])
        di_ref[:, q * S5_SLICE_W:(q + 1) * S5_SLICE_W] = _dot(lhs, wdi_ref[q])

    for cg in range(S5_W // S5_COLS):
        cols = slice(cg * S5_COLS, (cg + 1) * S5_COLS)
        ar = ar_ref[:, cols]
        ai = ai_ref[:, cols]

        def step(s, carry, cols=cols, ar=ar, ai=ai):
            hr, hi = carry
            rows = pl.ds(pl.multiple_of(s * ROWS, ROWS), ROWS)
            nr = ar * hr - ai * hi + dr_ref[rows, cols]
            ni = ar * hi + ai * hr + di_ref[rows, cols]
            dr_ref[rows, cols] = nr
            di_ref[rows, cols] = ni
            return nr, ni

        hr, hi = lax.fori_loop(0, tc, step, (hr_ref[:, cols], hi_ref[:, cols]))
        hr_ref[:, cols] = hr
        hi_ref[:, cols] = hi

    ys = []
    for q in range(S5_SLICES):
        hre = dr_ref[:, q * S5_SLICE_W:(q + 1) * S5_SLICE_W]
        him = di_ref[:, q * S5_SLICE_W:(q + 1) * S5_SLICE_W]
        both = _dot(jnp.concatenate([hre, him], axis=-1).astype(BF16), wc_ref[q])
        ys.append(jnp.where(bwd, both[:, LANE:], both[:, :LANE]))
    y = jnp.concatenate(ys, axis=-1).reshape(tc, ROWS, BRANCH_W)
    yf_out[...], yb_out[...] = _split_dirs(y)


def _s5_mixer(z, p, scan):
    n = z.shape[0]
    m = scan.tc * ROWS
    w = BRANCH_W
    return pl.pallas_call(
        _s5_kernel,
        grid=(scan.n,),
        in_specs=scan.chunk_specs(w)
        + [pl.BlockSpec((ROWS, S5_W), lambda g: (0, 0)),
           pl.BlockSpec((ROWS, S5_W), lambda g: (0, 0)),
           pl.BlockSpec((S5_SLICES, 2 * LANE, S5_SLICE_W), lambda g: (0, 0, 0)),
           pl.BlockSpec((S5_SLICES, 2 * LANE, S5_SLICE_W), lambda g: (0, 0, 0)),
           pl.BlockSpec((S5_SLICES, 2 * S5_SLICE_W, 2 * LANE), lambda g: (0, 0, 0))],
        out_specs=[pl.BlockSpec((scan.rows, w), lambda g: (scan.fwd(g), 0)),
                   pl.BlockSpec((scan.rows, w), lambda g: (scan.bwd(g), 0))],
        out_shape=[jax.ShapeDtypeStruct((n, w), F32)] * 2,
        scratch_shapes=[pltpu.VMEM((m, S5_W), F32), pltpu.VMEM((m, S5_W), F32),
                        pltpu.VMEM((ROWS, S5_W), F32), pltpu.VMEM((ROWS, S5_W), F32)],
        compiler_params=_cparams("arbitrary"),
        name="s5_mixer",
    )(z, z, p["ar"], p["ai"], p["wdr"], p["wdi"], p["wc"])


LRU_HALO = 2


def _lru_kernel(seg_chunk, xf_ref, xb_ref, xfp_ref, xfn_ref, xbn_ref, xbp_ref, cwf_ref, cwb_ref, cb_ref,
                wa_ref, wx_ref, ba_ref, bx_ref, sp_ref, hf_out, hb_out, a_ref, b_ref, hs_ref, h_ref):
    g = pl.program_id(0)

    @pl.when(g == 0)
    def _():
        h_ref[...] = jnp.zeros_like(h_ref)

    prev_ok, next_ok = _halo_valid(g, pl.num_programs(0), seg_chunk)
    x = _merge_dirs(xf_ref[...], xb_ref[...])
    tc = x.shape[0]
    ext = jnp.concatenate([_merge_dirs(xfp_ref[...], xbn_ref[...]) * prev_ok, x,
                           _merge_dirs(xfn_ref[...], xbp_ref[...]) * next_ok], axis=0)
    xcv = cb_ref[...]
    for j in range(LRU_CONV):
        xcv = xcv + ext[j:j + tc] * cwf_ref[j] + ext[LRU_HALO * 2 - j:LRU_HALO * 2 - j + tc] * cwb_ref[j]
    m = tc * ROWS
    x2 = xcv.reshape(m, BRANCH_W)
    bwd = _backward_rows((m, LANE))
    ga, gx = [], []
    for q in range(BRANCH_W // LANE):
        lhs = _by_direction(x2[:, q * LANE:(q + 1) * LANE], bwd).astype(BF16)
        ga.append(_dot(lhs, wa_ref[q]))
        gx.append(_dot(lhs, wx_ref[q]))
    gr = jax.nn.sigmoid(jnp.concatenate(ga, axis=-1).reshape(tc, ROWS, BRANCH_W) + ba_ref[...])
    gi = jax.nn.sigmoid(jnp.concatenate(gx, axis=-1).reshape(tc, ROWS, BRANCH_W) + bx_ref[...])
    log_a = -LRU_C * gr * sp_ref[...]
    a_ref[...] = jnp.exp(log_a)
    b_ref[...] = jnp.sqrt(-jnp.tanh(log_a) * (jnp.exp(2.0 * log_a) + 1.0)) * gi * xcv

    def step(s, h):
        h = a_ref[s] * h + b_ref[s]
        hs_ref[s] = h
        return h

    h_ref[...] = lax.fori_loop(0, tc, step, h_ref[...])
    hf_out[...], hb_out[...] = _split_dirs(hs_ref[...])


def _lru_mixer(z, p, scan):
    n = z.shape[0]
    w = BRANCH_W
    row_spec = pl.BlockSpec((ROWS, w), lambda g: (0, 0))
    chunk = pltpu.VMEM((scan.tc, ROWS, w), F32)
    return pl.pallas_call(
        functools.partial(_lru_kernel, scan.cs),
        grid=(scan.n,),
        in_specs=scan.chunk_specs(w) + scan.halo_specs(w)
        + [pl.BlockSpec((LRU_CONV, ROWS, w), lambda g: (0, 0, 0)),
           pl.BlockSpec((LRU_CONV, ROWS, w), lambda g: (0, 0, 0)),
           pl.BlockSpec((1, w), lambda g: (0, 0)),
           pl.BlockSpec((w // LANE, 2 * LANE, LANE), lambda g: (0, 0, 0)),
           pl.BlockSpec((w // LANE, 2 * LANE, LANE), lambda g: (0, 0, 0)),
           row_spec, row_spec, row_spec],
        out_specs=[pl.BlockSpec((scan.rows, w), lambda g: (scan.fwd(g), 0)),
                   pl.BlockSpec((scan.rows, w), lambda g: (scan.bwd(g), 0))],
        out_shape=[jax.ShapeDtypeStruct((n, w), F32)] * 2,
        scratch_shapes=[chunk, chunk, chunk, pltpu.VMEM((ROWS, w), F32)],
        compiler_params=_cparams("arbitrary"),
        name="lru_mixer",
    )(z, z, z, z, z, z, p["cwf"], p["cwb"], p["cb"], p["wa"], p["wx"], p["ba"], p["bx"], p["sp"])


def _branch_out_kernel(yf_ref, yb_ref, b0_ref, b1_ref, g_ref, lnw_ref, lnb_ref, blk_ref,
                       u_ref, sf_ref, sb_ref, d_ref, gw_ref, gb_ref,
                       hf_ref, hb_ref, gate_ref, ob_ref, os_ref, od_ref):
    blk = blk_ref[...]
    y = yf_ref[...] + yb_ref[...]
    mean = _head_sum(y, blk) * (1.0 / RWKV_HEAD)
    yc = y - mean
    var = _head_sum(yc * yc, blk) * (1.0 / RWKV_HEAD)
    yn = yc * lax.rsqrt(var + RWKV_GN_EPS) * lnw_ref[...] + lnb_ref[...]
    ob_ref[...] = ((yn + (b0_ref[...] + b1_ref[...])) * g_ref[...]).astype(BF16)

    t = _gelu(u_ref[...] * d_ref[...] + sf_ref[...] + sb_ref[...])
    os_ref[...] = (t * jax.nn.sigmoid(_dot(t.astype(BF16), gw_ref[...]) + gb_ref[...])).astype(BF16)

    od_ref[...] = ((hf_ref[...] + hb_ref[...]) * _gelu(gate_ref[...])).astype(BF16)


def _branch_out(rw, s5, lru, p, tm):
    n = rw[0].shape[0]
    w = BRANCH_W
    tok = pl.BlockSpec((tm, w), lambda i: (i, 0))
    vec = pl.BlockSpec((1, w), lambda i: (0, 0))
    mat = pl.BlockSpec((w, w), lambda i: (0, 0))
    return pl.pallas_call(
        _branch_out_kernel,
        grid=(n // tm,),
        in_specs=[tok] * 5 + [vec, vec, pl.BlockSpec((3 * w, w), lambda i: (0, 0))] + [tok] * 3 + [vec, mat, vec]
        + [tok, tok, pl.BlockSpec((tm, w), lambda i: (i, 1))],
        out_specs=[tok] * 3,
        out_shape=[jax.ShapeDtypeStruct((n, w), BF16)] * 3,
        compiler_params=_cparams("parallel"),
        name="branch_out",
    )(*rw, p["ln_w"], p["ln_b"], p["blk"], *s5, p["s5_d"], p["glu_w"], p["glu_b"], *lru)


def _rows_by_direction(p):
    return jnp.repeat(p.astype(F32), HALF, axis=0)


def _block_diag(blocks):
    n, a, b = blocks.shape
    eye = jnp.eye(n, dtype=blocks.dtype)
    return jnp.einsum('nab,nm->namb', blocks, eye).reshape(n * a, n * b)


def _by_slices(fn, per_dir, n_slices, per_slice):
    return jnp.stack([jnp.stack([fn(per_dir[dd, q * per_slice:(q + 1) * per_slice]) for q in range(n_slices)])
                      for dd in range(2)])


def _rope_tables(seq, nt):
    rows = seq // GRID_W
    row = jnp.repeat(jnp.arange(rows, dtype=F32), GRID_W)
    col = jnp.tile(jnp.arange(GRID_W, dtype=F32), rows)
    n_pair = MLA_ROPE // 4
    inv = ROPE_BASE ** (-jnp.arange(n_pair, dtype=F32) / n_pair)
    ang = jnp.concatenate([row[:, None] * inv, col[:, None] * inv], -1)
    cos = jnp.concatenate([jnp.cos(ang), jnp.ones((nt, MLA_ROPE // 2), F32)], axis=0)
    sin = jnp.concatenate([jnp.sin(ang), jnp.zeros((nt, MLA_ROPE // 2), F32)], axis=0)
    n = cos.shape[0]
    z32 = jnp.zeros((n, MLA_ROPE // 2), F32)
    z64 = jnp.zeros((n, MLA_ROPE), F32)
    one = jnp.ones((n, MLA_NOPE), F32)
    z128 = jnp.zeros((n, MLA_NOPE), F32)
    qc = MLA_SCALE * jnp.concatenate([one, cos, cos, z64], axis=1)
    qs1 = MLA_SCALE * jnp.concatenate([z128, -sin, z32, z64], axis=1)
    qs2 = MLA_SCALE * jnp.concatenate([z128, z32, sin, z64], axis=1)
    rep = lambda t: jnp.repeat(t, HALF, axis=0)
    kc = rep(jnp.concatenate([cos, cos, z64], axis=1))
    ks1 = rep(jnp.concatenate([-sin, z32, z64], axis=1))
    ks2 = rep(jnp.concatenate([z32, sin, z64], axis=1))
    return qc, qs1, qs2, kc, ks1, ks2


def _time_batch_to_batch_time(tm):
    nt = tm // HALF
    dst = jnp.arange(tm)
    src = (dst % nt) * HALF + dst // nt
    return (src[:, None] == jnp.arange(tm)[None, :]).astype(BF16)


def _s5_discretise(lam_re, lam_im, log_dt, b_re, b_im):
    lre = jnp.minimum(lam_re.astype(F32), -1e-4)
    lim = lam_im.astype(F32)
    dt = jnp.exp(log_dt.astype(F32))[..., None]
    mag = jnp.exp(lre * dt)
    ar, ai = mag * jnp.cos(lim * dt), mag * jnp.sin(lim * dt)
    den = lre * lre + lim * lim
    nr, ni = ar - 1.0, ai
    cr = (nr * lre + ni * lim) / den
    ci = (ni * lre - nr * lim) / den
    b_re, b_im = b_re.astype(F32), b_im.astype(F32)
    br = cr[..., None] * b_re - ci[..., None] * b_im
    bi = cr[..., None] * b_im + ci[..., None] * b_re
    return ar, ai, br, bi


def kernel(x, c, ctx, c_ctx, ada_w, ada_b, norm_w, ffn_w1, ffn_w3, ffn_w2, w_in, gate_b, mla_q_norm, mla_w_uq, mla_kv_norm, mla_w_ukv, rwkv_mu, rwkv_w0, rwkv_w2, rwkv_a0, rwkv_a2, rwkv_g2, rwkv_kk, rwkv_ka, rwkv_rk, rwkv_ln_w, rwkv_ln_b, s5_lam_re, s5_lam_im, s5_log_dt, s5_b_re, s5_b_im, s5_c_re, s5_c_im, s5_d, s5_glu_w, s5_glu_b, lru_conv_w, lru_conv_b, lru_wa, lru_ba, lru_wx, lru_bx, lru_lam, w_branch, w_out):
    batch, seq, d = x.shape
    ctx_len = ctx.shape[1]
    depth = ada_w.shape[0]
    assert batch == HALF and d == D_MODEL
    n_lat, n_ctx = batch * seq, batch * ctx_len
    w = BRANCH_W

    rows_big = _Rows(n_lat, n_ctx, 512)
    rows = _Rows(n_lat, n_ctx, 256)
    scan = _Scan(seq, ctx_len, 64)

    w1 = _cast_bf16(ffn_w1, 512)
    w3 = _cast_bf16(ffn_w3, 512)
    w2 = _cast_bf16(ffn_w2, 1024)
    perm = jnp.concatenate([jnp.arange(0, MLA_ROPE, 2), jnp.arange(1, MLA_ROPE, 2)])
    rope_perm = (jnp.arange(MLA_ROPE)[:, None] == perm[None, :]).astype(BF16)
    w_mix, w_gate = _pack_w_in(w_in, rope_perm)
    w_br = w_branch.astype(BF16)
    w_o = w_out.astype(BF16)
    nw = norm_w.reshape(depth, 6, 1, d)
    gb = gate_b.reshape(depth, N_BRANCH, 1, d)

    hd = MLA_NOPE + MLA_ROPE
    qcols = jnp.concatenate([jnp.arange(MLA_NOPE), MLA_NOPE + perm])
    wq = mla_w_uq.reshape(depth, MLA_Q_LORA, MLA_HEADS, hd)[..., qcols]
    wq = jnp.concatenate([wq, jnp.zeros((depth, MLA_Q_LORA, MLA_HEADS, MLA_HEAD_PAD - hd), F32)], axis=-1)
    wq = wq.reshape(depth, MLA_Q_LORA, MLA_HEADS * MLA_HEAD_PAD).astype(BF16)
    wkv = mla_w_ukv.astype(BF16)
    qn = mla_q_norm.reshape(depth, 1, MLA_Q_LORA)
    kvn = mla_kv_norm.reshape(depth, 1, MLA_KV_LORA)
    tabs = _rope_tables(seq, rows.tm // HALF)
    perm_bt = _time_batch_to_batch_time(rows.tm)
    perm_tb = _time_batch_to_batch_time(rows_big.tm).T

    blk = _block_diag(jnp.ones((RWKV_HEADS, RWKV_HEAD, RWKV_HEAD), BF16))
    blk = jnp.concatenate([blk, blk, blk], axis=0)

    cc = jnp.concatenate([c, c_ctx[None], jnp.zeros((ROWS - batch - 1, d), F32)], axis=0)
    mods = _modulation(cc, ada_w, ada_b).reshape(depth, ROWS, N_MOD, d)
    m_lat = jnp.concatenate([mods[:, :batch], mods[:, :batch]], axis=1).transpose(0, 2, 1, 3)
    m_ctx = jnp.broadcast_to(mods[:, batch][:, :, None, :], (depth, N_MOD, ROWS, d))
    mods = jnp.stack([m_lat, m_ctx], axis=1)

    xs = jnp.concatenate([x.transpose(1, 0, 2).reshape(n_lat, d), ctx.transpose(1, 0, 2).reshape(n_ctx, d)], axis=0)

    for l in range(depth):
        xs = _ffn_half(xs, mods, nw, w1, w3, w2, rows_big, l, 0, 0, 0, 1)
        z_mla, z_rwkv, z_s5, z_lru = _in_projection(xs, mods, nw, w_mix, rows, l)

        q, k, v = _mla_project(z_mla, qn, kvn, wq, wkv, perm_bt, tabs, rows, l)
        ya = _mla_attention(q, k, v, seq, ctx_len, 256)

        mu = rwkv_mu[l]
        wl = jnp.concatenate([
            jnp.concatenate([jnp.concatenate([rwkv_w2[l, dd], jnp.zeros((RWKV_DECAY_LORA, w), F32)], axis=1),
                             jnp.concatenate([jnp.zeros((RWKV_A_LORA, w), F32), rwkv_a2[l, dd]], axis=1)], axis=0)
            for dd in range(2)], axis=0).astype(BF16)
        rp = {"mua": _rows_by_direction(mu), "mub": _rows_by_direction(mu[::-1]),
              "w0": _rows_by_direction(rwkv_w0[l]), "a0": _rows_by_direction(rwkv_a0[l]),
              "kk": rwkv_kk[l].reshape(1, w), "ka": rwkv_ka[l].reshape(1, w),
              "rk": rwkv_rk[l].reshape(1, w), "wl": wl, "g2": rwkv_g2[l].astype(BF16), "blk": blk}
        rw = _rwkv_mixer(z_rwkv, rp, scan)

        ar, ai, br, bi = _s5_discretise(s5_lam_re[l], s5_lam_im[l], s5_log_dt[l], s5_b_re[l], s5_b_im[l])
        gps = S5_GROUPS // S5_SLICES
        tdiag = lambda blocks: _block_diag(blocks.astype(F32).transpose(0, 2, 1))
        drive = lambda bm: (lambda bd: jnp.concatenate([bd[0], bd[1]], axis=1).astype(BF16))(
            _by_slices(tdiag, bm, S5_SLICES, gps))
        cre = _by_slices(tdiag, s5_c_re[l], S5_SLICES, gps)
        cim = _by_slices(tdiag, s5_c_im[l], S5_SLICES, gps)
        sp = {"ar": _rows_by_direction(ar.reshape(2, S5_W)), "ai": _rows_by_direction(ai.reshape(2, S5_W)),
              "wdr": drive(br), "wdi": drive(bi),
              "wc": jnp.concatenate([jnp.concatenate([cre[0], cre[1]], axis=2),
                                     jnp.concatenate([-cim[0], -cim[1]], axis=2)], axis=1).astype(BF16)}
        sf, sb = _s5_mixer(z_s5, sp, scan)

        diag = lambda blocks: _block_diag(blocks.astype(F32))
        lru_gate = lambda wm: (lambda bd: jnp.concatenate([bd[0], bd[1]], axis=1).astype(BF16))(
            _by_slices(diag, wm, w // LANE, LANE // LRU_BLOCK))
        fwd_rows = (jnp.arange(ROWS) < HALF).astype(F32)[None, :, None]
        cw = jnp.broadcast_to(lru_conv_w[l].astype(F32)[:, None, :], (LRU_CONV, ROWS, w))
        lp = {"cwf": cw * fwd_rows, "cwb": cw * (1.0 - fwd_rows), "cb": lru_conv_b[l].reshape(1, w),
              "wa": lru_gate(lru_wa[l]), "wx": lru_gate(lru_wx[l]),
              "ba": _rows_by_direction(lru_ba[l]), "bx": _rows_by_direction(lru_bx[l]),
              "sp": _rows_by_direction(jax.nn.softplus(-lru_lam[l].astype(F32)))}
        hf, hb = _lru_mixer(z_lru, lp, scan)

        bp = {"ln_w": rwkv_ln_w[l].reshape(1, w), "ln_b": rwkv_ln_b[l].reshape(1, w), "blk": blk,
              "s5_d": s5_d[l].reshape(1, w), "glu_w": s5_glu_w[l].astype(BF16), "glu_b": s5_glu_b[l].reshape(1, w)}
        yb_o, ys_o, yd_o = _branch_out(rw, (z_s5, sf, sb), (hf, hb, z_lru), bp, rows.tm)

        m = _gated_sum(xs, mods, nw, ya, yb_o, ys_o, yd_o, perm_tb, w_gate, gb, w_br, rows_big, l)
        xs = _out_projection(m, w_o, xs, mods, nw, rows_big, l)
        xs = _ffn_half(xs, mods, nw, w1, w3, w2, rows_big, l, 1, 2, 4, 5)

    return xs[:n_lat].reshape(seq, batch, d).transpose(1, 0, 2)
```

```python
import functools
import math

import jax
import jax.numpy as jnp
from jax import lax
from jax.experimental import pallas as pl
from jax.experimental.pallas import tpu as pltpu

F32 = jnp.float32
BF16 = jnp.bfloat16

D_MODEL = 2048
N_BRANCH = 4
BRANCH_W = 512
D_FF = 5632
FFN_RES = 0.5
N_MOD = 9
NORM_EPS = 1e-6
GRID_W = 64

MLA_HEADS = 4
MLA_NOPE = 128
MLA_ROPE = 64
MLA_V = 128
MLA_Q_LORA = 768
MLA_KV_LORA = 512
MLA_SCALE = (MLA_NOPE + MLA_ROPE) ** -0.5
ROPE_BASE = 10000.0
MLA_HEAD_PAD = 256
MLA_IN_PAD = 1408

RWKV_HEAD = 64
RWKV_HEADS = BRANCH_W // RWKV_HEAD
RWKV_DECAY_LORA = 64
RWKV_A_LORA = 64
RWKV_G_LORA = 128
RWKV_GN_EPS = 64e-5
RWKV_IN = 3 * BRANCH_W + RWKV_DECAY_LORA + RWKV_A_LORA + RWKV_G_LORA

S5_GROUP = 16
S5_GROUPS = BRANCH_W // S5_GROUP
S5_STATE = 64
S5_W = S5_GROUPS * S5_STATE

LRU_BLOCKS = 8
LRU_BLOCK = BRANCH_W // LRU_BLOCKS
LRU_CONV = 4
LRU_C = 8.0

MLA_IN = MLA_Q_LORA + MLA_KV_LORA + MLA_ROPE
O_RWKV = MLA_IN
O_S5 = O_RWKV + RWKV_IN
O_LRU = O_S5 + BRANCH_W
MIX_IN = O_LRU + 2 * BRANCH_W

ROWS = 8
HALF = ROWS // 2
LANE = 128
VMEM_LIMIT = 56 * 1024 * 1024


def _cparams(*sem):
    return pltpu.CompilerParams(dimension_semantics=sem, vmem_limit_bytes=VMEM_LIMIT)


def _rms(x, w):
    return x * lax.rsqrt(jnp.mean(x * x, axis=-1, keepdims=True) + NORM_EPS) * w


def _dot(a, b):
    return jnp.dot(a, b, preferred_element_type=F32)


def _gelu(x):
    return 0.5 * x * (1.0 + jnp.tanh(math.sqrt(2.0 / math.pi) * (x + 0.044715 * (x * x * x))))


def _rows8(x):
    return x.reshape(x.shape[0] // ROWS, ROWS, x.shape[1])


def _mod_in(x, g, mod_ref):
    return (_rows8(_rms(x, g)) * (1.0 + mod_ref[1]) + mod_ref[0]).reshape(x.shape)


def _mod_gate(y, g, mod_ref):
    return (_rows8(_rms(y, g)) * mod_ref[2]).reshape(y.shape)


def _mod_kernel(cc_ref, w_ref, b_ref, o_ref):
    cc = cc_ref[...]
    s = (cc * jax.nn.sigmoid(cc)).astype(BF16)
    o_ref[...] = _dot(s, w_ref[...].astype(BF16)) + b_ref[...]


def _modulation(cc, ada_w, ada_b):
    L, D, N = ada_w.shape
    tn = 1024
    return pl.pallas_call(
        _mod_kernel,
        grid=(L, N // tn),
        in_specs=[pl.BlockSpec((ROWS, D), lambda l, j: (0, 0)),
                  pl.BlockSpec((None, D, tn), lambda l, j: (l, 0, j)),
                  pl.BlockSpec((None, 1, tn), lambda l, j: (l, 0, j))],
        out_specs=pl.BlockSpec((None, ROWS, tn), lambda l, j: (l, 0, j)),
        out_shape=jax.ShapeDtypeStruct((L, ROWS, N), F32),
        compiler_params=_cparams("parallel", "parallel"),
        name="modulation",
    )(cc, ada_w, ada_b.reshape(L, 1, N))


def _cast_kernel(x_ref, o_ref):
    o_ref[...] = x_ref[...].astype(BF16)


def _cast_bf16(w, rows_per_block):
    x = w.reshape(-1, w.shape[-1])
    n, c = x.shape
    return pl.pallas_call(
        _cast_kernel,
        grid=(n // rows_per_block,),
        in_specs=[pl.BlockSpec((rows_per_block, c), lambda i: (i, 0))],
        out_specs=pl.BlockSpec((rows_per_block, c), lambda i: (i, 0)),
        out_shape=jax.ShapeDtypeStruct((n, c), BF16),
        compiler_params=_cparams("parallel"),
        name="cast_bf16",
    )(x).reshape(w.shape)


MIX_WIDTHS = (MLA_IN_PAD, RWKV_IN, BRANCH_W, 2 * BRANCH_W)
MIX_PACKED = sum(MIX_WIDTHS)
MLA_KR0 = MLA_Q_LORA + MLA_KV_LORA


def _pack_kernel(w_ref, p_ref, mix_ref, gate_ref):
    x = w_ref[...]
    mix_ref[:, :MLA_KR0] = x[:, :MLA_KR0].astype(BF16)
    mix_ref[:, MLA_KR0:MLA_IN] = _dot(x[:, MLA_KR0:MLA_IN].astype(BF16), p_ref[...]).astype(BF16)
    mix_ref[:, MLA_IN:MLA_IN_PAD] = jnp.zeros((x.shape[0], MLA_IN_PAD - MLA_IN), BF16)
    mix_ref[:, MLA_IN_PAD:] = x[:, MLA_IN:MIX_IN].astype(BF16)
    gate_ref[...] = x[:, MIX_IN:].astype(BF16)


def _pack_w_in(w_in, rope_perm):
    depth, d, n_in = w_in.shape
    tr = 256
    n_gate = n_in - MIX_IN
    return pl.pallas_call(
        _pack_kernel,
        grid=(depth, d // tr),
        in_specs=[pl.BlockSpec((None, tr, n_in), lambda l, i: (l, i, 0)),
                  pl.BlockSpec((MLA_ROPE, MLA_ROPE), lambda l, i: (0, 0))],
        out_specs=[pl.BlockSpec((None, tr, MIX_PACKED), lambda l, i: (l, i, 0)),
                   pl.BlockSpec((None, tr, n_gate), lambda l, i: (l, i, 0))],
        out_shape=[jax.ShapeDtypeStruct((depth, d, MIX_PACKED), BF16),
                   jax.ShapeDtypeStruct((depth, d, n_gate), BF16)],
        compiler_params=_cparams("parallel", "parallel"),
        name="pack_w_in",
    )(w_in, rope_perm)


class _Rows:
    def __init__(self, n_lat, n_ctx, tm):
        assert n_lat % tm == 0 and n_ctx % tm == 0
        self.tm = tm
        self.n_tiles = (n_lat + n_ctx) // tm
        self.lat_tiles = n_lat // tm

    def segment(self, i):
        return (i >= self.lat_tiles).astype(jnp.int32)

    def pos_tile(self, i):
        return jnp.minimum(i, self.lat_tiles)


def _mod_spec(rows, l, group, grid_rank):
    if grid_rank == 2:
        return pl.BlockSpec((None, None, 3, ROWS, D_MODEL), lambda i, j: (l, rows.segment(i), group, 0, 0))
    return pl.BlockSpec((None, None, 3, ROWS, D_MODEL), lambda i: (l, rows.segment(i), group, 0, 0))


def _nw_spec(l, idx, grid_rank):
    if grid_rank == 2:
        return pl.BlockSpec((None, None, 1, D_MODEL), lambda i, j: (l, idx, 0, 0))
    return pl.BlockSpec((None, None, 1, D_MODEL), lambda i: (l, idx, 0, 0))


def _ffn_kernel(x_ref, mod_ref, gpre_ref, gpost_ref, w1_ref, w3_ref, w2_ref, o_ref, h_ref, acc_ref):
    j = pl.program_id(1)

    @pl.when(j == 0)
    def _():
        h_ref[...] = _mod_in(x_ref[...], gpre_ref[...], mod_ref).astype(BF16)
        acc_ref[...] = jnp.zeros_like(acc_ref)

    h = h_ref[...]
    a = _dot(h, w1_ref[...])
    b = _dot(h, w3_ref[...])
    u = (a * jax.nn.sigmoid(a) * b).astype(BF16)
    acc_ref[...] += _dot(u, w2_ref[...])

    @pl.when(j == pl.num_programs(1) - 1)
    def _():
        o_ref[...] = x_ref[...] + FFN_RES * _mod_gate(acc_ref[...], gpost_ref[...], mod_ref)


def _ffn_half(x, mods, nw, w1, w3, w2, rows, l, hf, group, i_pre, i_post):
    n, d = x.shape
    tm, tf = rows.tm, 512
    return pl.pallas_call(
        _ffn_kernel,
        grid=(rows.n_tiles, D_FF // tf),
        in_specs=[pl.BlockSpec((tm, d), lambda i, j: (i, 0)),
                  _mod_spec(rows, l, group, 2),
                  _nw_spec(l, i_pre, 2), _nw_spec(l, i_post, 2),
                  pl.BlockSpec((None, None, d, tf), lambda i, j: (l, hf, 0, j)),
                  pl.BlockSpec((None, None, d, tf), lambda i, j: (l, hf, 0, j)),
                  pl.BlockSpec((None, None, tf, d), lambda i, j: (l, hf, j, 0))],
        out_specs=pl.BlockSpec((tm, d), lambda i, j: (i, 0)),
        out_shape=jax.ShapeDtypeStruct((n, d), F32),
        scratch_shapes=[pltpu.VMEM((tm, d), BF16), pltpu.VMEM((tm, d), F32)],
        compiler_params=_cparams("parallel", "arbitrary"),
        name="ffn_half",
    )(x, mods, nw, nw, w1, w3, w2)


def _inproj_kernel(x_ref, mod_ref, g_ref, w_ref, *o_refs):
    h = _mod_in(x_ref[...], g_ref[...], mod_ref).astype(BF16)
    lo = 0
    for o_ref, wd in zip(o_refs, MIX_WIDTHS):
        o_ref[...] = _dot(h, w_ref[:, lo:lo + wd])
        lo += wd


def _in_projection(x, mods, nw, w_packed, rows, l):
    n, d = x.shape
    tm = rows.tm
    return pl.pallas_call(
        _inproj_kernel,
        grid=(rows.n_tiles,),
        in_specs=[pl.BlockSpec((tm, d), lambda i: (i, 0)), _mod_spec(rows, l, 1, 1), _nw_spec(l, 2, 1),
                  pl.BlockSpec((None, d, MIX_PACKED), lambda i: (l, 0, 0))],
        out_specs=[pl.BlockSpec((tm, wd), lambda i: (i, 0)) for wd in MIX_WIDTHS],
        out_shape=[jax.ShapeDtypeStruct((n, wd), F32) for wd in MIX_WIDTHS],
        compiler_params=_cparams("parallel"),
        name="in_projection",
    )(x, mods, nw, w_packed)


def _gate_kernel(x_ref, mod_ref, g_ref, ya_ref, yb_ref, ys_ref, yd_ref, pt_ref, wg_ref, gb_ref, wb_ref,
                 o_ref, h_ref, acc_ref, y_ref):
    k = pl.program_id(1)
    tm = x_ref.shape[0]

    @pl.when(k == 0)
    def _():
        h_ref[...] = _mod_in(x_ref[...], g_ref[...], mod_ref).astype(BF16)
        acc_ref[...] = jnp.zeros_like(acc_ref)
        y_ref[...] = _dot(pt_ref[...], ya_ref[...].reshape(tm, BRANCH_W)).astype(BF16)

    for kk, ref in ((1, yb_ref), (2, ys_ref), (3, yd_ref)):
        @pl.when(k == kk)
        def _(ref=ref):
            y_ref[...] = ref[...]

    zg = _dot(h_ref[...], wg_ref[...]) + gb_ref[...]
    acc_ref[...] += jax.nn.sigmoid(zg) * _dot(y_ref[...], wb_ref[...])

    @pl.when(k == pl.num_programs(1) - 1)
    def _():
        o_ref[...] = acc_ref[...].astype(BF16)


def _gated_sum(x, mods, nw, ya, yb, ys, yd, perm_t, w_gate, gate_b, w_branch, rows, l):
    n, d = x.shape
    tm = rows.tm
    tok = pl.BlockSpec((tm, BRANCH_W), lambda i, k: (i, 0))
    return pl.pallas_call(
        _gate_kernel,
        grid=(rows.n_tiles, N_BRANCH),
        in_specs=[pl.BlockSpec((tm, d), lambda i, k: (i, 0)),
                  _mod_spec(rows, l, 1, 2), _nw_spec(l, 2, 2),
                  pl.BlockSpec((HALF, tm // HALF, BRANCH_W), lambda i, k: (0, i, 0)),
                  tok, tok, tok,
                  pl.BlockSpec((tm, tm), lambda i, k: (0, 0)),
                  pl.BlockSpec((None, d, d), lambda i, k: (l, 0, k)),
                  pl.BlockSpec((None, None, 1, d), lambda i, k: (l, k, 0, 0)),
                  pl.BlockSpec((None, None, BRANCH_W, d), lambda i, k: (l, k, 0, 0))],
        out_specs=pl.BlockSpec((tm, d), lambda i, k: (i, 0)),
        out_shape=jax.ShapeDtypeStruct((n, d), BF16),
        scratch_shapes=[pltpu.VMEM((tm, d), BF16), pltpu.VMEM((tm, d), F32), pltpu.VMEM((tm, BRANCH_W), BF16)],
        compiler_params=_cparams("parallel", "arbitrary"),
        name="gated_sum",
    )(x, mods, nw, ya, yb, ys, yd, perm_t, w_gate, gate_b, w_branch)


def _outproj_kernel(m_ref, w_ref, x_ref, mod_ref, g_ref, o_ref):
    m = _dot(m_ref[...], w_ref[...])
    o_ref[...] = x_ref[...] + _mod_gate(m, g_ref[...], mod_ref)


def _out_projection(m, w_out, x, mods, nw, rows, l):
    n, d = x.shape
    tm = rows.tm
    return pl.pallas_call(
        _outproj_kernel,
        grid=(rows.n_tiles,),
        in_specs=[pl.BlockSpec((tm, d), lambda i: (i, 0)),
                  pl.BlockSpec((None, d, d), lambda i: (l, 0, 0)),
                  pl.BlockSpec((tm, d), lambda i: (i, 0)),
                  _mod_spec(rows, l, 1, 1), _nw_spec(l, 3, 1)],
        out_specs=pl.BlockSpec((tm, d), lambda i: (i, 0)),
        out_shape=jax.ShapeDtypeStruct((n, d), F32),
        compiler_params=_cparams("parallel"),
        name="out_projection",
    )(m, w_out, x, mods, nw)


def _rope_lanes(x, c, s_lo, s_hi, width):
    return x * c + pltpu.roll(x, width - MLA_ROPE // 2, 1) * s_lo + pltpu.roll(x, MLA_ROPE // 2, 1) * s_hi


def _mla_proj_kernel(z_ref, qn_ref, kvn_ref, wq_ref, wkv_ref, perm_ref, qc_ref, qs1_ref, qs2_ref,
                     kc_ref, ks1_ref, ks2_ref, q_ref, k_ref, v_ref):
    z = z_ref[...]
    tm = z.shape[0]
    nt = tm // HALF
    pm = perm_ref[...]
    cq = _dot(pm, _rms(z[:, :MLA_Q_LORA], qn_ref[...]).astype(BF16)).astype(BF16)
    ckv = _dot(pm, _rms(z[:, MLA_Q_LORA:MLA_Q_LORA + MLA_KV_LORA], kvn_ref[...]).astype(BF16)).astype(BF16)
    kr = z[:, MLA_Q_LORA + MLA_KV_LORA:]
    krr = _rope_lanes(kr, kc_ref[...], ks1_ref[...], ks2_ref[...], LANE).astype(BF16)
    krr = _dot(pm, krr).astype(BF16)
    q = _dot(cq, wq_ref[...])
    kv = _dot(ckv, wkv_ref[...])
    tile4 = lambda t: jnp.concatenate([t] * HALF, axis=0)
    qc, qs1, qs2 = tile4(qc_ref[...]), tile4(qs1_ref[...]), tile4(qs2_ref[...])
    for h in range(MLA_HEADS):
        qh = _rope_lanes(q[:, h * MLA_HEAD_PAD:(h + 1) * MLA_HEAD_PAD], qc, qs1, qs2, MLA_HEAD_PAD).astype(BF16)
        kn = kv[:, h * 256:h * 256 + MLA_NOPE].astype(BF16)
        vh = kv[:, h * 256 + MLA_NOPE:(h + 1) * 256].astype(BF16)
        for b in range(HALF):
            rs = slice(b * nt, (b + 1) * nt)
            q_ref[b, :, h * MLA_HEAD_PAD:(h + 1) * MLA_HEAD_PAD] = qh[rs]
            k_ref[b, :, h * MLA_HEAD_PAD:h * MLA_HEAD_PAD + MLA_NOPE] = kn[rs]
            k_ref[b, :, h * MLA_HEAD_PAD + MLA_NOPE:(h + 1) * MLA_HEAD_PAD] = krr[rs]
            v_ref[b, :, h * MLA_V:(h + 1) * MLA_V] = vh[rs]


def _mla_project(z, qn, kvn, wq, wkv, perm, tabs, rows, l):
    n = z.shape[0]
    tm = rows.tm
    nt = tm // HALF
    hq = MLA_HEADS * MLA_HEAD_PAD
    tab_q = pl.BlockSpec((nt, MLA_HEAD_PAD), lambda i: (rows.pos_tile(i), 0))
    tab_k = pl.BlockSpec((tm, LANE), lambda i: (rows.pos_tile(i), 0))
    out = lambda wd: pl.BlockSpec((HALF, nt, wd), lambda i: (0, i, 0))
    return pl.pallas_call(
        _mla_proj_kernel,
        grid=(rows.n_tiles,),
        in_specs=[pl.BlockSpec((tm, MLA_IN_PAD), lambda i: (i, 0)),
                  pl.BlockSpec((None, 1, MLA_Q_LORA), lambda i: (l, 0, 0)),
                  pl.BlockSpec((None, 1, MLA_KV_LORA), lambda i: (l, 0, 0)),
                  pl.BlockSpec((None, MLA_Q_LORA, hq), lambda i: (l, 0, 0)),
                  pl.BlockSpec((None, MLA_KV_LORA, hq), lambda i: (l, 0, 0)),
                  pl.BlockSpec((tm, tm), lambda i: (0, 0)),
                  tab_q, tab_q, tab_q, tab_k, tab_k, tab_k],
        out_specs=[out(hq), out(hq), out(MLA_HEADS * MLA_V)],
        out_shape=[jax.ShapeDtypeStruct((HALF, n // HALF, hq), BF16),
                   jax.ShapeDtypeStruct((HALF, n // HALF, hq), BF16),
                   jax.ShapeDtypeStruct((HALF, n // HALF, MLA_HEADS * MLA_V), BF16)],
        compiler_params=_cparams("parallel"),
        name="mla_project",
    )(z, qn, kvn, wq, wkv, perm, *tabs)


def _scores(q, k):
    return lax.dot_general(q, k, (((1,), (1,)), ((), ())), preferred_element_type=F32)


def _attn_kernel(q_ref, k1_ref, v1_ref, k2_ref, v2_ref, o_ref):
    q = q_ref[...]
    s1 = _scores(q, k1_ref[...])
    s2 = _scores(q, k2_ref[...])
    m = jnp.maximum(jnp.max(s1, axis=-1, keepdims=True), jnp.max(s2, axis=-1, keepdims=True))
    p1 = jnp.exp(s1 - m)
    p2 = jnp.exp(s2 - m)
    den = jnp.sum(p1, axis=-1, keepdims=True) + jnp.sum(p2, axis=-1, keepdims=True)
    o = _dot(p1.astype(BF16), v1_ref[...]) + _dot(p2.astype(BF16), v2_ref[...])
    o_ref[...] = (o / den).astype(BF16)


def _attn_ctx_kernel(q_ref, k_ref, v_ref, prev_ref, o_ref):
    del prev_ref
    s = _scores(q_ref[...], k_ref[...])
    p = jnp.exp(s - jnp.max(s, axis=-1, keepdims=True))
    o = _dot(p.astype(BF16), v_ref[...])
    o_ref[...] = (o / jnp.sum(p, axis=-1, keepdims=True)).astype(BF16)


def _mla_attention(q, k, v, seq, ctx_len, tq):
    batch, s_all, _ = q.shape
    cb = seq // ctx_len
    qt = seq // tq
    lat = pl.pallas_call(
        _attn_kernel,
        grid=(batch, MLA_HEADS, qt),
        in_specs=[pl.BlockSpec((None, tq, MLA_HEAD_PAD), lambda b, h, i: (b, i, h)),
                  pl.BlockSpec((None, seq, MLA_HEAD_PAD), lambda b, h, i: (b, 0, h)),
                  pl.BlockSpec((None, seq, MLA_V), lambda b, h, i: (b, 0, h)),
                  pl.BlockSpec((None, ctx_len, MLA_HEAD_PAD), lambda b, h, i: (b, cb, h)),
                  pl.BlockSpec((None, ctx_len, MLA_V), lambda b, h, i: (b, cb, h))],
        out_specs=pl.BlockSpec((None, tq, MLA_V), lambda b, h, i: (b, i, h)),
        out_shape=jax.ShapeDtypeStruct((batch, s_all, MLA_HEADS * MLA_V), BF16),
        compiler_params=_cparams("parallel", "parallel", "parallel"),
        name="mla_attention",
    )(q, k, v, k, v)
    return pl.pallas_call(
        _attn_ctx_kernel,
        grid=(batch, MLA_HEADS),
        in_specs=[pl.BlockSpec((None, ctx_len, MLA_HEAD_PAD), lambda b, h: (b, cb, h)),
                  pl.BlockSpec((None, ctx_len, MLA_HEAD_PAD), lambda b, h: (b, cb, h)),
                  pl.BlockSpec((None, ctx_len, MLA_V), lambda b, h: (b, cb, h)),
                  pl.BlockSpec(memory_space=pl.ANY)],
        out_specs=pl.BlockSpec((None, ctx_len, MLA_V), lambda b, h: (b, cb, h)),
        out_shape=jax.ShapeDtypeStruct((batch, s_all, MLA_HEADS * MLA_V), BF16),
        input_output_aliases={3: 0},
        compiler_params=_cparams("parallel", "parallel"),
        name="mla_attention_ctx",
    )(q, k, v, lat)


class _Scan:
    def __init__(self, seq, ctx_len, tc):
        assert seq % tc == 0 and ctx_len % tc == 0 and tc % 2 == 0
        self.tc = tc
        self.rows = tc * HALF
        self.cs = ctx_len // tc
        self.lat = seq // tc
        self.n = self.cs + self.lat
        self.tiles = tc // 2
        self.n_tiles = self.n * self.tiles

    def fwd(self, g):
        return jnp.where(g < self.cs, self.lat + g, g - self.cs)

    def bwd(self, g):
        return jnp.where(g < self.cs, self.lat + self.cs - 1 - g, self.n - 1 - g)

    def chunk_specs(self, cols, col_block=0):
        return [pl.BlockSpec((self.rows, cols), lambda g: (self.fwd(g), col_block)),
                pl.BlockSpec((self.rows, cols), lambda g: (self.bwd(g), col_block))]

    def halo_specs(self, cols, col_block=0):
        t, last = self.tiles, self.n_tiles - 1
        return [pl.BlockSpec((ROWS, cols), lambda g: (jnp.maximum(self.fwd(g) * t - 1, 0), col_block)),
                pl.BlockSpec((ROWS, cols), lambda g: (jnp.minimum((self.fwd(g) + 1) * t, last), col_block)),
                pl.BlockSpec((ROWS, cols), lambda g: (jnp.minimum((self.bwd(g) + 1) * t, last), col_block)),
                pl.BlockSpec((ROWS, cols), lambda g: (jnp.maximum(self.bwd(g) * t - 1, 0), col_block))]


def _flip0(x):
    n = x.shape[0]
    if n == 1:
        return x
    return jnp.concatenate([x[n - 1 - k:n - k] for k in range(n)], axis=0)


def _low_rows(shape):
    return lax.broadcasted_iota(jnp.int32, shape, len(shape) - 2) % ROWS < HALF


def _merge_dirs(df, db):
    c = df.shape[-1]
    n2 = df.shape[0] // ROWS
    t = df.reshape(n2, ROWS, c)
    u = _flip0(db.reshape(n2, ROWS, c))
    lo = _low_rows(t.shape)
    even = jnp.where(lo, t, u)
    odd = pltpu.roll(jnp.where(lo, u, t), HALF, 1)
    return jnp.stack([even, odd], axis=1).reshape(2 * n2, ROWS, c)


def _split_dirs(y):
    n, _, c = y.shape
    y2 = y.reshape(n // 2, 2 * ROWS, c)
    a, b = y2[:, :ROWS], y2[:, ROWS:]
    rb = pltpu.roll(b, HALF, 1)
    lo = _low_rows(a.shape)
    df = jnp.where(lo, a, rb).reshape(n * HALF, c)
    db = _flip0(jnp.where(lo, rb, a)).reshape(n * HALF, c)
    return df, db


def _backward_rows(shape):
    return lax.broadcasted_iota(jnp.int32, shape, len(shape) - 2) % ROWS >= HALF


def _by_direction(x, bwd):
    zero = jnp.zeros_like(x)
    return jnp.concatenate([jnp.where(bwd, zero, x), jnp.where(bwd, x, zero)], axis=-1)


def _halo_valid(g, n_chunks, seg_chunk):
    prev_ok = jnp.logical_and(g != 0, g != seg_chunk)
    next_ok = jnp.logical_and(g != seg_chunk - 1, g != n_chunks - 1)
    return prev_ok.astype(F32), next_ok.astype(F32)


def _head_sum(x, blk3):
    hi = x.astype(BF16)
    r1 = x - hi.astype(F32)
    mid = r1.astype(BF16)
    lo = (r1 - mid.astype(F32)).astype(BF16)
    return _dot(jnp.concatenate([hi, mid, lo], axis=-1), blk3)


RWKV_ACC = 2
RWKV_VQ = 4
RWKV_VROWS = RWKV_HEAD // RWKV_VQ
RWKV_PAIRS = RWKV_HEADS // 2
RWKV_SEQ = RWKV_PAIRS * ROWS


def _to_scan_layout(x):
    a = jnp.concatenate([x[:, q * LANE:(q + 1) * LANE] for q in range(RWKV_PAIRS)], axis=0)
    return jnp.concatenate([a] * RWKV_VQ, axis=0).T


def _from_scan_layout(ys):
    t = jnp.concatenate([y for y in ys for _ in range(RWKV_VQ)], axis=0).T.reshape(RWKV_VQ, RWKV_SEQ, LANE)
    quarter = (lax.broadcasted_iota(jnp.int32, (RWKV_SEQ, LANE), 1) // RWKV_VROWS) % RWKV_VQ
    ya = t[RWKV_VQ - 1]
    for iq in range(RWKV_VQ - 2, -1, -1):
        ya = jnp.where(quarter == iq, t[iq], ya)
    return jnp.concatenate([ya[q * ROWS:(q + 1) * ROWS] for q in range(RWKV_PAIRS)], axis=1)


def _rwkv_kernel(seg_chunk, zf_ref, zb_ref, zfp_ref, zfn_ref, zbn_ref, zbp_ref, mua_ref, mub_ref, w0_ref, a0_ref,
                 kk_ref, ka_ref, rk_ref, wl_ref, g2_ref, blk_ref,
                 yf_out, yb_out, bf_out, bb_out, g_out, *scratch):
    nat_refs, bufs_a, bufs_b = scratch[0:5], scratch[5:10], scratch[10:15]
    wend_ref, y_ref, p_ref = scratch[15], scratch[16], scratch[17]
    g = pl.program_id(0)

    @pl.when(g == 0)
    def _():
        p_ref[...] = jnp.zeros_like(p_ref)

    prev_ok, next_ok = _halo_valid(g, pl.num_programs(0), seg_chunk)
    z = _merge_dirs(zf_ref[...], zb_ref[...])
    tc = z.shape[0]
    zprev = jnp.concatenate([_merge_dirs(zfp_ref[...], zbn_ref[...])[1:2] * prev_ok, z[:-1]], axis=0)
    znext = jnp.concatenate([z[1:], _merge_dirs(zfn_ref[...], zbp_ref[...])[0:1] * next_ok], axis=0)
    zs = z + (zprev - z) * mua_ref[...] + (znext - z) * mub_ref[...]
    w = BRANCH_W
    r, k, v = zs[..., :w], zs[..., w:2 * w], zs[..., 2 * w:3 * w]
    wa = zs[..., 3 * w:3 * w + LANE]
    gl = zs[..., 3 * w + LANE:]
    lane = lax.broadcasted_iota(jnp.int32, wa.shape, 2)
    t = jnp.where(lane < RWKV_DECAY_LORA, jnp.tanh(wa), wa)
    bwd = _backward_rows(t.shape)
    lhs = _by_direction(t, bwd).reshape(tc * ROWS, 2 * LANE).astype(BF16)
    lo = _dot(lhs, wl_ref[...]).reshape(tc, ROWS, 2 * w)
    w_pre = w0_ref[...] + lo[..., :w]
    decay = jnp.exp(-math.exp(-0.5) * jax.nn.sigmoid(w_pre))
    asig = jax.nn.sigmoid(a0_ref[...] + lo[..., w:])
    blk = blk_ref[...]
    kkv = k * kk_ref[...]
    ssq = _head_sum((kkv * kkv).reshape(tc * ROWS, w), blk).reshape(tc, ROWS, w)
    kkn = kkv * lax.rsqrt(ssq + 1e-12)
    kd = k * (1.0 + (asig - 1.0) * ka_ref[...])
    gg = _dot(jax.nn.sigmoid(gl).reshape(tc * ROWS, RWKV_G_LORA).astype(BF16), g2_ref[...]).reshape(tc, ROWS, w)
    rkd = _head_sum((r * kd * rk_ref[...]).reshape(tc * ROWS, w), blk).reshape(tc, ROWS, w)
    g_out[...] = _split_dirs(gg)[0]
    bf_out[...], bb_out[...] = _split_dirs(rkd * v)

    cum = [decay[0]]
    for s in range(1, tc):
        cum.append(cum[-1] * decay[s])
    wc = jnp.stack(cum, axis=0)
    w_before = jnp.concatenate([jnp.ones_like(wc[:1]), wc[:-1]], axis=0)
    inv = 1.0 / wc
    for ref, val in zip(nat_refs, (kkn * w_before, kkn * asig * inv, kd * inv, r * wc, v)):
        ref[...] = val
    wend_ref[...] = _to_scan_layout(cum[-1])

    def stage(s, bufs):
        for src, dst in zip(nat_refs, bufs):
            dst[...] = _to_scan_layout(src[s])

    def advance(s, bufs):
        tkk_ref, tkb_ref, tkd_ref, tr_ref, tv_ref = bufs
        copy = lax.broadcasted_iota(jnp.int32, (RWKV_VROWS, LANE), 1) // RWKV_SEQ
        ys = []
        for par in range(2):
            base = par * RWKV_HEAD
            vt = tv_ref[base + (RWKV_VQ - 1) * RWKV_VROWS:base + RWKV_VQ * RWKV_VROWS, :]
            for iq in range(RWKV_VQ - 2, -1, -1):
                vt = jnp.where(copy == iq, tv_ref[base + iq * RWKV_VROWS:base + (iq + 1) * RWKV_VROWS, :], vt)
            acc = [None] * RWKV_ACC
            for j in range(RWKV_HEAD):
                term = p_ref[base + j] * tkk_ref[base + j:base + j + 1, :]
                acc[j % RWKV_ACC] = term if acc[j % RWKV_ACC] is None else acc[j % RWKV_ACC] + term
            sa = -functools.reduce(lambda a, b: a + b, acc)
            yacc = [None] * RWKV_ACC
            for j in range(RWKV_HEAD):
                row = slice(base + j, base + j + 1)
                pn = p_ref[base + j] + sa * tkb_ref[row, :] + vt * tkd_ref[row, :]
                p_ref[base + j] = pn
                term = pn * tr_ref[row, :]
                yacc[j % RWKV_ACC] = term if yacc[j % RWKV_ACC] is None else yacc[j % RWKV_ACC] + term
            ys.append(functools.reduce(lambda a, b: a + b, yacc))
        y_ref[s] = _from_scan_layout(ys)

    def two_steps(h, carry):
        s = 2 * h
        stage(s + 1, bufs_b)
        advance(s, bufs_a)
        stage(jnp.minimum(s + 2, tc - 1), bufs_a)
        advance(s + 1, bufs_b)
        return carry

    stage(0, bufs_a)
    lax.fori_loop(0, tc // 2, two_steps, 0)
    for row in range(2 * RWKV_HEAD):
        p_ref[row] = p_ref[row] * wend_ref[row:row + 1, :]
    yf_out[...], yb_out[...] = _split_dirs(y_ref[...])


def _rwkv_mixer(z, p, scan):
    n = z.shape[0]
    w = BRANCH_W
    row_spec = lambda cols: pl.BlockSpec((ROWS, cols), lambda g: (0, 0))
    vec = pl.BlockSpec((1, w), lambda g: (0, 0))
    fout = pl.BlockSpec((scan.rows, w), lambda g: (scan.fwd(g), 0))
    bout = pl.BlockSpec((scan.rows, w), lambda g: (scan.bwd(g), 0))
    chunk = pltpu.VMEM((scan.tc, ROWS, w), F32)
    one_step = pltpu.VMEM((LANE, LANE), F32)
    return pl.pallas_call(
        functools.partial(_rwkv_kernel, scan.cs),
        grid=(scan.n,),
        in_specs=scan.chunk_specs(RWKV_IN) + scan.halo_specs(RWKV_IN)
        + [row_spec(RWKV_IN), row_spec(RWKV_IN), row_spec(w), row_spec(w), vec, vec, vec,
           pl.BlockSpec((2 * LANE, 2 * w), lambda g: (0, 0)),
           pl.BlockSpec((RWKV_G_LORA, w), lambda g: (0, 0)),
           pl.BlockSpec((3 * w, w), lambda g: (0, 0))],
        out_specs=[fout, bout, fout, bout, fout],
        out_shape=[jax.ShapeDtypeStruct((n, w), F32)] * 5,
        scratch_shapes=[chunk] * 5 + [one_step] * 11 + [chunk, pltpu.VMEM((2 * RWKV_HEAD, RWKV_VROWS, LANE), F32)],
        compiler_params=_cparams("arbitrary"),
        name="rwkv_mixer",
    )(z, z, z, z, z, z, p["mua"], p["mub"], p["w0"], p["a0"], p["kk"], p["ka"], p["rk"], p["wl"], p["g2"], p["blk"])


S5_SLICES = BRANCH_W // LANE
S5_SLICE_W = S5_W // S5_SLICES
S5_COLS = 512


def _s5_kernel(uf_ref, ub_ref, ar_ref, ai_ref, wdr_ref, wdi_ref, wc_ref, yf_out, yb_out,
               dr_ref, di_ref, hr_ref, hi_ref):
    @pl.when(pl.program_id(0) == 0)
    def _():
        hr_ref[...] = jnp.zeros_like(hr_ref)
        hi_ref[...] = jnp.zeros_like(hi_ref)

    u = _merge_dirs(uf_ref[...], ub_ref[...])
    tc = u.shape[0]
    m = tc * ROWS
    u2 = u.reshape(m, BRANCH_W)
    bwd = _backward_rows((m, LANE))
    for q in range(S5_SLICES):
        lhs = _by_direction(u2[:, q * LANE:(q + 1) * LANE], bwd).astype(BF16)
        dr_ref[:, q * S5_SLICE_W:(q + 1) * S5_SLICE_W] = _dot(lhs, wdr_ref[q])
        di_ref[:, q * S5_SLICE_W:(q + 1) * S5_SLICE_W] = _dot(lhs, wdi_ref[q])

    for cg in range(S5_W // S5_COLS):
        cols = slice(cg * S5_COLS, (cg + 1) * S5_COLS)
        ar = ar_ref[:, cols]
        ai = ai_ref[:, cols]

        def step(s, carry, cols=cols, ar=ar, ai=ai):
            hr, hi = carry
            rows = pl.ds(pl.multiple_of(s * ROWS, ROWS), ROWS)
            nr = ar * hr - ai * hi + dr_ref[rows, cols]
            ni = ar * hi + ai * hr + di_ref[rows, cols]
            dr_ref[rows, cols] = nr
            di_ref[rows, cols] = ni
            return nr, ni

        hr, hi = lax.fori_loop(0, tc, step, (hr_ref[:, cols], hi_ref[:, cols]))
        hr_ref[:, cols] = hr
        hi_ref[:, cols] = hi

    ys = []
    for q in range(S5_SLICES):
        hre = dr_ref[:, q * S5_SLICE_W:(q + 1) * S5_SLICE_W]
        him = di_ref[:, q * S5_SLICE_W:(q + 1) * S5_SLICE_W]
        both = _dot(jnp.concatenate([hre, him], axis=-1).astype(BF16), wc_ref[q])
        ys.append(jnp.where(bwd, both[:, LANE:], both[:, :LANE]))
    y = jnp.concatenate(ys, axis=-1).reshape(tc, ROWS, BRANCH_W)
    yf_out[...], yb_out[...] = _split_dirs(y)


def _s5_mixer(z, p, scan):
    n = z.shape[0]
    m = scan.tc * ROWS
    w = BRANCH_W
    return pl.pallas_call(
        _s5_kernel,
        grid=(scan.n,),
        in_specs=scan.chunk_specs(w)
        + [pl.BlockSpec((ROWS, S5_W), lambda g: (0, 0)),
           pl.BlockSpec((ROWS, S5_W), lambda g: (0, 0)),
           pl.BlockSpec((S5_SLICES, 2 * LANE, S5_SLICE_W), lambda g: (0, 0, 0)),
           pl.BlockSpec((S5_SLICES, 2 * LANE, S5_SLICE_W), lambda g: (0, 0, 0)),
           pl.BlockSpec((S5_SLICES, 2 * S5_SLICE_W, 2 * LANE), lambda g: (0, 0, 0))],
        out_specs=[pl.BlockSpec((scan.rows, w), lambda g: (scan.fwd(g), 0)),
                   pl.BlockSpec((scan.rows, w), lambda g: (scan.bwd(g), 0))],
        out_shape=[jax.ShapeDtypeStruct((n, w), F32)] * 2,
        scratch_shapes=[pltpu.VMEM((m, S5_W), F32), pltpu.VMEM((m, S5_W), F32),
                        pltpu.VMEM((ROWS, S5_W), F32), pltpu.VMEM((ROWS, S5_W), F32)],
        compiler_params=_cparams("arbitrary"),
        name="s5_mixer",
    )(z, z, p["ar"], p["ai"], p["wdr"], p["wdi"], p["wc"])


LRU_HALO = 2


def _lru_kernel(seg_chunk, xf_ref, xb_ref, xfp_ref, xfn_ref, xbn_ref, xbp_ref, cwf_ref, cwb_ref, cb_ref,
                wa_ref, wx_ref, ba_ref, bx_ref, sp_ref, hf_out, hb_out, a_ref, b_ref, hs_ref, h_ref):
    g = pl.program_id(0)

    @pl.when(g == 0)
    def _():
        h_ref[...] = jnp.zeros_like(h_ref)

    prev_ok, next_ok = _halo_valid(g, pl.num_programs(0), seg_chunk)
    x = _merge_dirs(xf_ref[...], xb_ref[...])
    tc = x.shape[0]
    ext = jnp.concatenate([_merge_dirs(xfp_ref[...], xbn_ref[...]) * prev_ok, x,
                           _merge_dirs(xfn_ref[...], xbp_ref[...]) * next_ok], axis=0)
    xcv = cb_ref[...]
    for j in range(LRU_CONV):
        xcv = xcv + ext[j:j + tc] * cwf_ref[j] + ext[LRU_HALO * 2 - j:LRU_HALO * 2 - j + tc] * cwb_ref[j]
    m = tc * ROWS
    x2 = xcv.reshape(m, BRANCH_W)
    bwd = _backward_rows((m, LANE))
    ga, gx = [], []
    for q in range(BRANCH_W // LANE):
        lhs = _by_direction(x2[:, q * LANE:(q + 1) * LANE], bwd).astype(BF16)
        ga.append(_dot(lhs, wa_ref[q]))
        gx.append(_dot(lhs, wx_ref[q]))
    gr = jax.nn.sigmoid(jnp.concatenate(ga, axis=-1).reshape(tc, ROWS, BRANCH_W) + ba_ref[...])
    gi = jax.nn.sigmoid(jnp.concatenate(gx, axis=-1).reshape(tc, ROWS, BRANCH_W) + bx_ref[...])
    log_a = -LRU_C * gr * sp_ref[...]
    a_ref[...] = jnp.exp(log_a)
    b_ref[...] = jnp.sqrt(-jnp.tanh(log_a) * (jnp.exp(2.0 * log_a) + 1.0)) * gi * xcv

    def step(s, h):
        h = a_ref[s] * h + b_ref[s]
        hs_ref[s] = h
        return h

    h_ref[...] = lax.fori_loop(0, tc, step, h_ref[...])
    hf_out[...], hb_out[...] = _split_dirs(hs_ref[...])


def _lru_mixer(z, p, scan):
    n = z.shape[0]
    w = BRANCH_W
    row_spec = pl.BlockSpec((ROWS, w), lambda g: (0, 0))
    chunk = pltpu.VMEM((scan.tc, ROWS, w), F32)
    return pl.pallas_call(
        functools.partial(_lru_kernel, scan.cs),
        grid=(scan.n,),
        in_specs=scan.chunk_specs(w) + scan.halo_specs(w)
        + [pl.BlockSpec((LRU_CONV, ROWS, w), lambda g: (0, 0, 0)),
           pl.BlockSpec((LRU_CONV, ROWS, w), lambda g: (0, 0, 0)),
           pl.BlockSpec((1, w), lambda g: (0, 0)),
           pl.BlockSpec((w // LANE, 2 * LANE, LANE), lambda g: (0, 0, 0)),
           pl.BlockSpec((w // LANE, 2 * LANE, LANE), lambda g: (0, 0, 0)),
           row_spec, row_spec, row_spec],
        out_specs=[pl.BlockSpec((scan.rows, w), lambda g: (scan.fwd(g), 0)),
                   pl.BlockSpec((scan.rows, w), lambda g: (scan.bwd(g), 0))],
        out_shape=[jax.ShapeDtypeStruct((n, w), F32)] * 2,
        scratch_shapes=[chunk, chunk, chunk, pltpu.VMEM((ROWS, w), F32)],
        compiler_params=_cparams("arbitrary"),
        name="lru_mixer",
    )(z, z, z, z, z, z, p["cwf"], p["cwb"], p["cb"], p["wa"], p["wx"], p["ba"], p["bx"], p["sp"])


def _branch_out_kernel(yf_ref, yb_ref, b0_ref, b1_ref, g_ref, lnw_ref, lnb_ref, blk_ref,
                       u_ref, sf_ref, sb_ref, d_ref, gw_ref, gb_ref,
                       hf_ref, hb_ref, gate_ref, ob_ref, os_ref, od_ref):
    blk = blk_ref[...]
    y = yf_ref[...] + yb_ref[...]
    mean = _head_sum(y, blk) * (1.0 / RWKV_HEAD)
    yc = y - mean
    var = _head_sum(yc * yc, blk) * (1.0 / RWKV_HEAD)
    yn = yc * lax.rsqrt(var + RWKV_GN_EPS) * lnw_ref[...] + lnb_ref[...]
    ob_ref[...] = ((yn + (b0_ref[...] + b1_ref[...])) * g_ref[...]).astype(BF16)

    t = _gelu(u_ref[...] * d_ref[...] + sf_ref[...] + sb_ref[...])
    os_ref[...] = (t * jax.nn.sigmoid(_dot(t.astype(BF16), gw_ref[...]) + gb_ref[...])).astype(BF16)

    od_ref[...] = ((hf_ref[...] + hb_ref[...]) * _gelu(gate_ref[...])).astype(BF16)


def _branch_out(rw, s5, lru, p, tm):
    n = rw[0].shape[0]
    w = BRANCH_W
    tok = pl.BlockSpec((tm, w), lambda i: (i, 0))
    vec = pl.BlockSpec((1, w), lambda i: (0, 0))
    mat = pl.BlockSpec((w, w), lambda i: (0, 0))
    return pl.pallas_call(
        _branch_out_kernel,
        grid=(n // tm,),
        in_specs=[tok] * 5 + [vec, vec, pl.BlockSpec((3 * w, w), lambda i: (0, 0))] + [tok] * 3 + [vec, mat, vec]
        + [tok, tok, pl.BlockSpec((tm, w), lambda i: (i, 1))],
        out_specs=[tok] * 3,
        out_shape=[jax.ShapeDtypeStruct((n, w), BF16)] * 3,
        compiler_params=_cparams("parallel"),
        name="branch_out",
    )(*rw, p["ln_w"], p["ln_b"], p["blk"], *s5, p["s5_d"], p["glu_w"], p["glu_b"], *lru)


def _rows_by_direction(p):
    return jnp.repeat(p.astype(F32), HALF, axis=0)


def _block_diag(blocks):
    n, a, b = blocks.shape
    eye = jnp.eye(n, dtype=blocks.dtype)
    return jnp.einsum('nab,nm->namb', blocks, eye).reshape(n * a, n * b)


def _by_slices(fn, per_dir, n_slices, per_slice):
    return jnp.stack([jnp.stack([fn(per_dir[dd, q * per_slice:(q + 1) * per_slice]) for q in range(n_slices)])
                      for dd in range(2)])


def _rope_tables(seq, nt):
    rows = seq // GRID_W
    row = jnp.repeat(jnp.arange(rows, dtype=F32), GRID_W)
    col = jnp.tile(jnp.arange(GRID_W, dtype=F32), rows)
    n_pair = MLA_ROPE // 4
    inv = ROPE_BASE ** (-jnp.arange(n_pair, dtype=F32) / n_pair)
    ang = jnp.concatenate([row[:, None] * inv, col[:, None] * inv], -1)
    cos = jnp.concatenate([jnp.cos(ang), jnp.ones((nt, MLA_ROPE // 2), F32)], axis=0)
    sin = jnp.concatenate([jnp.sin(ang), jnp.zeros((nt, MLA_ROPE // 2), F32)], axis=0)
    n = cos.shape[0]
    z32 = jnp.zeros((n, MLA_ROPE // 2), F32)
    z64 = jnp.zeros((n, MLA_ROPE), F32)
    one = jnp.ones((n, MLA_NOPE), F32)
    z128 = jnp.zeros((n, MLA_NOPE), F32)
    qc = MLA_SCALE * jnp.concatenate([one, cos, cos, z64], axis=1)
    qs1 = MLA_SCALE * jnp.concatenate([z128, -sin, z32, z64], axis=1)
    qs2 = MLA_SCALE * jnp.concatenate([z128, z32, sin, z64], axis=1)
    rep = lambda t: jnp.repeat(t, HALF, axis=0)
    kc = rep(jnp.concatenate([cos, cos, z64], axis=1))
    ks1 = rep(jnp.concatenate([-sin, z32, z64], axis=1))
    ks2 = rep(jnp.concatenate([z32, sin, z64], axis=1))
    return qc, qs1, qs2, kc, ks1, ks2


def _time_batch_to_batch_time(tm):
    nt = tm // HALF
    dst = jnp.arange(tm)
    src = (dst % nt) * HALF + dst // nt
    return (src[:, None] == jnp.arange(tm)[None, :]).astype(BF16)


def _s5_discretise(lam_re, lam_im, log_dt, b_re, b_im):
    lre = jnp.minimum(lam_re.astype(F32), -1e-4)
    lim = lam_im.astype(F32)
    dt = jnp.exp(log_dt.astype(F32))[..., None]
    mag = jnp.exp(lre * dt)
    ar, ai = mag * jnp.cos(lim * dt), mag * jnp.sin(lim * dt)
    den = lre * lre + lim * lim
    nr, ni = ar - 1.0, ai
    cr = (nr * lre + ni * lim) / den
    ci = (ni * lre - nr * lim) / den
    b_re, b_im = b_re.astype(F32), b_im.astype(F32)
    br = cr[..., None] * b_re - ci[..., None] * b_im
    bi = cr[..., None] * b_im + ci[..., None] * b_re
    return ar, ai, br, bi


def kernel(x, c, ctx, c_ctx, ada_w, ada_b, norm_w, ffn_w1, ffn_w3, ffn_w2, w_in, gate_b, mla_q_norm, mla_w_uq, mla_kv_norm, mla_w_ukv, rwkv_mu, rwkv_w0, rwkv_w2, rwkv_a0, rwkv_a2, rwkv_g2, rwkv_kk, rwkv_ka, rwkv_rk, rwkv_ln_w, rwkv_ln_b, s5_lam_re, s5_lam_im, s5_log_dt, s5_b_re, s5_b_im, s5_c_re, s5_c_im, s5_d, s5_glu_w, s5_glu_b, lru_conv_w, lru_conv_b, lru_wa, lru_ba, lru_wx, lru_bx, lru_lam, w_branch, w_out):
    batch, seq, d = x.shape
    ctx_len = ctx.shape[1]
    depth = ada_w.shape[0]
    assert batch == HALF and d == D_MODEL
    n_lat, n_ctx = batch * seq, batch * ctx_len
    w = BRANCH_W

    rows_big = _Rows(n_lat, n_ctx, 512)
    rows = _Rows(n_lat, n_ctx, 256)
    scan = _Scan(seq, ctx_len, 64)

    w1 = _cast_bf16(ffn_w1, 512)
    w3 = _cast_bf16(ffn_w3, 512)
    w2 = _cast_bf16(ffn_w2, 1024)
    perm = jnp.concatenate([jnp.arange(0, MLA_ROPE, 2), jnp.arange(1, MLA_ROPE, 2)])
    rope_perm = (jnp.arange(MLA_ROPE)[:, None] == perm[None, :]).astype(BF16)
    w_mix, w_gate = _pack_w_in(w_in, rope_perm)
    w_br = w_branch.astype(BF16)
    w_o = w_out.astype(BF16)
    nw = norm_w.reshape(depth, 6, 1, d)
    gb = gate_b.reshape(depth, N_BRANCH, 1, d)

    hd = MLA_NOPE + MLA_ROPE
    qcols = jnp.concatenate([jnp.arange(MLA_NOPE), MLA_NOPE + perm])
    wq = mla_w_uq.reshape(depth, MLA_Q_LORA, MLA_HEADS, hd)[..., qcols]
    wq = jnp.concatenate([wq, jnp.zeros((depth, MLA_Q_LORA, MLA_HEADS, MLA_HEAD_PAD - hd), F32)], axis=-1)
    wq = wq.reshape(depth, MLA_Q_LORA, MLA_HEADS * MLA_HEAD_PAD).astype(BF16)
    wkv = mla_w_ukv.astype(BF16)
    qn = mla_q_norm.reshape(depth, 1, MLA_Q_LORA)
    kvn = mla_kv_norm.reshape(depth, 1, MLA_KV_LORA)
    tabs = _rope_tables(seq, rows.tm // HALF)
    perm_bt = _time_batch_to_batch_time(rows.tm)
    perm_tb = _time_batch_to_batch_time(rows_big.tm).T

    blk = _block_diag(jnp.ones((RWKV_HEADS, RWKV_HEAD, RWKV_HEAD), BF16))
    blk = jnp.concatenate([blk, blk, blk], axis=0)

    cc = jnp.concatenate([c, c_ctx[None], jnp.zeros((ROWS - batch - 1, d), F32)], axis=0)
    mods = _modulation(cc, ada_w, ada_b).reshape(depth, ROWS, N_MOD, d)
    m_lat = jnp.concatenate([mods[:, :batch], mods[:, :batch]], axis=1).transpose(0, 2, 1, 3)
    m_ctx = jnp.broadcast_to(mods[:, batch][:, :, None, :], (depth, N_MOD, ROWS, d))
    mods = jnp.stack([m_lat, m_ctx], axis=1)

    xs = jnp.concatenate([x.transpose(1, 0, 2).reshape(n_lat, d), ctx.transpose(1, 0, 2).reshape(n_ctx, d)], axis=0)

    for l in range(depth):
        xs = _ffn_half(xs, mods, nw, w1, w3, w2, rows_big, l, 0, 0, 0, 1)
        z_mla, z_rwkv, z_s5, z_lru = _in_projection(xs, mods, nw, w_mix, rows, l)

        q, k, v = _mla_project(z_mla, qn, kvn, wq, wkv, perm_bt, tabs, rows, l)
        ya = _mla_attention(q, k, v, seq, ctx_len, 256)

        mu = rwkv_mu[l]
        wl = jnp.concatenate([
            jnp.concatenate([jnp.concatenate([rwkv_w2[l, dd], jnp.zeros((RWKV_DECAY_LORA, w), F32)], axis=1),
                             jnp.concatenate([jnp.zeros((RWKV_A_LORA, w), F32), rwkv_a2[l, dd]], axis=1)], axis=0)
            for dd in range(2)], axis=0).astype(BF16)
        rp = {"mua": _rows_by_direction(mu), "mub": _rows_by_direction(mu[::-1]),
              "w0": _rows_by_direction(rwkv_w0[l]), "a0": _rows_by_direction(rwkv_a0[l]),
              "kk": rwkv_kk[l].reshape(1, w), "ka": rwkv_ka[l].reshape(1, w),
              "rk": rwkv_rk[l].reshape(1, w), "wl": wl, "g2": rwkv_g2[l].astype(BF16), "blk": blk}
        rw = _rwkv_mixer(z_rwkv, rp, scan)

        ar, ai, br, bi = _s5_discretise(s5_lam_re[l], s5_lam_im[l], s5_log_dt[l], s5_b_re[l], s5_b_im[l])
        gps = S5_GROUPS // S5_SLICES
        tdiag = lambda blocks: _block_diag(blocks.astype(F32).transpose(0, 2, 1))
        drive = lambda bm: (lambda bd: jnp.concatenate([bd[0], bd[1]], axis=1).astype(BF16))(
            _by_slices(tdiag, bm, S5_SLICES, gps))
        cre = _by_slices(tdiag, s5_c_re[l], S5_SLICES, gps)
        cim = _by_slices(tdiag, s5_c_im[l], S5_SLICES, gps)
        sp = {"ar": _rows_by_direction(ar.reshape(2, S5_W)), "ai": _rows_by_direction(ai.reshape(2, S5_W)),
              "wdr": drive(br), "wdi": drive(bi),
              "wc": jnp.concatenate([jnp.concatenate([cre[0], cre[1]], axis=2),
                                     jnp.concatenate([-cim[0], -cim[1]], axis=2)], axis=1).astype(BF16)}
        sf, sb = _s5_mixer(z_s5, sp, scan)

        diag = lambda blocks: _block_diag(blocks.astype(F32))
        lru_gate = lambda wm: (lambda bd: jnp.concatenate([bd[0], bd[1]], axis=1).astype(BF16))(
            _by_slices(diag, wm, w // LANE, LANE // LRU_BLOCK))
        fwd_rows = (jnp.arange(ROWS) < HALF).astype(F32)[None, :, None]
        cw = jnp.broadcast_to(lru_conv_w[l].astype(F32)[:, None, :], (LRU_CONV, ROWS, w))
        lp = {"cwf": cw * fwd_rows, "cwb": cw * (1.0 - fwd_rows), "cb": lru_conv_b[l].reshape(1, w),
              "wa": lru_gate(lru_wa[l]), "wx": lru_gate(lru_wx[l]),
              "ba": _rows_by_direction(lru_ba[l]), "bx": _rows_by_direction(lru_bx[l]),
              "sp": _rows_by_direction(jax.nn.softplus(-lru_lam[l].astype(F32)))}
        hf, hb = _lru_mixer(z_lru, lp, scan)

        bp = {"ln_w": rwkv_ln_w[l].reshape(1, w), "ln_b": rwkv_ln_b[l].reshape(1, w), "blk": blk,
              "s5_d": s5_d[l].reshape(1, w), "glu_w": s5_glu_w[l].astype(BF16), "glu_b": s5_glu_b[l].reshape(1, w)}
        yb_o, ys_o, yd_o = _branch_out(rw, (z_s5, sf, sb), (hf, hb, z_lru), bp, rows.tm)

        m = _gated_sum(xs, mods, nw, ya, yb_o, ys_o, yd_o, perm_tb, w_gate, gb, w_br, rows_big, l)
        xs = _out_projection(m, w_o, xs, mods, nw, rows_big, l)
        xs = _ffn_half(xs, mods, nw, w1, w3, w2, rows_big, l, 1, 2, 4, 5)

    return xs[:n_lat].reshape(seq, batch, d).transpose(1, 0, 2)
```
